```python
import jax, jax.numpy as jnp
from jax import lax
import numpy as np

D_MODEL = 1024
BATCH = 8
SEQ = 4096
DEPTH = 1
DEC_BATCH = 4
DEC_SEQ = 8192
PAST_LEN = 128

HEAD_DIM = 64
N_Q_HEADS = 8
N_KV_HEADS = 2
GQA_GROUP = N_Q_HEADS // N_KV_HEADS
ATTN_WIDTH = N_Q_HEADS * HEAD_DIM
KV_WIDTH = N_KV_HEADS * HEAD_DIM
WINDOW = 128
ATTN_BLOCK = 128
ROPE_THETA = 500000.0
ROPE_DIM = HEAD_DIM // 4
GMLP_WIDTH = D_MODEL // 2
GMLP_GROUPS = 8
GMLP_GROUP_DIM = GMLP_WIDTH // GMLP_GROUPS
GMLP_CHUNK = 128
N_BRANCH = 2
IN_WIDTH = ATTN_WIDTH + 2 * KV_WIDTH + 2 * GMLP_WIDTH + N_BRANCH * D_MODEL
N_EXPERTS = 32
TOP_K = 4
D_FF = D_MODEL
SWIGLU_ALPHA = 1.702
SWIGLU_LIMIT = 7.0
MOE_BLOCK = 256
PLE_DIM = 256
EPS = 1e-6

kernel_name = "hybrid_window_gqa_gmlp_moe_encoder"


def rms_norm(x, gain):
    xf = x.astype(jnp.float32)
    y = xf * lax.rsqrt(jnp.mean(xf * xf, axis=-1, keepdims=True) + EPS)
    return (y * gain.astype(jnp.float32)).astype(x.dtype)


def partial_rope(x, pos):
    half = ROPE_DIM // 2
    inv_freq = jnp.power(ROPE_THETA, -jnp.arange(half, dtype=jnp.float32) * (2.0 / ROPE_DIM))
    ang = pos.astype(jnp.float32)[:, None] * inv_freq[None, :]
    cos = jnp.cos(ang)[None, :, None, :]
    sin = jnp.sin(ang)[None, :, None, :]
    xf = x.astype(jnp.float32)
    x1 = xf[..., :half]
    x2 = xf[..., half:ROPE_DIM]
    out = jnp.concatenate([x1 * cos - x2 * sin, x2 * cos + x1 * sin, xf[..., ROPE_DIM:]], axis=-1)
    return out.astype(x.dtype)


def windowed_sink_attention(q, k, v, sink):
    B, S = q.shape[0], q.shape[1]
    nb = S // ATTN_BLOCK
    pad = ((0, 0), (ATTN_BLOCK, ATTN_BLOCK), (0, 0), (0, 0))
    kp = jnp.pad(k, pad)
    vp = jnp.pad(v, pad)
    qb = q.reshape(B, nb, ATTN_BLOCK, N_KV_HEADS, GQA_GROUP, HEAD_DIM).transpose(1, 0, 2, 3, 4, 5)
    scale = HEAD_DIM ** -0.5
    rel = (jnp.arange(ATTN_BLOCK)[:, None] + ATTN_BLOCK) - jnp.arange(3 * ATTN_BLOCK)[None, :]
    band = jnp.abs(rel) <= WINDOW
    sink_f = sink.astype(jnp.float32)[None, :, :, None, None]

    def one_block(args):
        i, qi = args
        start = i * ATTN_BLOCK
        ki = lax.dynamic_slice_in_dim(kp, start, 3 * ATTN_BLOCK, axis=1)
        vi = lax.dynamic_slice_in_dim(vp, start, 3 * ATTN_BLOCK, axis=1)
        kpos = start - ATTN_BLOCK + jnp.arange(3 * ATTN_BLOCK)
        valid = band & ((kpos >= 0) & (kpos < S))[None, :]
        s = jnp.einsum('bqhgd,bkhd->bhgqk', qi, ki, preferred_element_type=jnp.float32) * scale
        s = jnp.where(valid, s, -jnp.inf)
        m = jnp.maximum(jnp.max(s, axis=-1, keepdims=True), sink_f)
        p = jnp.exp(s - m)
        denom = jnp.sum(p, axis=-1, keepdims=True) + jnp.exp(sink_f - m)
        return jnp.einsum('bhgqk,bkhd->bqhgd', (p / denom).astype(vi.dtype), vi)

    out = lax.map(one_block, (jnp.arange(nb), qb))
    return out.transpose(1, 0, 2, 3, 4, 5).reshape(B, S, ATTN_WIDTH)


def chunked_spatial_gating(u, vg, v_gain, w_s, b_s):
    B, S = u.shape[0], u.shape[1]
    nc = S // GMLP_CHUNK
    u = jax.nn.gelu(u, approximate=False)
    vg = jax.nn.gelu(vg, approximate=False).reshape(B, nc, GMLP_CHUNK, GMLP_GROUPS, GMLP_GROUP_DIM)
    vg = rms_norm(vg, v_gain)
    mixed = jnp.einsum('gpq,bcqgd->bcpgd', w_s, vg) + b_s.T[None, None, :, :, None]
    return u * mixed.reshape(B, S, GMLP_WIDTH)


def moe_clamped_swiglu(x, w_router, b_router, w_gate_up, b_gate_up, w_down, b_down):
    N, D = x.shape
    logits = (x @ w_router).astype(jnp.float32) + b_router.astype(jnp.float32)
    top_val, top_idx = lax.top_k(logits, TOP_K)
    gates = jax.nn.softmax(top_val, axis=-1)
    flat_e = top_idx.reshape(-1).astype(jnp.int32)
    flat_g = gates.reshape(-1)
    order = jnp.argsort(flat_e)
    sorted_e = flat_e[order]
    counts = jnp.bincount(flat_e, length=N_EXPERTS)
    padded = (counts + MOE_BLOCK - 1) // MOE_BLOCK * MOE_BLOCK
    starts = jnp.cumsum(counts) - counts
    pends = jnp.cumsum(padded)
    pstarts = pends - padded
    dest = pstarts[sorted_e] + jnp.arange(N * TOP_K, dtype=jnp.int32) - starts[sorted_e]
    n_slots = -(-(N * TOP_K) // MOE_BLOCK) * MOE_BLOCK + N_EXPERTS * MOE_BLOCK
    n_blocks = n_slots // MOE_BLOCK
    slot_tok = jnp.full((n_slots,), N, jnp.int32).at[dest].set((order // TOP_K).astype(jnp.int32))
    slot_gate = jnp.zeros((n_slots,), jnp.float32).at[dest].set(flat_g[order])
    block_e = jnp.clip(jnp.searchsorted(pends, jnp.arange(n_blocks) * MOE_BLOCK, side='right'), 0, N_EXPERTS - 1)
    x_pad = jnp.concatenate([x, jnp.zeros((1, D), x.dtype)], axis=0)
    xb = x_pad[slot_tok].reshape(n_blocks, MOE_BLOCK, D)

    def expert_block(args):
        e, xe = args
        h = xe @ w_gate_up[e] + b_gate_up[e]
        gate = jnp.minimum(h[:, 0::2], SWIGLU_LIMIT)
        up = jnp.clip(h[:, 1::2], -SWIGLU_LIMIT, SWIGLU_LIMIT)
        act = (up + 1.0) * (gate * jax.nn.sigmoid(SWIGLU_ALPHA * gate))
        return act @ w_down[e] + b_down[e]

    yb = lax.map(expert_block, (block_e, xb)).reshape(n_slots, D)
    out = jnp.zeros_like(x_pad).at[slot_tok].add(yb * slot_gate[:, None].astype(yb.dtype))
    return out[:N]


def encoder_layer(h, p_l, norm_mix, w_in, q_gain, k_gain, sink, v_gain, w_s, b_s, w_branch, w_out,
                  norm_ffn, w_router, b_router, w_gate_up, b_gate_up, w_down, b_down,
                  norm_ple, w_ple_gate, w_ple_proj):
    B, S, D = h.shape
    pos = jnp.arange(S)
    xn = rms_norm(h, norm_mix)
    z = xn @ w_in
    splits = np.cumsum([ATTN_WIDTH, KV_WIDTH, KV_WIDTH, GMLP_WIDTH, GMLP_WIDTH]).tolist()
    q, k, v, u, vg, gl = jnp.split(z, splits, axis=-1)
    q = partial_rope(rms_norm(q.reshape(B, S, N_Q_HEADS, HEAD_DIM), q_gain), pos)
    k = partial_rope(rms_norm(k.reshape(B, S, N_KV_HEADS, HEAD_DIM), k_gain), pos)
    q = q.reshape(B, S, N_KV_HEADS, GQA_GROUP, HEAD_DIM)
    v = v.reshape(B, S, N_KV_HEADS, HEAD_DIM)
    attn = windowed_sink_attention(q, k, v, sink.reshape(N_KV_HEADS, GQA_GROUP))
    gm = chunked_spatial_gating(u, vg, v_gain, w_s, b_s)
    branches = jnp.stack([attn, gm], axis=2)
    y = jnp.einsum('bsnc,ncd->bsnd', branches, w_branch)
    g = jax.nn.sigmoid(gl.reshape(B, S, N_BRANCH, D))
    h = h + jnp.sum(g * y, axis=2) @ w_out
    hn = rms_norm(h, norm_ffn)
    h = h + moe_clamped_swiglu(hn.reshape(B * S, D), w_router, b_router, w_gate_up, b_gate_up,
                               w_down, b_down).reshape(B, S, D)
    h = h + jax.nn.sigmoid(rms_norm(h, norm_ple) @ w_ple_gate) * (p_l @ w_ple_proj)
    return h


def setup_inputs(seed: int = 0) -> dict:
    key = jax.random.key(seed)
    ks = jax.random.split(key, 24)

    def nrm(k, shape, scale):
        return jax.random.normal(k, shape, jnp.float32) * scale

    L = DEPTH
    return {
        "x_prompt": nrm(ks[0], (BATCH, SEQ, D_MODEL), 1.0),
        "x_sample": nrm(ks[1], (DEC_BATCH, DEC_SEQ, D_MODEL), 1.0),
        "p_prompt": nrm(ks[2], (DEPTH, BATCH, SEQ, PLE_DIM), 1.0),
        "p_sample": nrm(ks[3], (DEPTH, DEC_BATCH, DEC_SEQ, PLE_DIM), 1.0),
        "norm_mix": 1.0 + nrm(ks[4], (L, D_MODEL), 0.02),
        "w_in": nrm(ks[5], (L, D_MODEL, IN_WIDTH), D_MODEL ** -0.5),
        "q_gain": 1.0 + nrm(ks[6], (L, HEAD_DIM), 0.02),
        "k_gain": 1.0 + nrm(ks[7], (L, HEAD_DIM), 0.02),
        "attn_sink": nrm(ks[8], (L, N_Q_HEADS), 0.5),
        "gmlp_v_gain": 1.0 + nrm(ks[9], (L, GMLP_GROUPS, GMLP_GROUP_DIM), 0.02),
        "gmlp_w_s": nrm(ks[10], (L, GMLP_GROUPS, GMLP_CHUNK, GMLP_CHUNK), GMLP_CHUNK ** -0.5),
        "gmlp_b_s": 1.0 + nrm(ks[11], (L, GMLP_GROUPS, GMLP_CHUNK), 0.1),
        "w_branch": nrm(ks[12], (L, N_BRANCH, ATTN_WIDTH, D_MODEL), ATTN_WIDTH ** -0.5),
        "w_out": nrm(ks[13], (L, D_MODEL, D_MODEL), D_MODEL ** -0.5),
        "norm_ffn": 1.0 + nrm(ks[14], (L, D_MODEL), 0.02),
        "w_router": nrm(ks[15], (L, D_MODEL, N_EXPERTS), D_MODEL ** -0.5),
        "b_router": nrm(ks[16], (L, N_EXPERTS), 0.01),
        "w_gate_up": nrm(ks[17], (L, N_EXPERTS, D_MODEL, 2 * D_FF), D_MODEL ** -0.5),
        "b_gate_up": nrm(ks[18], (L, N_EXPERTS, 2 * D_FF), 0.02),
        "w_down": nrm(ks[19], (L, N_EXPERTS, D_FF, D_MODEL), D_FF ** -0.5),
        "b_down": nrm(ks[20], (L, N_EXPERTS, D_MODEL), 0.02),
        "norm_ple": 1.0 + nrm(ks[21], (L, D_MODEL), 0.02),
        "w_ple_gate": nrm(ks[22], (L, D_MODEL, D_MODEL), D_MODEL ** -0.5),
        "w_ple_proj": nrm(ks[23], (L, PLE_DIM, D_MODEL), PLE_DIM ** -0.5),
    }


def reference(x_prompt, x_sample, p_prompt, p_sample, norm_mix, w_in, q_gain, k_gain, attn_sink,
              gmlp_v_gain, gmlp_w_s, gmlp_b_s, w_branch, w_out, norm_ffn, w_router, b_router,
              w_gate_up, b_gate_up, w_down, b_down, norm_ple, w_ple_gate, w_ple_proj):
    hp = x_prompt
    hs = x_sample
    for i in range(DEPTH):
        params = (norm_mix[i], w_in[i], q_gain[i], k_gain[i], attn_sink[i], gmlp_v_gain[i],
                  gmlp_w_s[i], gmlp_b_s[i], w_branch[i], w_out[i], norm_ffn[i], w_router[i],
                  b_router[i], w_gate_up[i], b_gate_up[i], w_down[i], b_down[i], norm_ple[i],
                  w_ple_gate[i], w_ple_proj[i])
        hp = encoder_layer(hp, p_prompt[i], *params)
        hs = encoder_layer(hs, p_sample[i], *params)
    y_prompt = hp.astype(x_prompt.dtype)
    y_sample = hs.astype(x_sample.dtype)
    return (y_prompt, y_sample)
```

```python
import functools

import jax
import jax.numpy as jnp
import numpy as np
from jax import lax
from jax.experimental import pallas as pl
from jax.experimental.pallas import tpu as pltpu

D_MODEL = 1024
HEAD_DIM = 64
N_Q_HEADS = 8
N_KV_HEADS = 2
GQA_GROUP = N_Q_HEADS // N_KV_HEADS
ATTN_WIDTH = N_Q_HEADS * HEAD_DIM
KV_WIDTH = N_KV_HEADS * HEAD_DIM
WINDOW = 128
ATTN_BLOCK = 128
ROPE_THETA = 500000.0
ROPE_DIM = HEAD_DIM // 4
GMLP_WIDTH = D_MODEL // 2
GMLP_GROUPS = 8
GMLP_GROUP_DIM = GMLP_WIDTH // GMLP_GROUPS
GMLP_CHUNK = 128
N_BRANCH = 2
IN_WIDTH = ATTN_WIDTH + 2 * KV_WIDTH + 2 * GMLP_WIDTH + N_BRANCH * D_MODEL
N_EXPERTS = 32
TOP_K = 4
D_FF = D_MODEL
SWIGLU_ALPHA = 1.702
SWIGLU_LIMIT = 7.0
PLE_DIM = 256
EPS = 1e-6

Q_OFF = 0
K_OFF = ATTN_WIDTH
V_OFF = K_OFF + KV_WIDTH
U_OFF = V_OFF + KV_WIDTH
VG_OFF = U_OFF + GMLP_WIDTH
GL_OFF = VG_OFF + GMLP_WIDTH

LANES = 128
MIX_TILE = 512
MOE_BLOCK = 256
CMB_TILE = 256
ROUTER_PAD = 128
VMEM_LIMIT = 56 * 1024 * 1024

BF16 = jnp.bfloat16
F32 = jnp.float32


def _rms(x, gain):
    return x * lax.rsqrt(jnp.mean(x * x, axis=-1, keepdims=True) + EPS) * gain


def _gelu(x):
    return 0.5 * x * (1.0 + lax.erf(x * np.float32(np.sqrt(0.5))))


def _head_rms(x, blockdiag, gain):
    ms = jnp.dot((x * x).astype(BF16), blockdiag, preferred_element_type=F32)
    return x * lax.rsqrt(ms + EPS) * gain


def _rope(x, cos, sin_lo, sin_hi):
    w = x.shape[-1]
    return x * cos + pltpu.roll(x, w - ROPE_DIM // 2, 1) * sin_lo + pltpu.roll(x, ROPE_DIM // 2, 1) * sin_hi


def _tile_lanes(t, reps):
    return t if reps == 1 else jnp.concatenate([t] * reps, axis=-1)


def _mixer_kernel(x_ref, xp_ref, xn_ref, rp_ref, rpp_ref, rpn_ref,
                  nmix_ref, win_ref, qg_ref, kg_ref, sink_ref, vgain_ref, bdq_ref,
                  wcat_ref, bias_ref, wbr_ref, wout_ref, nffn_ref, wr_ref, br_ref,
                  h1_ref, idx_ref, gate_ref,
                  z_ref, q_ref, k_ref, v_ref, attn_ref, gm_ref, *, n_blocks_seq):
    ts = MIX_TILE
    i = pl.program_id(1)
    x = x_ref[0]
    xn = _rms(x, nmix_ref[...]).astype(BF16)
    z_ref[...] = jnp.dot(xn, win_ref[...], preferred_element_type=F32)

    cos = rp_ref[:, 0:LANES]
    sin_lo = rp_ref[:, LANES:2 * LANES]
    sin_hi = rp_ref[:, 2 * LANES:3 * LANES]
    bdq = bdq_ref[...]
    bdk = bdq_ref[0:KV_WIDTH, 0:KV_WIDTH]

    q = _head_rms(z_ref[:, Q_OFF:Q_OFF + ATTN_WIDTH], bdq, qg_ref[...])
    reps = ATTN_WIDTH // LANES
    q = _rope(q, _tile_lanes(cos, reps), _tile_lanes(sin_lo, reps), _tile_lanes(sin_hi, reps))
    q_ref[...] = (q * (HEAD_DIM ** -0.5)).astype(BF16)

    k = _head_rms(z_ref[:, K_OFF:K_OFF + KV_WIDTH], bdk, kg_ref[...])
    k_ref[ATTN_BLOCK:ATTN_BLOCK + ts, :] = _rope(k, cos, sin_lo, sin_hi).astype(BF16)
    v_ref[ATTN_BLOCK:ATTN_BLOCK + ts, :] = z_ref[:, V_OFF:V_OFF + KV_WIDTH].astype(BF16)

    xh = jnp.concatenate([xp_ref[0], xn_ref[0]], axis=0)
    xhn = _rms(xh, nmix_ref[...]).astype(BF16)
    zh = jnp.dot(xhn, win_ref[:, K_OFF:K_OFF + 2 * KV_WIDTH], preferred_element_type=F32)
    kh = _head_rms(zh[:, 0:KV_WIDTH], bdk, kg_ref[...])
    rph = jnp.concatenate([rpp_ref[...], rpn_ref[...]], axis=0)
    kh = _rope(kh, rph[:, 0:LANES], rph[:, LANES:2 * LANES], rph[:, 2 * LANES:3 * LANES]).astype(BF16)
    vh = zh[:, KV_WIDTH:2 * KV_WIDTH].astype(BF16)
    k_ref[0:ATTN_BLOCK, :] = kh[0:ATTN_BLOCK]
    k_ref[ATTN_BLOCK + ts:2 * ATTN_BLOCK + ts, :] = kh[ATTN_BLOCK:]
    v_ref[0:ATTN_BLOCK, :] = vh[0:ATTN_BLOCK]
    v_ref[ATTN_BLOCK + ts:2 * ATTN_BLOCK + ts, :] = vh[ATTN_BLOCK:]

    rows = GQA_GROUP * ATTN_BLOCK
    keys = 3 * ATTN_BLOCK
    r = lax.broadcasted_iota(jnp.int32, (rows, keys), 0) % ATTN_BLOCK
    c = lax.broadcasted_iota(jnp.int32, (rows, keys), 1)
    band = (c >= r) & (c <= r + 2 * WINDOW)
    for qb in range(ts // ATTN_BLOCK):
        gb = i * (ts // ATTN_BLOCK) + qb
        lo = jnp.where(gb == 0, ATTN_BLOCK, 0)
        hi = jnp.where(gb == n_blocks_seq - 1, 2 * ATTN_BLOCK, keys)
        valid = band & (c >= lo) & (c < hi)
        r0 = qb * ATTN_BLOCK
        for j in range(N_KV_HEADS):
            q4 = jnp.concatenate(
                [q_ref[r0:r0 + ATTN_BLOCK, (GQA_GROUP * j + g) * HEAD_DIM:(GQA_GROUP * j + g + 1) * HEAD_DIM]
                 for g in range(GQA_GROUP)], axis=0)
            kw = k_ref[r0:r0 + keys, j * HEAD_DIM:(j + 1) * HEAD_DIM]
            vw = v_ref[r0:r0 + keys, j * HEAD_DIM:(j + 1) * HEAD_DIM]
            s = lax.dot_general(q4, kw, (((1,), (1,)), ((), ())), preferred_element_type=F32)
            s = jnp.where(valid, s, -jnp.inf)
            sk = sink_ref[j]
            m = jnp.maximum(jnp.max(s, axis=-1, keepdims=True), sk)
            p = jnp.exp(s - m)
            denom = jnp.sum(p, axis=-1, keepdims=True) + jnp.exp(sk - m)
            o = jnp.dot(p.astype(BF16), vw, preferred_element_type=F32) / denom
            for g in range(GQA_GROUP):
                h = GQA_GROUP * j + g
                attn_ref[r0:r0 + ATTN_BLOCK, h * HEAD_DIM:(h + 1) * HEAD_DIM] = (
                    o[g * ATTN_BLOCK:(g + 1) * ATTN_BLOCK].astype(BF16))

    half = GMLP_WIDTH // 2
    gpm = half // GMLP_GROUP_DIM
    lane_grp = lax.broadcasted_iota(jnp.int32, (GMLP_CHUNK, half), 1) // GMLP_GROUP_DIM
    for ch in range(ts // GMLP_CHUNK):
        c0 = ch * GMLP_CHUNK
        vg = _gelu(z_ref[c0:c0 + GMLP_CHUNK, VG_OFF:VG_OFF + GMLP_WIDTH])
        vgn = _head_rms(vg, bdq, vgain_ref[...])
        mixed = []
        for nt in range(2):
            part = vgn[:, nt * half:(nt + 1) * half]
            vexp = jnp.concatenate(
                [jnp.where(lane_grp == gl, part, 0.0).astype(BF16) for gl in range(gpm)], axis=0)
            wpart = wcat_ref[:, nt * gpm * GMLP_CHUNK:(nt + 1) * gpm * GMLP_CHUNK]
            mixed.append(jnp.dot(wpart, vexp, preferred_element_type=F32))
        mixed = jnp.concatenate(mixed, axis=-1) + bias_ref[...]
        u = _gelu(z_ref[c0:c0 + GMLP_CHUNK, U_OFF:U_OFF + GMLP_WIDTH])
        gm_ref[c0:c0 + GMLP_CHUNK, :] = (u * mixed).astype(BF16)

    ya = jnp.dot(attn_ref[...], wbr_ref[0], preferred_element_type=F32)
    yg = jnp.dot(gm_ref[...], wbr_ref[1], preferred_element_type=F32)
    ga = jax.nn.sigmoid(z_ref[:, GL_OFF:GL_OFF + D_MODEL])
    gg = jax.nn.sigmoid(z_ref[:, GL_OFF + D_MODEL:GL_OFF + 2 * D_MODEL])
    merged = (ga * ya + gg * yg).astype(BF16)
    h1 = x + jnp.dot(merged, wout_ref[...], preferred_element_type=F32)
    h1_ref[0] = h1

    hn = _rms(h1, nffn_ref[...])
    hn_hi = hn.astype(BF16)
    hn_lo = (hn - hn_hi.astype(F32)).astype(BF16)
    w_hi = wr_ref[0]
    w_lo = wr_ref[1]
    logits = (jnp.dot(hn_hi, w_hi, preferred_element_type=F32)
              + jnp.dot(hn_hi, w_lo, preferred_element_type=F32)
              + jnp.dot(hn_lo, w_hi, preferred_element_type=F32)) + br_ref[...]
    lt = jnp.transpose(logits)[0:N_EXPERTS, :]
    eid = lax.broadcasted_iota(jnp.int32, lt.shape, 0)
    vals, ids = [], []
    for _ in range(TOP_K):
        mx = jnp.max(lt, axis=0, keepdims=True)
        am = jnp.min(jnp.where(lt == mx, eid, N_EXPERTS), axis=0, keepdims=True)
        vals.append(mx)
        ids.append(am)
        lt = jnp.where(eid == am, -jnp.inf, lt)
    ex = [jnp.exp(v - vals[0]) for v in vals]
    tot = ex[0] + ex[1] + ex[2] + ex[3]
    zero_i = jnp.zeros_like(ids[0])
    idx_ref[0] = jnp.concatenate(ids + [zero_i] * (8 - TOP_K), axis=0)
    g8 = jnp.concatenate([e / tot for e in ex] + [jnp.zeros_like(tot)] * (8 - TOP_K), axis=0)
    gates = jnp.concatenate([g8, jnp.zeros((LANES - 8, ts), F32)], axis=0)
    gate_ref[0] = jnp.transpose(gates)


def _mixer_call(x, rope_tab, prm):
    b, s, d = x.shape
    ts = MIX_TILE
    nt = s // ts
    nb = s // ATTN_BLOCK
    per = ts // ATTN_BLOCK
    const2 = lambda bi, i: (0, 0)
    const3 = lambda bi, i: (0, 0, 0)

    def wspec(arr):
        return pl.BlockSpec(arr.shape, const2 if arr.ndim == 2 else const3)

    in_specs = [
        pl.BlockSpec((1, ts, d), lambda bi, i: (bi, i, 0)),
        pl.BlockSpec((1, ATTN_BLOCK, d), lambda bi, i: (bi, jnp.maximum(i * per - 1, 0), 0)),
        pl.BlockSpec((1, ATTN_BLOCK, d), lambda bi, i: (bi, jnp.minimum((i + 1) * per, nb - 1), 0)),
        pl.BlockSpec((ts, 3 * LANES), lambda bi, i: (i, 0)),
        pl.BlockSpec((ATTN_BLOCK, 3 * LANES), lambda bi, i: (jnp.maximum(i * per - 1, 0), 0)),
        pl.BlockSpec((ATTN_BLOCK, 3 * LANES), lambda bi, i: (jnp.minimum((i + 1) * per, nb - 1), 0)),
    ] + [wspec(a) for a in prm]
    out_shape = [
        jax.ShapeDtypeStruct((b, s, d), F32),
        jax.ShapeDtypeStruct((b, 8, s), jnp.int32),
        jax.ShapeDtypeStruct((b, s, LANES), F32),
    ]
    out_specs = [
        pl.BlockSpec((1, ts, d), lambda bi, i: (bi, i, 0)),
        pl.BlockSpec((1, 8, ts), lambda bi, i: (bi, 0, i)),
        pl.BlockSpec((1, ts, LANES), lambda bi, i: (bi, i, 0)),
    ]
    scratch = [
        pltpu.VMEM((ts, IN_WIDTH), F32),
        pltpu.VMEM((ts, ATTN_WIDTH), BF16),
        pltpu.VMEM((ts + 2 * ATTN_BLOCK, KV_WIDTH), BF16),
        pltpu.VMEM((ts + 2 * ATTN_BLOCK, KV_WIDTH), BF16),
        pltpu.VMEM((ts, ATTN_WIDTH), BF16),
        pltpu.VMEM((ts, GMLP_WIDTH), BF16),
    ]
    return pl.pallas_call(
        functools.partial(_mixer_kernel, n_blocks_seq=nb),
        grid=(b, nt),
        in_specs=in_specs,
        out_specs=out_specs,
        out_shape=out_shape,
        scratch_shapes=scratch,
        compiler_params=pltpu.CompilerParams(
            dimension_semantics=("arbitrary", "arbitrary"), vmem_limit_bytes=VMEM_LIMIT),
        name="mixer",
    )(x, x, x, rope_tab, rope_tab, rope_tab, *prm)


def _gather_rows(idx_ref, src_hbm, dst, sem, n_rows, start):
    def body(r, carry):
        tok = idx_ref[0, 0, r]
        cp = pltpu.make_async_copy(src_hbm.at[pl.ds(tok, 1), :], dst.at[pl.ds(r, 1), :], sem)
        if start:
            cp.start()
        else:
            cp.wait()
        return carry
    lax.fori_loop(0, n_rows, body, 0, unroll=8)


def _moe_kernel(be_ref, cur_ref, nxt_ref, h1_hbm, nffn_ref, wgu_ref, bgu_ref, wd_ref, bd_ref,
                y_ref, xbuf, sem):
    i = pl.program_id(0)
    n = pl.num_programs(0)
    slot = i % 2

    @pl.when(i == 0)
    def _():
        _gather_rows(cur_ref, h1_hbm, xbuf.at[0], sem.at[0], MOE_BLOCK, True)

    @pl.when(i + 1 < n)
    def _():
        _gather_rows(nxt_ref, h1_hbm, xbuf.at[1 - slot], sem.at[1 - slot], MOE_BLOCK, True)

    _gather_rows(cur_ref, h1_hbm, xbuf.at[slot], sem.at[slot], MOE_BLOCK, False)

    xe = _rms(xbuf[slot], nffn_ref[...]).astype(BF16)
    h = jnp.dot(xe, wgu_ref[0], preferred_element_type=F32) + bgu_ref[0]
    up = pltpu.roll(h, 2 * D_FF - 1, 1)
    gate = jnp.minimum(h, SWIGLU_LIMIT)
    up = jnp.clip(up, -SWIGLU_LIMIT, SWIGLU_LIMIT)
    act = (up + 1.0) * (gate * jax.nn.sigmoid(SWIGLU_ALPHA * gate))
    even = (lax.broadcasted_iota(jnp.int32, (MOE_BLOCK, D_FF), 1) % 2) == 0
    packed = jnp.where(even, act[:, 0:D_FF], pltpu.roll(act[:, D_FF:2 * D_FF], 1, 1))
    y_ref[...] = jnp.dot(packed.astype(BF16), wd_ref[0], preferred_element_type=F32) + bd_ref[0]


def _moe_call(block_e, slot_tok, h1_flat, nffn, wgu, bgu, wd, bd):
    n_blocks = block_e.shape[0]
    d = D_MODEL
    grid_spec = pltpu.PrefetchScalarGridSpec(
        num_scalar_prefetch=1,
        grid=(n_blocks,),
        in_specs=[
            pl.BlockSpec((1, 1, MOE_BLOCK), lambda i, be: (i, 0, 0), memory_space=pltpu.SMEM),
            pl.BlockSpec((1, 1, MOE_BLOCK), lambda i, be: (jnp.minimum(i + 1, n_blocks - 1), 0, 0),
                         memory_space=pltpu.SMEM),
            pl.BlockSpec(memory_space=pl.ANY),
            pl.BlockSpec((1, d), lambda i, be: (0, 0)),
            pl.BlockSpec((1, d, 2 * D_FF), lambda i, be: (be[i], 0, 0)),
            pl.BlockSpec((1, 1, 2 * D_FF), lambda i, be: (be[i], 0, 0)),
            pl.BlockSpec((1, D_FF, d), lambda i, be: (be[i], 0, 0)),
            pl.BlockSpec((1, 1, d), lambda i, be: (be[i], 0, 0)),
        ],
        out_specs=pl.BlockSpec((MOE_BLOCK, d), lambda i, be: (i, 0)),
        scratch_shapes=[pltpu.VMEM((2, MOE_BLOCK, d), F32), pltpu.SemaphoreType.DMA((2,))],
    )
    return pl.pallas_call(
        _moe_kernel,
        grid_spec=grid_spec,
        out_shape=jax.ShapeDtypeStruct((n_blocks * MOE_BLOCK, d), F32),
        compiler_params=pltpu.CompilerParams(
            dimension_semantics=("arbitrary",), vmem_limit_bytes=VMEM_LIMIT),
        name="moe",
    )(block_e, slot_tok, slot_tok, h1_flat, nffn, wgu, bgu, wd, bd)


def _combine_kernel(cur_ref, nxt_ref, y_hbm, h1_ref, gate_ref, p_ref, nple_ref, wpg_ref, wpp_ref,
                    o_ref, ybuf, sem):
    i = pl.program_id(0)
    n = pl.num_programs(0)
    slot = i % 2
    rows = TOP_K * CMB_TILE

    @pl.when(i == 0)
    def _():
        _gather_rows(cur_ref, y_hbm, ybuf.at[0], sem.at[0], rows, True)

    @pl.when(i + 1 < n)
    def _():
        _gather_rows(nxt_ref, y_hbm, ybuf.at[1 - slot], sem.at[1 - slot], rows, True)

    _gather_rows(cur_ref, y_hbm, ybuf.at[slot], sem.at[slot], rows, False)

    g = gate_ref[...]
    moe = jnp.zeros((CMB_TILE, D_MODEL), F32)
    for kk in range(TOP_K):
        moe = moe + ybuf[slot, kk * CMB_TILE:(kk + 1) * CMB_TILE, :] * g[:, kk:kk + 1]
    h2 = h1_ref[...] + moe
    hp = _rms(h2, nple_ref[...]).astype(BF16)
    gate = jax.nn.sigmoid(jnp.dot(hp, wpg_ref[...], preferred_element_type=F32))
    proj = jnp.dot(p_ref[...].astype(BF16), wpp_ref[...], preferred_element_type=F32)
    o_ref[...] = h2 + gate * proj


def _combine_call(pos, y, h1_flat, gates, p_flat, nple, wpg, wpp):
    n_tok, d = h1_flat.shape
    tc = CMB_TILE
    n_tiles = n_tok // tc
    rows = TOP_K * tc
    row = lambda i: (i, 0)
    const = lambda i: (0, 0)
    return pl.pallas_call(
        _combine_kernel,
        grid=(n_tiles,),
        in_specs=[
            pl.BlockSpec((1, 1, rows), lambda i: (i, 0, 0), memory_space=pltpu.SMEM),
            pl.BlockSpec((1, 1, rows), lambda i: (jnp.minimum(i + 1, n_tiles - 1), 0, 0),
                         memory_space=pltpu.SMEM),
            pl.BlockSpec(memory_space=pl.ANY),
            pl.BlockSpec((tc, d), row),
            pl.BlockSpec((tc, LANES), row),
            pl.BlockSpec((tc, PLE_DIM), row),
            pl.BlockSpec((1, d), const),
            pl.BlockSpec((d, d), const),
            pl.BlockSpec((PLE_DIM, d), const),
        ],
        out_specs=pl.BlockSpec((tc, d), row),
        out_shape=jax.ShapeDtypeStruct((n_tok, d), F32),
        scratch_shapes=[pltpu.VMEM((2, rows, d), F32), pltpu.SemaphoreType.DMA((2,))],
        compiler_params=pltpu.CompilerParams(
            dimension_semantics=("arbitrary",), vmem_limit_bytes=VMEM_LIMIT),
        name="combine",
    )(pos, pos, y, h1_flat, gates, p_flat, nple, wpg, wpp)


def _rope_table(s):
    half = ROPE_DIM // 2
    inv_freq = jnp.power(ROPE_THETA, -jnp.arange(half, dtype=F32) * (2.0 / ROPE_DIM))
    ang = jnp.arange(s, dtype=F32)[:, None] * inv_freq[None, :]
    cos, sin = jnp.cos(ang), jnp.sin(ang)
    pad1 = jnp.ones((s, HEAD_DIM - ROPE_DIM), F32)
    pad0 = jnp.zeros((s, HEAD_DIM - half), F32)
    c = jnp.concatenate([cos, cos, pad1], axis=-1)
    s_lo = jnp.concatenate([-sin, pad0], axis=-1)
    s_hi = jnp.concatenate([jnp.zeros((s, half), F32), sin, pad0[:, half:]], axis=-1)
    two = lambda t: jnp.concatenate([t, t], axis=-1)
    return jnp.concatenate([two(c), two(s_lo), two(s_hi)], axis=-1)


def _routing(idx, n_tok):
    flat_e = idx.reshape(-1)
    onehot = (flat_e[:, None] == jnp.arange(N_EXPERTS, dtype=jnp.int32)[None, :]).astype(jnp.int32)
    csum = jnp.cumsum(onehot, axis=0)
    counts = csum[-1]
    rank = jnp.take_along_axis(csum, flat_e[:, None], axis=1)[:, 0] - 1
    padded = (counts + MOE_BLOCK - 1) // MOE_BLOCK * MOE_BLOCK
    pends = jnp.cumsum(padded)
    pstarts = pends - padded
    pos = pstarts[flat_e] + rank
    n_slots = -(-(n_tok * TOP_K) // MOE_BLOCK) * MOE_BLOCK + N_EXPERTS * MOE_BLOCK
    n_blocks = n_slots // MOE_BLOCK
    tok = jnp.arange(n_tok * TOP_K, dtype=jnp.int32) // TOP_K
    slot_tok = jnp.zeros((n_slots,), jnp.int32).at[pos].set(tok)
    block_e = jnp.clip(jnp.searchsorted(pends, jnp.arange(n_blocks, dtype=jnp.int32) * MOE_BLOCK, side='right'),
                       0, N_EXPERTS - 1).astype(jnp.int32)
    return pos.astype(jnp.int32), slot_tok.reshape(n_blocks, 1, MOE_BLOCK), block_e


def _group_forward(x, p_l, mix_prm, moe_prm, ple_prm):
    b, s, d = x.shape
    n_tok = b * s
    h1, idx, gates = _mixer_call(x, _rope_table(s), mix_prm)
    idx_tk = jnp.transpose(idx[:, 0:TOP_K, :], (0, 2, 1)).reshape(n_tok, TOP_K)
    pos, slot_tok, block_e = _routing(idx_tk, n_tok)
    h1_flat = h1.reshape(n_tok, d)
    y = _moe_call(block_e, slot_tok, h1_flat, *moe_prm)
    n_tiles = n_tok // CMB_TILE
    pos_t = pos.reshape(n_tiles, CMB_TILE, TOP_K).transpose(0, 2, 1).reshape(n_tiles, 1, TOP_K * CMB_TILE)
    out = _combine_call(pos_t, y, h1_flat, gates.reshape(n_tok, LANES), p_l.reshape(n_tok, PLE_DIM), *ple_prm)
    return out.reshape(b, s, d)


def kernel(x_prompt, x_sample, p_prompt, p_sample, norm_mix, w_in, q_gain, k_gain, attn_sink, gmlp_v_gain, gmlp_w_s, gmlp_b_s, w_branch, w_out, norm_ffn, w_router, b_router, w_gate_up, b_gate_up, w_down, b_down, norm_ple, w_ple_gate, w_ple_proj):
    depth = norm_mix.shape[0]
    hp, hs = x_prompt, x_sample
    for l in range(depth):
        row = lambda a: a.reshape(1, -1)
        blockdiag = jnp.kron(jnp.eye(N_Q_HEADS, dtype=F32),
                             jnp.full((HEAD_DIM, HEAD_DIM), 1.0 / HEAD_DIM, F32)).astype(BF16)
        sink_rows = jnp.repeat(attn_sink[l].reshape(N_KV_HEADS, GQA_GROUP), ATTN_BLOCK, axis=1)[..., None]
        wcat = jnp.transpose(gmlp_w_s[l], (1, 0, 2)).reshape(GMLP_CHUNK, GMLP_GROUPS * GMLP_CHUNK).astype(BF16)
        bias_full = jnp.repeat(gmlp_b_s[l].T, GMLP_GROUP_DIM, axis=1)
        wr = jnp.pad(w_router[l], ((0, 0), (0, ROUTER_PAD - N_EXPERTS)))
        wr_hi = wr.astype(BF16)
        wr_lo = (wr - wr_hi.astype(F32)).astype(BF16)
        br = jnp.pad(b_router[l], (0, ROUTER_PAD - N_EXPERTS)).reshape(1, ROUTER_PAD)
        mix_prm = (
            row(norm_mix[l]), w_in[l].astype(BF16),
            row(jnp.tile(q_gain[l], N_Q_HEADS)), row(jnp.tile(k_gain[l], N_KV_HEADS)),
            sink_rows, row(gmlp_v_gain[l]), blockdiag, wcat, bias_full,
            w_branch[l].astype(BF16), w_out[l].astype(BF16), row(norm_ffn[l]),
            jnp.stack([wr_hi, wr_lo]), br,
        )
        wd_perm = w_down[l].reshape(N_EXPERTS, 2, D_FF // 2, D_MODEL).transpose(0, 2, 1, 3)
        moe_prm = (
            row(norm_ffn[l]), w_gate_up[l].astype(BF16), b_gate_up[l].reshape(N_EXPERTS, 1, 2 * D_FF),
            wd_perm.reshape(N_EXPERTS, D_FF, D_MODEL).astype(BF16), b_down[l].reshape(N_EXPERTS, 1, D_MODEL),
        )
        ple_prm = (row(norm_ple[l]), w_ple_gate[l].astype(BF16), w_ple_proj[l].astype(BF16))
        hp = _group_forward(hp, p_prompt[l], mix_prm, moe_prm, ple_prm)
        hs = _group_forward(hs, p_sample[l], mix_prm, moe_prm, ple_prm)
    return (hp.astype(x_prompt.dtype), hs.astype(x_sample.dtype))
```

```python
import functools

import jax
import jax.numpy as jnp
import numpy as np
from jax import lax
from jax.experimental import pallas as pl
from jax.experimental.pallas import tpu as pltpu
from jax.experimental.pallas import tpu_sc as plsc

D_MODEL = 1024
HEAD_DIM = 64
N_Q_HEADS = 8
N_KV_HEADS = 2
GQA_GROUP = N_Q_HEADS // N_KV_HEADS
ATTN_WIDTH = N_Q_HEADS * HEAD_DIM
KV_WIDTH = N_KV_HEADS * HEAD_DIM
WINDOW = 128
ATTN_BLOCK = 128
ROPE_THETA = 500000.0
ROPE_DIM = HEAD_DIM // 4
GMLP_WIDTH = D_MODEL // 2
GMLP_GROUPS = 8
GMLP_GROUP_DIM = GMLP_WIDTH // GMLP_GROUPS
GMLP_CHUNK = 128
N_BRANCH = 2
IN_WIDTH = ATTN_WIDTH + 2 * KV_WIDTH + 2 * GMLP_WIDTH + N_BRANCH * D_MODEL
N_EXPERTS = 32
TOP_K = 4
D_FF = D_MODEL
SWIGLU_ALPHA = 1.702
SWIGLU_LIMIT = 7.0
PLE_DIM = 256
EPS = 1e-6

Q_OFF = 0
K_OFF = ATTN_WIDTH
V_OFF = K_OFF + KV_WIDTH
U_OFF = V_OFF + KV_WIDTH
VG_OFF = U_OFF + GMLP_WIDTH
GL_OFF = VG_OFF + GMLP_WIDTH

LANES = 128
PACK_WORDS = D_MODEL // 2
SUB = PACK_WORDS // LANES
MIX_TILE = 512
MOE_BLOCK = 256
CMB_TILE = 256
ROUTER_PAD = 128
SC_CHUNK = 128
DISPATCH_TOKENS = 64
VMEM_LIMIT = 56 * 1024 * 1024

BF16 = jnp.bfloat16
F32 = jnp.float32
U32 = jnp.uint32


def _rms(x, gain):
    return x * lax.rsqrt(jnp.mean(x * x, axis=-1, keepdims=True) + EPS) * gain


def _gelu(x):
    return 0.5 * x * (1.0 + lax.erf(x * np.float32(np.sqrt(0.5))))


def _head_rms(x, blockdiag, gain):
    ms = jnp.dot((x * x).astype(BF16), blockdiag, preferred_element_type=F32)
    return x * lax.rsqrt(ms + EPS) * gain


def _rope(x, cos, sin_lo, sin_hi):
    w = x.shape[-1]
    return x * cos + pltpu.roll(x, w - ROPE_DIM // 2, 1) * sin_lo + pltpu.roll(x, ROPE_DIM // 2, 1) * sin_hi


def _tile_lanes(t, reps):
    return t if reps == 1 else jnp.concatenate([t] * reps, axis=-1)


def _store_packed(ref, x):
    rows = x.shape[0]
    hi = lax.bitcast_convert_type(x[:, :PACK_WORDS].astype(BF16).astype(F32), U32)
    lo = lax.bitcast_convert_type(x[:, PACK_WORDS:].astype(BF16).astype(F32), U32)
    words = hi | (lo >> 16)
    for j in range(SUB):
        ref[pl.ds(j, rows, stride=SUB), :] = words[:, j * LANES:(j + 1) * LANES]


def _load_packed(ref, rows):
    words = jnp.concatenate([ref[pl.ds(j, rows, stride=SUB), :] for j in range(SUB)], axis=1)
    left = lax.bitcast_convert_type(words & np.uint32(0xFFFF0000), F32)
    right = lax.bitcast_convert_type(words << 16, F32)
    return jnp.concatenate([left, right], axis=1)


def _mixer_kernel(x_ref, xp_ref, xn_ref, rp_ref, rpp_ref, rpn_ref,
                  nmix_ref, win_ref, qg_ref, kg_ref, sink_ref, vgain_ref, bdq_ref,
                  wcat_ref, bias_ref, wbr_ref, wout_ref, nffn_ref, wr_ref, br_ref, tri_ref,
                  h1_ref, hn_ref, route_ref, gate_ref, cnt_out_ref,
                  z_ref, q_ref, k_ref, v_ref, attn_ref, gm_ref, cnt_ref, *, n_blocks_seq):
    ts = MIX_TILE
    i = pl.program_id(1)
    x = x_ref[0]
    xn = _rms(x, nmix_ref[...]).astype(BF16)
    z_ref[...] = jnp.dot(xn, win_ref[...], preferred_element_type=F32)

    cos = rp_ref[:, 0:LANES]
    sin_lo = rp_ref[:, LANES:2 * LANES]
    sin_hi = rp_ref[:, 2 * LANES:3 * LANES]
    bdq = bdq_ref[...]
    bdk = bdq_ref[0:KV_WIDTH, 0:KV_WIDTH]

    q = _head_rms(z_ref[:, Q_OFF:Q_OFF + ATTN_WIDTH], bdq, qg_ref[...])
    reps = ATTN_WIDTH // LANES
    q = _rope(q, _tile_lanes(cos, reps), _tile_lanes(sin_lo, reps), _tile_lanes(sin_hi, reps))
    q_ref[...] = (q * (HEAD_DIM ** -0.5)).astype(BF16)

    k = _head_rms(z_ref[:, K_OFF:K_OFF + KV_WIDTH], bdk, kg_ref[...])
    k_ref[ATTN_BLOCK:ATTN_BLOCK + ts, :] = _rope(k, cos, sin_lo, sin_hi).astype(BF16)
    v_ref[ATTN_BLOCK:ATTN_BLOCK + ts, :] = z_ref[:, V_OFF:V_OFF + KV_WIDTH].astype(BF16)

    xh = jnp.concatenate([xp_ref[0], xn_ref[0]], axis=0)
    xhn = _rms(xh, nmix_ref[...]).astype(BF16)
    zh = jnp.dot(xhn, win_ref[:, K_OFF:K_OFF + 2 * KV_WIDTH], preferred_element_type=F32)
    kh = _head_rms(zh[:, 0:KV_WIDTH], bdk, kg_ref[...])
    rph = jnp.concatenate([rpp_ref[...], rpn_ref[...]], axis=0)
    kh = _rope(kh, rph[:, 0:LANES], rph[:, LANES:2 * LANES], rph[:, 2 * LANES:3 * LANES]).astype(BF16)
    vh = zh[:, KV_WIDTH:2 * KV_WIDTH].astype(BF16)
    k_ref[0:ATTN_BLOCK, :] = kh[0:ATTN_BLOCK]
    k_ref[ATTN_BLOCK + ts:2 * ATTN_BLOCK + ts, :] = kh[ATTN_BLOCK:]
    v_ref[0:ATTN_BLOCK, :] = vh[0:ATTN_BLOCK]
    v_ref[ATTN_BLOCK + ts:2 * ATTN_BLOCK + ts, :] = vh[ATTN_BLOCK:]

    rows = GQA_GROUP * ATTN_BLOCK
    keys = 3 * ATTN_BLOCK
    r = lax.broadcasted_iota(jnp.int32, (rows, keys), 0) % ATTN_BLOCK
    c = lax.broadcasted_iota(jnp.int32, (rows, keys), 1)
    band = (c >= r) & (c <= r + 2 * WINDOW)
    for qb in range(ts // ATTN_BLOCK):
        gb = i * (ts // ATTN_BLOCK) + qb
        lo = jnp.where(gb == 0, ATTN_BLOCK, 0)
        hi = jnp.where(gb == n_blocks_seq - 1, 2 * ATTN_BLOCK, keys)
        valid = band & (c >= lo) & (c < hi)
        r0 = qb * ATTN_BLOCK
        for j in range(N_KV_HEADS):
            q4 = jnp.concatenate(
                [q_ref[r0:r0 + ATTN_BLOCK, (GQA_GROUP * j + g) * HEAD_DIM:(GQA_GROUP * j + g + 1) * HEAD_DIM]
                 for g in range(GQA_GROUP)], axis=0)
            kw = k_ref[r0:r0 + keys, j * HEAD_DIM:(j + 1) * HEAD_DIM]
            vw = v_ref[r0:r0 + keys, j * HEAD_DIM:(j + 1) * HEAD_DIM]
            s = lax.dot_general(q4, kw, (((1,), (1,)), ((), ())), preferred_element_type=F32)
            s = jnp.where(valid, s, -jnp.inf)
            sk = sink_ref[j]
            m = jnp.maximum(jnp.max(s, axis=-1, keepdims=True), sk)
            p = jnp.exp(s - m)
            denom = jnp.sum(p, axis=-1, keepdims=True) + jnp.exp(sk - m)
            o = jnp.dot(p.astype(BF16), vw, preferred_element_type=F32) / denom
            for g in range(GQA_GROUP):
                h = GQA_GROUP * j + g
                attn_ref[r0:r0 + ATTN_BLOCK, h * HEAD_DIM:(h + 1) * HEAD_DIM] = (
                    o[g * ATTN_BLOCK:(g + 1) * ATTN_BLOCK].astype(BF16))

    half = GMLP_WIDTH // 2
    gpm = half // GMLP_GROUP_DIM
    lane_grp = lax.broadcasted_iota(jnp.int32, (GMLP_CHUNK, half), 1) // GMLP_GROUP_DIM
    for ch in range(ts // GMLP_CHUNK):
        c0 = ch * GMLP_CHUNK
        vg = _gelu(z_ref[c0:c0 + GMLP_CHUNK, VG_OFF:VG_OFF + GMLP_WIDTH])
        vgn = _head_rms(vg, bdq, vgain_ref[...])
        mixed = []
        for nt in range(2):
            part = vgn[:, nt * half:(nt + 1) * half]
            vexp = jnp.concatenate(
                [jnp.where(lane_grp == gl, part, 0.0).astype(BF16) for gl in range(gpm)], axis=0)
            wpart = wcat_ref[:, nt * gpm * GMLP_CHUNK:(nt + 1) * gpm * GMLP_CHUNK]
            mixed.append(jnp.dot(wpart, vexp, preferred_element_type=F32))
        mixed = jnp.concatenate(mixed, axis=-1) + bias_ref[...]
        u = _gelu(z_ref[c0:c0 + GMLP_CHUNK, U_OFF:U_OFF + GMLP_WIDTH])
        gm_ref[c0:c0 + GMLP_CHUNK, :] = (u * mixed).astype(BF16)

    ya = jnp.dot(attn_ref[...], wbr_ref[0], preferred_element_type=F32)
    yg = jnp.dot(gm_ref[...], wbr_ref[1], preferred_element_type=F32)
    ga = jax.nn.sigmoid(z_ref[:, GL_OFF:GL_OFF + D_MODEL])
    gg = jax.nn.sigmoid(z_ref[:, GL_OFF + D_MODEL:GL_OFF + 2 * D_MODEL])
    merged = (ga * ya + gg * yg).astype(BF16)
    h1 = x + jnp.dot(merged, wout_ref[...], preferred_element_type=F32)
    h1_ref[0] = h1

    hn = _rms(h1, nffn_ref[...])
    _store_packed(hn_ref, hn)

    hn_hi = hn.astype(BF16)
    hn_lo = (hn - hn_hi.astype(F32)).astype(BF16)
    w_hi = wr_ref[0]
    w_lo = wr_ref[1]
    logits = (jnp.dot(hn_hi, w_hi, preferred_element_type=F32)
              + jnp.dot(hn_hi, w_lo, preferred_element_type=F32)
              + jnp.dot(hn_lo, w_hi, preferred_element_type=F32)) + br_ref[...]
    lt = jnp.transpose(logits)[0:N_EXPERTS, :]
    eid = lax.broadcasted_iota(jnp.int32, lt.shape, 0)
    vals, ids = [], []
    for _ in range(TOP_K):
        mx = jnp.max(lt, axis=0, keepdims=True)
        am = jnp.min(jnp.where(lt == mx, eid, N_EXPERTS), axis=0, keepdims=True)
        vals.append(mx)
        ids.append(am)
        lt = jnp.where(eid == am, -jnp.inf, lt)
    ex = [jnp.exp(v - vals[0]) for v in vals]
    tot = ex[0] + ex[1] + ex[2] + ex[3]
    g8 = jnp.concatenate([e / tot for e in ex] + [jnp.zeros_like(tot)] * (8 - TOP_K), axis=0)
    gates = jnp.concatenate([g8, jnp.zeros((LANES - 8, ts), F32)], axis=0)
    gate_ref[0] = jnp.transpose(gates)

    @pl.when((pl.program_id(0) == 0) & (i == 0))
    def _():
        cnt_ref[...] = jnp.zeros_like(cnt_ref)

    sel = [eid == a for a in ids]
    member = (sel[0] | sel[1] | sel[2] | sel[3]).astype(F32)
    before = jnp.dot(member.astype(BF16), tri_ref[...], preferred_element_type=F32) + cnt_ref[:, 0:1]
    ranks = [jnp.sum(jnp.where(s_, before, 0.0), axis=0, keepdims=True).astype(jnp.int32) for s_ in sel]
    route_ref[0] = jnp.concatenate(ids + ranks, axis=0)
    cnt_ref[...] = cnt_ref[...] + jnp.sum(member, axis=1, keepdims=True)
    cnt_out_ref[...] = cnt_ref[...]


def _mixer_call(x, rope_tab, prm):
    b, s, d = x.shape
    ts = MIX_TILE
    nt = s // ts
    nb = s // ATTN_BLOCK
    per = ts // ATTN_BLOCK
    const2 = lambda bi, i: (0, 0)
    const3 = lambda bi, i: (0, 0, 0)

    def wspec(arr):
        return pl.BlockSpec(arr.shape, const2 if arr.ndim == 2 else const3)

    in_specs = [
        pl.BlockSpec((1, ts, d), lambda bi, i: (bi, i, 0)),
        pl.BlockSpec((1, ATTN_BLOCK, d), lambda bi, i: (bi, jnp.maximum(i * per - 1, 0), 0)),
        pl.BlockSpec((1, ATTN_BLOCK, d), lambda bi, i: (bi, jnp.minimum((i + 1) * per, nb - 1), 0)),
        pl.BlockSpec((ts, 3 * LANES), lambda bi, i: (i, 0)),
        pl.BlockSpec((ATTN_BLOCK, 3 * LANES), lambda bi, i: (jnp.maximum(i * per - 1, 0), 0)),
        pl.BlockSpec((ATTN_BLOCK, 3 * LANES), lambda bi, i: (jnp.minimum((i + 1) * per, nb - 1), 0)),
    ] + [wspec(a) for a in prm]
    out_shape = [
        jax.ShapeDtypeStruct((b, s, d), F32),
        jax.ShapeDtypeStruct((b * s * SUB, LANES), U32),
        jax.ShapeDtypeStruct((b, 2 * TOP_K, s), jnp.int32),
        jax.ShapeDtypeStruct((b, s, LANES), F32),
        jax.ShapeDtypeStruct((N_EXPERTS, LANES), F32),
    ]
    out_specs = [
        pl.BlockSpec((1, ts, d), lambda bi, i: (bi, i, 0)),
        pl.BlockSpec((ts * SUB, LANES), lambda bi, i: (bi * nt + i, 0)),
        pl.BlockSpec((1, 2 * TOP_K, ts), lambda bi, i: (bi, 0, i)),
        pl.BlockSpec((1, ts, LANES), lambda bi, i: (bi, i, 0)),
        pl.BlockSpec((N_EXPERTS, LANES), const2),
    ]
    scratch = [
        pltpu.VMEM((ts, IN_WIDTH), F32),
        pltpu.VMEM((ts, ATTN_WIDTH), BF16),
        pltpu.VMEM((ts + 2 * ATTN_BLOCK, KV_WIDTH), BF16),
        pltpu.VMEM((ts + 2 * ATTN_BLOCK, KV_WIDTH), BF16),
        pltpu.VMEM((ts, ATTN_WIDTH), BF16),
        pltpu.VMEM((ts, GMLP_WIDTH), BF16),
        pltpu.VMEM((N_EXPERTS, LANES), F32),
    ]
    return pl.pallas_call(
        functools.partial(_mixer_kernel, n_blocks_seq=nb),
        grid=(b, nt),
        in_specs=in_specs,
        out_specs=out_specs,
        out_shape=out_shape,
        scratch_shapes=scratch,
        compiler_params=pltpu.CompilerParams(
            dimension_semantics=("arbitrary", "arbitrary"), vmem_limit_bytes=VMEM_LIMIT),
        name="mixer",
    )(x, x, x, rope_tab, rope_tab, rope_tab, *prm)


def _sc_workers():
    info = plsc.get_sparse_core_info()
    return info.num_cores, info.num_cores * info.num_subcores


def _sc_dispatch(rows2d, dst_idx, n_out_rows):
    n_cores, n_workers = _sc_workers()
    chunk_rows = DISPATCH_TOKENS * SUB
    halves = chunk_rows // SC_CHUNK
    n_chunks = rows2d.shape[0] // chunk_rows
    per_w = n_chunks // n_workers
    mesh = plsc.VectorSubcoreMesh(core_axis_name="c", subcore_axis_name="s")

    @functools.partial(
        pl.kernel, mesh=mesh,
        out_type=jax.ShapeDtypeStruct((n_out_rows, LANES), rows2d.dtype),
        scratch_types=[pltpu.VMEM((TOP_K * halves, SC_CHUNK), jnp.int32),
                       pltpu.VMEM((chunk_rows, LANES), rows2d.dtype),
                       pltpu.SemaphoreType.DMA],
    )
    def k(src_hbm, idx_hbm, out_hbm, idx_v, rows_v, sem):
        wid = lax.axis_index("s") * n_cores + lax.axis_index("c")

        @pl.loop(0, per_w)
        def _(j):
            c = wid * per_w + j
            pltpu.sync_copy(src_hbm.at[pl.ds(c * chunk_rows, chunk_rows)], rows_v)
            pltpu.sync_copy(idx_hbm.at[pl.ds(c * TOP_K * halves, TOP_K * halves)], idx_v)
            copies = [
                pltpu.async_copy(rows_v.at[pl.ds((q % halves) * SC_CHUNK, SC_CHUNK)],
                                 out_hbm.at[idx_v.at[q]], sem)
                for q in range(TOP_K * halves)]
            for cp in copies:
                cp.wait()

    return k(rows2d, dst_idx)


def _sc_gather(table2d, idx):
    n_cores, n_workers = _sc_workers()
    n = idx.shape[0]
    per_w = n // n_workers
    mesh = plsc.VectorSubcoreMesh(core_axis_name="c", subcore_axis_name="s")

    @functools.partial(
        pl.kernel, mesh=mesh,
        out_type=jax.ShapeDtypeStruct((n, LANES), table2d.dtype),
        scratch_types=[pltpu.VMEM((SC_CHUNK,), jnp.int32),
                       pltpu.VMEM((SC_CHUNK, LANES), table2d.dtype),
                       pltpu.SemaphoreType.DMA],
    )
    def k(table_hbm, idx_hbm, out_hbm, idx_v, rows_v, sem):
        wid = lax.axis_index("s") * n_cores + lax.axis_index("c")

        @pl.loop(0, per_w // SC_CHUNK)
        def _(j):
            base = wid * per_w + j * SC_CHUNK
            pltpu.sync_copy(idx_hbm.at[pl.ds(base, SC_CHUNK)], idx_v)
            pltpu.async_copy(table_hbm.at[idx_v], rows_v, sem).wait()
            pltpu.sync_copy(rows_v, out_hbm.at[pl.ds(base, SC_CHUNK)])

    return k(table2d, idx)


def _moe_kernel(be_ref, nv_ref, x_ref, wgu_ref, bgu_ref, wd_ref, bd_ref, y_ref):
    i = pl.program_id(0)
    nv = nv_ref[i]

    @pl.when(nv > 0)
    def _():
        x = _load_packed(x_ref, MOE_BLOCK)
        live = lax.broadcasted_iota(jnp.int32, x.shape, 0) < nv
        xe = jnp.where(live, x, 0.0).astype(BF16)
        h = jnp.dot(xe, wgu_ref[0], preferred_element_type=F32) + bgu_ref[0]
        up = pltpu.roll(h, 2 * D_FF - 1, 1)
        gate = jnp.minimum(h, SWIGLU_LIMIT)
        up = jnp.clip(up, -SWIGLU_LIMIT, SWIGLU_LIMIT)
        act = (up + 1.0) * (gate * jax.nn.sigmoid(SWIGLU_ALPHA * gate))
        even = (lax.broadcasted_iota(jnp.int32, (MOE_BLOCK, D_FF), 1) % 2) == 0
        packed = jnp.where(even, act[:, 0:D_FF], pltpu.roll(act[:, D_FF:2 * D_FF], 1, 1))
        y = jnp.dot(packed.astype(BF16), wd_ref[0], preferred_element_type=F32) + bd_ref[0]
        _store_packed(y_ref, y)


def _moe_call(block_e, n_valid, xb, wgu, bgu, wd, bd):
    n_blocks = block_e.shape[0]
    d = D_MODEL
    grid_spec = pltpu.PrefetchScalarGridSpec(
        num_scalar_prefetch=2,
        grid=(n_blocks,),
        in_specs=[
            pl.BlockSpec((MOE_BLOCK * SUB, LANES), lambda i, be, nv: (i, 0)),
            pl.BlockSpec((1, d, 2 * D_FF), lambda i, be, nv: (be[i], 0, 0)),
            pl.BlockSpec((1, 1, 2 * D_FF), lambda i, be, nv: (be[i], 0, 0)),
            pl.BlockSpec((1, D_FF, d), lambda i, be, nv: (be[i], 0, 0)),
            pl.BlockSpec((1, 1, d), lambda i, be, nv: (be[i], 0, 0)),
        ],
        out_specs=pl.BlockSpec((MOE_BLOCK * SUB, LANES), lambda i, be, nv: (i, 0)),
    )
    return pl.pallas_call(
        _moe_kernel,
        grid_spec=grid_spec,
        out_shape=jax.ShapeDtypeStruct((n_blocks * MOE_BLOCK * SUB, LANES), U32),
        compiler_params=pltpu.CompilerParams(
            dimension_semantics=("arbitrary",), vmem_limit_bytes=VMEM_LIMIT),
        name="moe",
    )(block_e, n_valid, xb, wgu, bgu, wd, bd)


def _combine_kernel(g0_ref, g1_ref, g2_ref, g3_ref, h1_ref, gate_ref, p_ref, nple_ref, wpg_ref, wpp_ref, o_ref):
    g = gate_ref[...]
    moe = jnp.zeros((CMB_TILE, D_MODEL), F32)
    for kk, g_ref in enumerate((g0_ref, g1_ref, g2_ref, g3_ref)):
        moe = moe + _load_packed(g_ref, CMB_TILE) * g[:, kk:kk + 1]
    h2 = h1_ref[...] + moe
    hp = _rms(h2, nple_ref[...]).astype(BF16)
    gate = jax.nn.sigmoid(jnp.dot(hp, wpg_ref[...], preferred_element_type=F32))
    proj = jnp.dot(p_ref[...].astype(BF16), wpp_ref[...], preferred_element_type=F32)
    o_ref[...] = h2 + gate * proj


def _combine_call(gathered, h1_flat, gates, p_flat, nple, wpg, wpp):
    n_tok, d = h1_flat.shape
    tc = CMB_TILE
    n_tiles = n_tok // tc
    row = lambda i: (i, 0)
    const = lambda i: (0, 0)
    g_specs = [pl.BlockSpec((tc * SUB, LANES), functools.partial(lambda i, kk: (kk * n_tiles + i, 0), kk=kk))
               for kk in range(TOP_K)]
    return pl.pallas_call(
        _combine_kernel,
        grid=(n_tiles,),
        in_specs=g_specs + [
            pl.BlockSpec((tc, d), row),
            pl.BlockSpec((tc, LANES), row),
            pl.BlockSpec((tc, PLE_DIM), row),
            pl.BlockSpec((1, d), const),
            pl.BlockSpec((d, d), const),
            pl.BlockSpec((PLE_DIM, d), const),
        ],
        out_specs=pl.BlockSpec((tc, d), row),
        out_shape=jax.ShapeDtypeStruct((n_tok, d), F32),
        compiler_params=pltpu.CompilerParams(
            dimension_semantics=("arbitrary",), vmem_limit_bytes=VMEM_LIMIT),
        name="combine",
    )(gathered, gathered, gathered, gathered, h1_flat, gates, p_flat, nple, wpg, wpp)


def _rope_table(s):
    half = ROPE_DIM // 2
    inv_freq = jnp.power(ROPE_THETA, -jnp.arange(half, dtype=F32) * (2.0 / ROPE_DIM))
    ang = jnp.arange(s, dtype=F32)[:, None] * inv_freq[None, :]
    cos, sin = jnp.cos(ang), jnp.sin(ang)
    pad1 = jnp.ones((s, HEAD_DIM - ROPE_DIM), F32)
    pad0 = jnp.zeros((s, HEAD_DIM - half), F32)
    c = jnp.concatenate([cos, cos, pad1], axis=-1)
    s_lo = jnp.concatenate([-sin, pad0], axis=-1)
    s_hi = jnp.concatenate([jnp.zeros((s, half), F32), sin, pad0[:, half:]], axis=-1)
    two = lambda t: jnp.concatenate([t, t], axis=-1)
    return jnp.concatenate([two(c), two(s_lo), two(s_hi)], axis=-1)


def _routing(route, counts, n_tok):
    ids = jnp.transpose(route[:, 0:TOP_K, :], (0, 2, 1)).reshape(n_tok, TOP_K)
    ranks = jnp.transpose(route[:, TOP_K:2 * TOP_K, :], (0, 2, 1)).reshape(n_tok, TOP_K)
    counts = counts.astype(jnp.int32)
    padded = (counts + MOE_BLOCK - 1) // MOE_BLOCK * MOE_BLOCK
    pends = jnp.cumsum(padded)
    pstarts = pends - padded
    experts = jnp.arange(N_EXPERTS, dtype=jnp.int32)
    start_of = jnp.sum(jnp.where(ids[..., None] == experts, pstarts, 0), axis=-1)
    pos = start_of + ranks
    n_slots = -(-(n_tok * TOP_K) // MOE_BLOCK) * MOE_BLOCK + N_EXPERTS * MOE_BLOCK
    n_blocks = n_slots // MOE_BLOCK
    first = jnp.arange(n_blocks, dtype=jnp.int32) * MOE_BLOCK
    block_e = jnp.clip(jnp.sum((first[:, None] >= pends[None, :]).astype(jnp.int32), axis=1), 0, N_EXPERTS - 1)
    n_valid = jnp.clip(counts[block_e] - (first - pstarts[block_e]), 0, MOE_BLOCK)
    return pos, block_e, n_valid.astype(jnp.int32), n_slots


def _group_forward(x, p_l, mix_prm, moe_prm, ple_prm):
    b, s, d = x.shape
    n_tok = b * s
    h1, hn_rows, route, gates, counts = _mixer_call(x, _rope_table(s), mix_prm)
    pos, block_e, n_valid, n_slots = _routing(route, counts[:, 0], n_tok)
    sub = jnp.arange(SUB, dtype=jnp.int32)
    pos_rows = pos[:, :, None] * SUB + sub
    halves = DISPATCH_TOKENS * SUB // SC_CHUNK
    per_half = DISPATCH_TOKENS // halves
    dst_idx = pos_rows.reshape(n_tok // DISPATCH_TOKENS, halves, per_half, TOP_K, SUB)
    dst_idx = dst_idx.transpose(0, 3, 1, 2, 4).reshape(-1, SC_CHUNK)
    xb = _sc_dispatch(hn_rows, dst_idx, n_slots * SUB)
    y = _moe_call(block_e, n_valid, xb, *moe_prm)
    src_idx = pos_rows.transpose(1, 0, 2).reshape(-1)
    gathered = _sc_gather(y, src_idx)
    out = _combine_call(gathered, h1.reshape(n_tok, d), gates.reshape(n_tok, LANES),
                        p_l.reshape(n_tok, PLE_DIM), *ple_prm)
    return out.reshape(b, s, d)


def kernel(x_prompt, x_sample, p_prompt, p_sample, norm_mix, w_in, q_gain, k_gain, attn_sink, gmlp_v_gain, gmlp_w_s, gmlp_b_s, w_branch, w_out, norm_ffn, w_router, b_router, w_gate_up, b_gate_up, w_down, b_down, norm_ple, w_ple_gate, w_ple_proj):
    depth = norm_mix.shape[0]
    hp, hs = x_prompt, x_sample
    for l in range(depth):
        row = lambda a: a.reshape(1, -1)
        blockdiag = jnp.kron(jnp.eye(N_Q_HEADS, dtype=F32),
                             jnp.full((HEAD_DIM, HEAD_DIM), 1.0 / HEAD_DIM, F32)).astype(BF16)
        sink_rows = jnp.repeat(attn_sink[l].reshape(N_KV_HEADS, GQA_GROUP), ATTN_BLOCK, axis=1)[..., None]
        wcat = jnp.transpose(gmlp_w_s[l], (1, 0, 2)).reshape(GMLP_CHUNK, GMLP_GROUPS * GMLP_CHUNK).astype(BF16)
        bias_full = jnp.repeat(gmlp_b_s[l].T, GMLP_GROUP_DIM, axis=1)
        wr = jnp.pad(w_router[l], ((0, 0), (0, ROUTER_PAD - N_EXPERTS)))
        wr_hi = wr.astype(BF16)
        wr_lo = (wr - wr_hi.astype(F32)).astype(BF16)
        br = jnp.pad(b_router[l], (0, ROUTER_PAD - N_EXPERTS)).reshape(1, ROUTER_PAD)
        tri = (jnp.arange(MIX_TILE)[:, None] < jnp.arange(MIX_TILE)[None, :]).astype(BF16)
        mix_prm = (
            row(norm_mix[l]), w_in[l].astype(BF16),
            row(jnp.tile(q_gain[l], N_Q_HEADS)), row(jnp.tile(k_gain[l], N_KV_HEADS)),
            sink_rows, row(gmlp_v_gain[l]), blockdiag, wcat, bias_full,
            w_branch[l].astype(BF16), w_out[l].astype(BF16), row(norm_ffn[l]),
            jnp.stack([wr_hi, wr_lo]), br, tri,
        )
        wd_perm = w_down[l].reshape(N_EXPERTS, 2, D_FF // 2, D_MODEL).transpose(0, 2, 1, 3)
        moe_prm = (
            w_gate_up[l].astype(BF16), b_gate_up[l].reshape(N_EXPERTS, 1, 2 * D_FF),
            wd_perm.reshape(N_EXPERTS, D_FF, D_MODEL).astype(BF16), b_down[l].reshape(N_EXPERTS, 1, D_MODEL),
        )
        ple_prm = (row(norm_ple[l]), w_ple_gate[l].astype(BF16), w_ple_proj[l].astype(BF16))
        hp = _group_forward(hp, p_prompt[l], mix_prm, moe_prm, ple_prm)
        hs = _group_forward(hs, p_sample[l], mix_prm, moe_prm, ple_prm)
    return (hp.astype(x_prompt.dtype), hs.astype(x_sample.dtype))
```

```python
import functools

import jax
import jax.numpy as jnp
import numpy as np
from jax import lax
from jax.experimental import pallas as pl
from jax.experimental.pallas import tpu as pltpu
from jax.experimental.pallas import tpu_sc as plsc

D_MODEL = 1024
HEAD_DIM = 64
N_Q_HEADS = 8
N_KV_HEADS = 2
GQA_GROUP = N_Q_HEADS // N_KV_HEADS
ATTN_WIDTH = N_Q_HEADS * HEAD_DIM
KV_WIDTH = N_KV_HEADS * HEAD_DIM
WINDOW = 128
ATTN_BLOCK = 128
ROPE_THETA = 500000.0
ROPE_DIM = HEAD_DIM // 4
GMLP_WIDTH = D_MODEL // 2
GMLP_GROUPS = 8
GMLP_GROUP_DIM = GMLP_WIDTH // GMLP_GROUPS
GMLP_CHUNK = 128
N_BRANCH = 2
IN_WIDTH = ATTN_WIDTH + 2 * KV_WIDTH + 2 * GMLP_WIDTH + N_BRANCH * D_MODEL
N_EXPERTS = 32
TOP_K = 4
D_FF = D_MODEL
SWIGLU_ALPHA = 1.702
SWIGLU_LIMIT = 7.0
PLE_DIM = 256
EPS = 1e-6

Q_OFF = 0
K_OFF = ATTN_WIDTH
V_OFF = K_OFF + KV_WIDTH
U_OFF = V_OFF + KV_WIDTH
VG_OFF = U_OFF + GMLP_WIDTH
GL_OFF = VG_OFF + GMLP_WIDTH

LANES = 128
PACK_WORDS = D_MODEL // 2
SUB = PACK_WORDS // LANES
MIX_TILE = 512
MOE_BLOCK = 256
CMB_TILE = 256
ROUTER_PAD = 128
SC_CHUNK = 128
DISPATCH_TOKENS = 64
VMEM_LIMIT = 56 * 1024 * 1024

BF16 = jnp.bfloat16
F32 = jnp.float32
U32 = jnp.uint32


def _rms(x, gain):
    return x * lax.rsqrt(jnp.mean(x * x, axis=-1, keepdims=True) + EPS) * gain


def _gelu(x):
    return 0.5 * x * (1.0 + lax.erf(x * np.float32(np.sqrt(0.5))))


def _head_rms(x, blockdiag, gain):
    ms = jnp.dot((x * x).astype(BF16), blockdiag, preferred_element_type=F32)
    return x * lax.rsqrt(ms + EPS) * gain


def _rope(x, cos, sin_lo, sin_hi):
    w = x.shape[-1]
    return x * cos + pltpu.roll(x, w - ROPE_DIM // 2, 1) * sin_lo + pltpu.roll(x, ROPE_DIM // 2, 1) * sin_hi


def _tile_lanes(t, reps):
    return t if reps == 1 else jnp.concatenate([t] * reps, axis=-1)


def _store_packed(ref, x):
    rows = x.shape[0]
    hi = lax.bitcast_convert_type(x[:, :PACK_WORDS].astype(BF16).astype(F32), U32)
    lo = lax.bitcast_convert_type(x[:, PACK_WORDS:].astype(BF16).astype(F32), U32)
    words = hi | (lo >> 16)
    for j in range(SUB):
        ref[pl.ds(j, rows, stride=SUB), :] = words[:, j * LANES:(j + 1) * LANES]


def _load_packed(ref, rows):
    words = jnp.concatenate([ref[pl.ds(j, rows, stride=SUB), :] for j in range(SUB)], axis=1)
    left = lax.bitcast_convert_type(words & np.uint32(0xFFFF0000), F32)
    right = lax.bitcast_convert_type(words << 16, F32)
    return jnp.concatenate([left, right], axis=1)


def _mixer_kernel(x_ref, xp_ref, xn_ref, rp_ref, rpp_ref, rpn_ref,
                  nmix_ref, win_ref, qg_ref, kg_ref, sink_ref, vgain_ref, bdq_ref,
                  wcat_ref, bias_ref, wbr_ref, wout_ref, nffn_ref, wr_ref, br_ref, tri_ref,
                  h1_ref, hn_ref, route_ref, gate_ref, cnt_out_ref,
                  z_ref, q_ref, k_ref, v_ref, attn_ref, gm_ref, cnt_ref, *, n_blocks_seq):
    ts = MIX_TILE
    i = pl.program_id(1)
    x = x_ref[0]
    xn = _rms(x, nmix_ref[...]).astype(BF16)
    z_ref[...] = jnp.dot(xn, win_ref[...], preferred_element_type=F32)

    cos = rp_ref[:, 0:LANES]
    sin_lo = rp_ref[:, LANES:2 * LANES]
    sin_hi = rp_ref[:, 2 * LANES:3 * LANES]
    bdq = bdq_ref[...]
    bdk = bdq_ref[0:KV_WIDTH, 0:KV_WIDTH]

    q = _head_rms(z_ref[:, Q_OFF:Q_OFF + ATTN_WIDTH], bdq, qg_ref[...])
    reps = ATTN_WIDTH // LANES
    q = _rope(q, _tile_lanes(cos, reps), _tile_lanes(sin_lo, reps), _tile_lanes(sin_hi, reps))
    q_ref[...] = (q * (HEAD_DIM ** -0.5)).astype(BF16)

    k = _head_rms(z_ref[:, K_OFF:K_OFF + KV_WIDTH], bdk, kg_ref[...])
    k_ref[ATTN_BLOCK:ATTN_BLOCK + ts, :] = _rope(k, cos, sin_lo, sin_hi).astype(BF16)
    v_ref[ATTN_BLOCK:ATTN_BLOCK + ts, :] = z_ref[:, V_OFF:V_OFF + KV_WIDTH].astype(BF16)

    xh = jnp.concatenate([xp_ref[0], xn_ref[0]], axis=0)
    xhn = _rms(xh, nmix_ref[...]).astype(BF16)
    zh = jnp.dot(xhn, win_ref[:, K_OFF:K_OFF + 2 * KV_WIDTH], preferred_element_type=F32)
    kh = _head_rms(zh[:, 0:KV_WIDTH], bdk, kg_ref[...])
    rph = jnp.concatenate([rpp_ref[...], rpn_ref[...]], axis=0)
    kh = _rope(kh, rph[:, 0:LANES], rph[:, LANES:2 * LANES], rph[:, 2 * LANES:3 * LANES]).astype(BF16)
    vh = zh[:, KV_WIDTH:2 * KV_WIDTH].astype(BF16)
    k_ref[0:ATTN_BLOCK, :] = kh[0:ATTN_BLOCK]
    k_ref[ATTN_BLOCK + ts:2 * ATTN_BLOCK + ts, :] = kh[ATTN_BLOCK:]
    v_ref[0:ATTN_BLOCK, :] = vh[0:ATTN_BLOCK]
    v_ref[ATTN_BLOCK + ts:2 * ATTN_BLOCK + ts, :] = vh[ATTN_BLOCK:]

    rows = GQA_GROUP * ATTN_BLOCK
    keys = 3 * ATTN_BLOCK
    r = lax.broadcasted_iota(jnp.int32, (rows, keys), 0) % ATTN_BLOCK
    c = lax.broadcasted_iota(jnp.int32, (rows, keys), 1)
    band = (c >= r) & (c <= r + 2 * WINDOW)
    for qb in range(ts // ATTN_BLOCK):
        gb = i * (ts // ATTN_BLOCK) + qb
        lo = jnp.where(gb == 0, ATTN_BLOCK, 0)
        hi = jnp.where(gb == n_blocks_seq - 1, 2 * ATTN_BLOCK, keys)
        valid = band & (c >= lo) & (c < hi)
        r0 = qb * ATTN_BLOCK
        for j in range(N_KV_HEADS):
            q4 = jnp.concatenate(
                [q_ref[r0:r0 + ATTN_BLOCK, (GQA_GROUP * j + g) * HEAD_DIM:(GQA_GROUP * j + g + 1) * HEAD_DIM]
                 for g in range(GQA_GROUP)], axis=0)
            kw = k_ref[r0:r0 + keys, j * HEAD_DIM:(j + 1) * HEAD_DIM]
            vw = v_ref[r0:r0 + keys, j * HEAD_DIM:(j + 1) * HEAD_DIM]
            s = lax.dot_general(q4, kw, (((1,), (1,)), ((), ())), preferred_element_type=F32)
            s = jnp.where(valid, s, -jnp.inf)
            sk = sink_ref[j]
            m = jnp.maximum(jnp.max(s, axis=-1, keepdims=True), sk)
            p = jnp.exp(s - m)
            denom = jnp.sum(p, axis=-1, keepdims=True) + jnp.exp(sk - m)
            o = jnp.dot(p.astype(BF16), vw, preferred_element_type=F32) / denom
            for g in range(GQA_GROUP):
                h = GQA_GROUP * j + g
                attn_ref[r0:r0 + ATTN_BLOCK, h * HEAD_DIM:(h + 1) * HEAD_DIM] = (
                    o[g * ATTN_BLOCK:(g + 1) * ATTN_BLOCK].astype(BF16))

    half = GMLP_WIDTH // 2
    gpm = half // GMLP_GROUP_DIM
    lane_grp = lax.broadcasted_iota(jnp.int32, (GMLP_CHUNK, half), 1) // GMLP_GROUP_DIM
    for ch in range(ts // GMLP_CHUNK):
        c0 = ch * GMLP_CHUNK
        vg = _gelu(z_ref[c0:c0 + GMLP_CHUNK, VG_OFF:VG_OFF + GMLP_WIDTH])
        vgn = _head_rms(vg, bdq, vgain_ref[...])
        mixed = []
        for nt in range(2):
            part = vgn[:, nt * half:(nt + 1) * half]
            vexp = jnp.concatenate(
                [jnp.where(lane_grp == gl, part, 0.0).astype(BF16) for gl in range(gpm)], axis=0)
            wpart = wcat_ref[:, nt * gpm * GMLP_CHUNK:(nt + 1) * gpm * GMLP_CHUNK]
            mixed.append(jnp.dot(wpart, vexp, preferred_element_type=F32))
        mixed = jnp.concatenate(mixed, axis=-1) + bias_ref[...]
        u = _gelu(z_ref[c0:c0 + GMLP_CHUNK, U_OFF:U_OFF + GMLP_WIDTH])
        gm_ref[c0:c0 + GMLP_CHUNK, :] = (u * mixed).astype(BF16)

    ya = jnp.dot(attn_ref[...], wbr_ref[0], preferred_element_type=F32)
    yg = jnp.dot(gm_ref[...], wbr_ref[1], preferred_element_type=F32)
    ga = jax.nn.sigmoid(z_ref[:, GL_OFF:GL_OFF + D_MODEL])
    gg = jax.nn.sigmoid(z_ref[:, GL_OFF + D_MODEL:GL_OFF + 2 * D_MODEL])
    merged = (ga * ya + gg * yg).astype(BF16)
    h1 = x + jnp.dot(merged, wout_ref[...], preferred_element_type=F32)
    h1_ref[0] = h1

    hn = _rms(h1, nffn_ref[...])
    _store_packed(hn_ref, hn)

    hn_hi = hn.astype(BF16)
    hn_lo = (hn - hn_hi.astype(F32)).astype(BF16)
    w_hi = wr_ref[0]
    w_lo = wr_ref[1]
    logits = (jnp.dot(hn_hi, w_hi, preferred_element_type=F32)
              + jnp.dot(hn_hi, w_lo, preferred_element_type=F32)
              + jnp.dot(hn_lo, w_hi, preferred_element_type=F32)) + br_ref[...]
    lt = jnp.transpose(logits)[0:N_EXPERTS, :]
    eid = lax.broadcasted_iota(jnp.int32, lt.shape, 0)
    vals, ids = [], []
    for _ in range(TOP_K):
        mx = jnp.max(lt, axis=0, keepdims=True)
        am = jnp.min(jnp.where(lt == mx, eid, N_EXPERTS), axis=0, keepdims=True)
        vals.append(mx)
        ids.append(am)
        lt = jnp.where(eid == am, -jnp.inf, lt)
    ex = [jnp.exp(v - vals[0]) for v in vals]
    tot = ex[0] + ex[1] + ex[2] + ex[3]
    g8 = jnp.concatenate([e / tot for e in ex] + [jnp.zeros_like(tot)] * (8 - TOP_K), axis=0)
    gates = jnp.concatenate([g8, jnp.zeros((LANES - 8, ts), F32)], axis=0)
    gate_ref[0] = jnp.transpose(gates)

    @pl.when((pl.program_id(0) == 0) & (i == 0))
    def _():
        cnt_ref[...] = jnp.zeros_like(cnt_ref)

    sel = [eid == a for a in ids]
    member = (sel[0] | sel[1] | sel[2] | sel[3]).astype(F32)
    before = jnp.dot(member.astype(BF16), tri_ref[...], preferred_element_type=F32) + cnt_ref[:, 0:1]
    ranks = [jnp.sum(jnp.where(s_, before, 0.0), axis=0, keepdims=True).astype(jnp.int32) for s_ in sel]
    route_ref[0] = jnp.concatenate(ids + ranks, axis=0)
    cnt_ref[...] = cnt_ref[...] + jnp.sum(member, axis=1, keepdims=True)
    cnt_out_ref[...] = cnt_ref[...]


def _mixer_call(x, rope_tab, prm):
    b, s, d = x.shape
    ts = MIX_TILE
    nt = s // ts
    nb = s // ATTN_BLOCK
    per = ts // ATTN_BLOCK
    const2 = lambda bi, i: (0, 0)
    const3 = lambda bi, i: (0, 0, 0)

    def wspec(arr):
        return pl.BlockSpec(arr.shape, const2 if arr.ndim == 2 else const3)

    in_specs = [
        pl.BlockSpec((1, ts, d), lambda bi, i: (bi, i, 0)),
        pl.BlockSpec((1, ATTN_BLOCK, d), lambda bi, i: (bi, jnp.maximum(i * per - 1, 0), 0)),
        pl.BlockSpec((1, ATTN_BLOCK, d), lambda bi, i: (bi, jnp.minimum((i + 1) * per, nb - 1), 0)),
        pl.BlockSpec((ts, 3 * LANES), lambda bi, i: (i, 0)),
        pl.BlockSpec((ATTN_BLOCK, 3 * LANES), lambda bi, i: (jnp.maximum(i * per - 1, 0), 0)),
        pl.BlockSpec((ATTN_BLOCK, 3 * LANES), lambda bi, i: (jnp.minimum((i + 1) * per, nb - 1), 0)),
    ] + [wspec(a) for a in prm]
    out_shape = [
        jax.ShapeDtypeStruct((b, s, d), F32),
        jax.ShapeDtypeStruct((b * s * SUB, LANES), U32),
        jax.ShapeDtypeStruct((b, 2 * TOP_K, s), jnp.int32),
        jax.ShapeDtypeStruct((b, s, LANES), F32),
        jax.ShapeDtypeStruct((N_EXPERTS, LANES), F32),
    ]
    out_specs = [
        pl.BlockSpec((1, ts, d), lambda bi, i: (bi, i, 0)),
        pl.BlockSpec((ts * SUB, LANES), lambda bi, i: (bi * nt + i, 0)),
        pl.BlockSpec((1, 2 * TOP_K, ts), lambda bi, i: (bi, 0, i)),
        pl.BlockSpec((1, ts, LANES), lambda bi, i: (bi, i, 0)),
        pl.BlockSpec((N_EXPERTS, LANES), const2),
    ]
    scratch = [
        pltpu.VMEM((ts, IN_WIDTH), F32),
        pltpu.VMEM((ts, ATTN_WIDTH), BF16),
        pltpu.VMEM((ts + 2 * ATTN_BLOCK, KV_WIDTH), BF16),
        pltpu.VMEM((ts + 2 * ATTN_BLOCK, KV_WIDTH), BF16),
        pltpu.VMEM((ts, ATTN_WIDTH), BF16),
        pltpu.VMEM((ts, GMLP_WIDTH), BF16),
        pltpu.VMEM((N_EXPERTS, LANES), F32),
    ]
    return pl.pallas_call(
        functools.partial(_mixer_kernel, n_blocks_seq=nb),
        grid=(b, nt),
        in_specs=in_specs,
        out_specs=out_specs,
        out_shape=out_shape,
        scratch_shapes=scratch,
        compiler_params=pltpu.CompilerParams(
            dimension_semantics=("arbitrary", "arbitrary"), vmem_limit_bytes=VMEM_LIMIT),
        name="mixer",
    )(x, x, x, rope_tab, rope_tab, rope_tab, *prm)


def _sc_workers():
    info = plsc.get_sparse_core_info()
    return info.num_cores, info.num_cores * info.num_subcores


def _sc_dispatch(rows2d, slot_rows, n_out_rows):
    n_cores, n_workers = _sc_workers()
    chunk_rows = DISPATCH_TOKENS * SUB
    halves = chunk_rows // SC_CHUNK
    n_chunks = rows2d.shape[0] // chunk_rows
    per_w = n_chunks // n_workers
    idx_rows_per_choice = rows2d.shape[0] // SC_CHUNK
    mesh = plsc.VectorSubcoreMesh(core_axis_name="c", subcore_axis_name="s")

    @functools.partial(
        pl.kernel, mesh=mesh,
        out_type=jax.ShapeDtypeStruct((n_out_rows, LANES), rows2d.dtype),
        scratch_types=[pltpu.VMEM((TOP_K * halves, SC_CHUNK), jnp.int32),
                       pltpu.VMEM((chunk_rows, LANES), rows2d.dtype),
                       pltpu.SemaphoreType.DMA, pltpu.SemaphoreType.DMA],
    )
    def k(src_hbm, idx_hbm, out_hbm, idx_v, rows_v, sem_in, sem_out):
        wid = lax.axis_index("s") * n_cores + lax.axis_index("c")

        @pl.loop(0, per_w)
        def _(j):
            c = wid * per_w + j
            loads = [pltpu.async_copy(src_hbm.at[pl.ds(c * chunk_rows, chunk_rows)], rows_v, sem_in)]
            loads += [
                pltpu.async_copy(idx_hbm.at[pl.ds(kk * idx_rows_per_choice + c * halves, halves)],
                                 idx_v.at[pl.ds(kk * halves, halves)], sem_in)
                for kk in range(TOP_K)]
            for cp in loads:
                cp.wait()
            copies = [
                pltpu.async_copy(rows_v.at[pl.ds((q % halves) * SC_CHUNK, SC_CHUNK)],
                                 out_hbm.at[idx_v.at[q]], sem_out)
                for q in range(TOP_K * halves)]
            for cp in copies:
                cp.wait()

    return k(rows2d, slot_rows)


def _sc_gather(table2d, idx):
    n_cores, n_workers = _sc_workers()
    n = idx.shape[0]
    per_w = n // n_workers
    mesh = plsc.VectorSubcoreMesh(core_axis_name="c", subcore_axis_name="s")

    @functools.partial(
        pl.kernel, mesh=mesh,
        out_type=jax.ShapeDtypeStruct((n, LANES), table2d.dtype),
        scratch_types=[pltpu.VMEM((SC_CHUNK,), jnp.int32),
                       pltpu.VMEM((SC_CHUNK, LANES), table2d.dtype),
                       pltpu.SemaphoreType.DMA],
    )
    def k(table_hbm, idx_hbm, out_hbm, idx_v, rows_v, sem):
        wid = lax.axis_index("s") * n_cores + lax.axis_index("c")

        @pl.loop(0, per_w // SC_CHUNK)
        def _(j):
            base = wid * per_w + j * SC_CHUNK
            pltpu.sync_copy(idx_hbm.at[pl.ds(base, SC_CHUNK)], idx_v)
            pltpu.async_copy(table_hbm.at[idx_v], rows_v, sem).wait()
            pltpu.sync_copy(rows_v, out_hbm.at[pl.ds(base, SC_CHUNK)])

    return k(table2d, idx)


def _moe_kernel(be_ref, nv_ref, x_ref, wgu_ref, bgu_ref, wd_ref, bd_ref, y_ref, wgu_s, wd_s, slab):
    i = pl.program_id(0)
    nv = nv_ref[i]

    @pl.when((i == 0) | (be_ref[i] != be_ref[jnp.maximum(i - 1, 0)]))
    def _():
        wgu_s[...] = wgu_ref[0].astype(BF16)
        half = D_FF // 2
        for c in range(D_MODEL // LANES):
            slab[c, pl.ds(0, half, stride=2), :] = wd_ref[0, 0:half, c * LANES:(c + 1) * LANES]
            slab[c, pl.ds(1, half, stride=2), :] = wd_ref[0, half:D_FF, c * LANES:(c + 1) * LANES]
        for c in range(D_MODEL // LANES):
            wd_s[:, c * LANES:(c + 1) * LANES] = slab[c].astype(BF16)

    @pl.when(nv > 0)
    def _():
        x = _load_packed(x_ref, MOE_BLOCK)
        live = lax.broadcasted_iota(jnp.int32, x.shape, 0) < nv
        xe = jnp.where(live, x, 0.0).astype(BF16)
        h = jnp.dot(xe, wgu_s[...], preferred_element_type=F32) + bgu_ref[0]
        up = pltpu.roll(h, 2 * D_FF - 1, 1)
        gate = jnp.minimum(h, SWIGLU_LIMIT)
        up = jnp.clip(up, -SWIGLU_LIMIT, SWIGLU_LIMIT)
        act = (up + 1.0) * (gate * jax.nn.sigmoid(SWIGLU_ALPHA * gate))
        even = (lax.broadcasted_iota(jnp.int32, (MOE_BLOCK, D_FF), 1) % 2) == 0
        packed = jnp.where(even, act[:, 0:D_FF], pltpu.roll(act[:, D_FF:2 * D_FF], 1, 1))
        y = jnp.dot(packed.astype(BF16), wd_s[...], preferred_element_type=F32) + bd_ref[0]
        _store_packed(y_ref, y)


def _moe_call(block_e, n_valid, xb, wgu, bgu, wd, bd):
    n_blocks = block_e.shape[0]
    d = D_MODEL
    grid_spec = pltpu.PrefetchScalarGridSpec(
        num_scalar_prefetch=2,
        grid=(n_blocks,),
        in_specs=[
            pl.BlockSpec((MOE_BLOCK * SUB, LANES), lambda i, be, nv: (i, 0)),
            pl.BlockSpec((1, d, 2 * D_FF), lambda i, be, nv: (be[i], 0, 0)),
            pl.BlockSpec((1, 1, 2 * D_FF), lambda i, be, nv: (be[i], 0, 0)),
            pl.BlockSpec((1, D_FF, d), lambda i, be, nv: (be[i], 0, 0)),
            pl.BlockSpec((1, 1, d), lambda i, be, nv: (be[i], 0, 0)),
        ],
        out_specs=pl.BlockSpec((MOE_BLOCK * SUB, LANES), lambda i, be, nv: (i, 0)),
        scratch_shapes=[pltpu.VMEM((d, 2 * D_FF), BF16), pltpu.VMEM((D_FF, d), BF16),
                        pltpu.VMEM((d // LANES, D_FF, LANES), F32)],
    )
    return pl.pallas_call(
        _moe_kernel,
        grid_spec=grid_spec,
        out_shape=jax.ShapeDtypeStruct((n_blocks * MOE_BLOCK * SUB, LANES), U32),
        compiler_params=pltpu.CompilerParams(
            dimension_semantics=("arbitrary",), vmem_limit_bytes=VMEM_LIMIT),
        name="moe",
    )(block_e, n_valid, xb, wgu, bgu, wd, bd)


def _combine_kernel(g0_ref, g1_ref, g2_ref, g3_ref, h1_ref, gate_ref, p_ref, nple_ref, wpg_ref, wpp_ref, o_ref):
    g = gate_ref[...]
    moe = jnp.zeros((CMB_TILE, D_MODEL), F32)
    for kk, g_ref in enumerate((g0_ref, g1_ref, g2_ref, g3_ref)):
        moe = moe + _load_packed(g_ref, CMB_TILE) * g[:, kk:kk + 1]
    h2 = h1_ref[...] + moe
    hp = _rms(h2, nple_ref[...]).astype(BF16)
    gate = jax.nn.sigmoid(jnp.dot(hp, wpg_ref[...], preferred_element_type=F32))
    proj = jnp.dot(p_ref[...].astype(BF16), wpp_ref[...], preferred_element_type=F32)
    o_ref[...] = h2 + gate * proj


def _combine_call(gathered, h1_flat, gates, p_flat, nple, wpg, wpp):
    n_tok, d = h1_flat.shape
    tc = CMB_TILE
    n_tiles = n_tok // tc
    row = lambda i: (i, 0)
    const = lambda i: (0, 0)
    g_specs = [pl.BlockSpec((tc * SUB, LANES), functools.partial(lambda i, kk: (kk * n_tiles + i, 0), kk=kk))
               for kk in range(TOP_K)]
    return pl.pallas_call(
        _combine_kernel,
        grid=(n_tiles,),
        in_specs=g_specs + [
            pl.BlockSpec((tc, d), row),
            pl.BlockSpec((tc, LANES), row),
            pl.BlockSpec((tc, PLE_DIM), row),
            pl.BlockSpec((1, d), const),
            pl.BlockSpec((d, d), const),
            pl.BlockSpec((PLE_DIM, d), const),
        ],
        out_specs=pl.BlockSpec((tc, d), row),
        out_shape=jax.ShapeDtypeStruct((n_tok, d), F32),
        compiler_params=pltpu.CompilerParams(
            dimension_semantics=("arbitrary",), vmem_limit_bytes=VMEM_LIMIT),
        name="combine",
    )(gathered, gathered, gathered, gathered, h1_flat, gates, p_flat, nple, wpg, wpp)


def _rope_table(s):
    half = ROPE_DIM // 2
    inv_freq = jnp.power(ROPE_THETA, -jnp.arange(half, dtype=F32) * (2.0 / ROPE_DIM))
    ang = jnp.arange(s, dtype=F32)[:, None] * inv_freq[None, :]
    cos, sin = jnp.cos(ang), jnp.sin(ang)
    pad1 = jnp.ones((s, HEAD_DIM - ROPE_DIM), F32)
    pad0 = jnp.zeros((s, HEAD_DIM - half), F32)
    c = jnp.concatenate([cos, cos, pad1], axis=-1)
    s_lo = jnp.concatenate([-sin, pad0], axis=-1)
    s_hi = jnp.concatenate([jnp.zeros((s, half), F32), sin, pad0[:, half:]], axis=-1)
    two = lambda t: jnp.concatenate([t, t], axis=-1)
    return jnp.concatenate([two(c), two(s_lo), two(s_hi)], axis=-1)


def _routing(route, counts, n_tok):
    ids = jnp.transpose(route[:, 0:TOP_K, :], (1, 0, 2)).reshape(TOP_K, n_tok)
    ranks = jnp.transpose(route[:, TOP_K:2 * TOP_K, :], (1, 0, 2)).reshape(TOP_K, n_tok)
    counts = counts.astype(jnp.int32)
    padded = (counts + MOE_BLOCK - 1) // MOE_BLOCK * MOE_BLOCK
    pends = jnp.cumsum(padded)
    pstarts = pends - padded
    pos = ranks
    for e in range(N_EXPERTS):
        pos = pos + jnp.where(ids == e, pstarts[e], 0)
    n_slots = -(-(n_tok * TOP_K) // MOE_BLOCK) * MOE_BLOCK + N_EXPERTS * MOE_BLOCK
    n_blocks = n_slots // MOE_BLOCK
    first = jnp.arange(n_blocks, dtype=jnp.int32) * MOE_BLOCK
    block_e = jnp.clip(jnp.sum((first[:, None] >= pends[None, :]).astype(jnp.int32), axis=1), 0, N_EXPERTS - 1)
    own = block_e[:, None] == jnp.arange(N_EXPERTS, dtype=jnp.int32)[None, :]
    left = jnp.sum(jnp.where(own, (counts + pstarts)[None, :], 0), axis=1) - first
    n_valid = jnp.clip(left, 0, MOE_BLOCK)
    return pos, block_e, n_valid.astype(jnp.int32), n_slots


def _expand_rows(pos):
    spread = (jnp.arange(SUB * LANES, dtype=jnp.int32)[None, :] // SUB
              == jnp.arange(LANES, dtype=jnp.int32)[:, None]).astype(F32)
    wide = jnp.dot(pos.reshape(-1, LANES).astype(F32), spread, precision=lax.Precision.HIGHEST)
    wide = wide.astype(jnp.int32) * SUB + (jnp.arange(SUB * LANES, dtype=jnp.int32) % SUB)[None, :]
    return wide.reshape(-1, LANES)


def _group_forward(x, p_l, rope_tab, mix_prm, moe_prm, ple_prm):
    b, s, d = x.shape
    n_tok = b * s
    h1, hn_rows, route, gates, counts = _mixer_call(x, rope_tab, mix_prm)
    pos, block_e, n_valid, n_slots = _routing(route, counts[:, 0], n_tok)
    slot_rows = _expand_rows(pos)
    xb = _sc_dispatch(hn_rows, slot_rows, n_slots * SUB)
    y = _moe_call(block_e, n_valid, xb, *moe_prm)
    gathered = _sc_gather(y, slot_rows.reshape(-1))
    out = _combine_call(gathered, h1.reshape(n_tok, d), gates.reshape(n_tok, LANES),
                        p_l.reshape(n_tok, PLE_DIM), *ple_prm)
    return out.reshape(b, s, d)


def kernel(x_prompt, x_sample, p_prompt, p_sample, norm_mix, w_in, q_gain, k_gain, attn_sink, gmlp_v_gain, gmlp_w_s, gmlp_b_s, w_branch, w_out, norm_ffn, w_router, b_router, w_gate_up, b_gate_up, w_down, b_down, norm_ple, w_ple_gate, w_ple_proj):
    depth = norm_mix.shape[0]
    hp, hs = x_prompt, x_sample
    for l in range(depth):
        row = lambda a: a.reshape(1, -1)
        blockdiag = jnp.kron(jnp.eye(N_Q_HEADS, dtype=F32),
                             jnp.full((HEAD_DIM, HEAD_DIM), 1.0 / HEAD_DIM, F32)).astype(BF16)
        sink_rows = jnp.repeat(attn_sink[l].reshape(N_KV_HEADS, GQA_GROUP), ATTN_BLOCK, axis=1)[..., None]
        wcat = jnp.transpose(gmlp_w_s[l], (1, 0, 2)).reshape(GMLP_CHUNK, GMLP_GROUPS * GMLP_CHUNK).astype(BF16)
        bias_full = jnp.repeat(gmlp_b_s[l].T, GMLP_GROUP_DIM, axis=1)
        wr = jnp.pad(w_router[l], ((0, 0), (0, ROUTER_PAD - N_EXPERTS)))
        wr_hi = wr.astype(BF16)
        wr_lo = (wr - wr_hi.astype(F32)).astype(BF16)
        br = jnp.pad(b_router[l], (0, ROUTER_PAD - N_EXPERTS)).reshape(1, ROUTER_PAD)
        tri = (jnp.arange(MIX_TILE)[:, None] < jnp.arange(MIX_TILE)[None, :]).astype(BF16)
        mix_prm = (
            row(norm_mix[l]), w_in[l].astype(BF16),
            row(jnp.tile(q_gain[l], N_Q_HEADS)), row(jnp.tile(k_gain[l], N_KV_HEADS)),
            sink_rows, row(gmlp_v_gain[l]), blockdiag, wcat, bias_full,
            w_branch[l].astype(BF16), w_out[l].astype(BF16), row(norm_ffn[l]),
            jnp.stack([wr_hi, wr_lo]), br, tri,
        )
        moe_prm = (
            w_gate_up[l], b_gate_up[l].reshape(N_EXPERTS, 1, 2 * D_FF),
            w_down[l], b_down[l].reshape(N_EXPERTS, 1, D_MODEL),
        )
        ple_prm = (row(norm_ple[l]), w_ple_gate[l].astype(BF16), w_ple_proj[l].astype(BF16))
        rope_tab = _rope_table(max(hp.shape[1], hs.shape[1]))
        hp = _group_forward(hp, p_prompt[l], rope_tab, mix_prm, moe_prm, ple_prm)
        hs = _group_forward(hs, p_sample[l], rope_tab, mix_prm, moe_prm, ple_prm)
    return (hp.astype(x_prompt.dtype), hs.astype(x_sample.dtype))
```

```python
import functools

import jax
import jax.numpy as jnp
import numpy as np
from jax import lax
from jax.experimental import pallas as pl
from jax.experimental.pallas import tpu as pltpu
from jax.experimental.pallas import tpu_sc as plsc

D_MODEL = 1024
HEAD_DIM = 64
N_Q_HEADS = 8
N_KV_HEADS = 2
GQA_GROUP = N_Q_HEADS // N_KV_HEADS
ATTN_WIDTH = N_Q_HEADS * HEAD_DIM
KV_WIDTH = N_KV_HEADS * HEAD_DIM
WINDOW = 128
ATTN_BLOCK = 128
ROPE_THETA = 500000.0
ROPE_DIM = HEAD_DIM // 4
GMLP_WIDTH = D_MODEL // 2
GMLP_GROUPS = 8
GMLP_GROUP_DIM = GMLP_WIDTH // GMLP_GROUPS
GMLP_CHUNK = 128
N_BRANCH = 2
IN_WIDTH = ATTN_WIDTH + 2 * KV_WIDTH + 2 * GMLP_WIDTH + N_BRANCH * D_MODEL
N_EXPERTS = 32
TOP_K = 4
D_FF = D_MODEL
SWIGLU_ALPHA = 1.702
SWIGLU_LIMIT = 7.0
PLE_DIM = 256
EPS = 1e-6

Q_OFF = 0
K_OFF = ATTN_WIDTH
V_OFF = K_OFF + KV_WIDTH
U_OFF = V_OFF + KV_WIDTH
VG_OFF = U_OFF + GMLP_WIDTH
GL_OFF = VG_OFF + GMLP_WIDTH

LANES = 128
PACK_WORDS = D_MODEL // 2
SUB = PACK_WORDS // LANES
MIX_TILE = 512
MOE_BLOCK = 256
MOE_COLS = 256
CMB_TILE = 256
ROUTER_PAD = 128
SC_CHUNK = 128
DISPATCH_TOKENS = 64
VMEM_LIMIT = 56 * 1024 * 1024

BF16 = jnp.bfloat16
F32 = jnp.float32
U32 = jnp.uint32


def _rms(x, gain):
    return x * lax.rsqrt(jnp.mean(x * x, axis=-1, keepdims=True) + EPS) * gain


def _gelu(x):
    return 0.5 * x * (1.0 + lax.erf(x * np.float32(np.sqrt(0.5))))


def _head_rms(x, blockdiag, gain):
    ms = jnp.dot((x * x).astype(BF16), blockdiag, preferred_element_type=F32)
    return x * lax.rsqrt(ms + EPS) * gain


def _rope(x, cos, sin_lo, sin_hi):
    w = x.shape[-1]
    return x * cos + pltpu.roll(x, w - ROPE_DIM // 2, 1) * sin_lo + pltpu.roll(x, ROPE_DIM // 2, 1) * sin_hi


def _tile_lanes(t, reps):
    return t if reps == 1 else jnp.concatenate([t] * reps, axis=-1)


def _store_packed(ref, x):
    rows = x.shape[0]
    hi = lax.bitcast_convert_type(x[:, :PACK_WORDS].astype(BF16).astype(F32), U32)
    lo = lax.bitcast_convert_type(x[:, PACK_WORDS:].astype(BF16).astype(F32), U32)
    words = hi | (lo >> 16)
    for j in range(SUB):
        ref[pl.ds(j, rows, stride=SUB), :] = words[:, j * LANES:(j + 1) * LANES]


def _load_packed(ref, rows):
    words = jnp.concatenate([ref[pl.ds(j, rows, stride=SUB), :] for j in range(SUB)], axis=1)
    left = lax.bitcast_convert_type(words & np.uint32(0xFFFF0000), F32)
    right = lax.bitcast_convert_type(words << 16, F32)
    return jnp.concatenate([left, right], axis=1)


def _mixer_kernel(x_ref, xp_ref, xn_ref, rp_ref, rpp_ref, rpn_ref,
                  nmix_ref, win_ref, qg_ref, kg_ref, sink_ref, vgain_ref, bdq_ref,
                  wcat_ref, bias_ref, wbr_ref, wout_ref, nffn_ref, wr_ref, br_ref, tri_ref,
                  h1_ref, hn_ref, route_ref, gate_ref, cnt_out_ref,
                  z_ref, q_ref, k_ref, v_ref, attn_ref, gm_ref, cnt_ref, *, n_blocks_seq):
    ts = MIX_TILE
    i = pl.program_id(1)
    x = x_ref[0]
    xn = _rms(x, nmix_ref[...]).astype(BF16)
    z_ref[...] = jnp.dot(xn, win_ref[...], preferred_element_type=F32)

    cos = rp_ref[:, 0:LANES]
    sin_lo = rp_ref[:, LANES:2 * LANES]
    sin_hi = rp_ref[:, 2 * LANES:3 * LANES]
    bdq = bdq_ref[...]
    bdk = bdq_ref[0:KV_WIDTH, 0:KV_WIDTH]

    q = _head_rms(z_ref[:, Q_OFF:Q_OFF + ATTN_WIDTH], bdq, qg_ref[...])
    reps = ATTN_WIDTH // LANES
    q = _rope(q, _tile_lanes(cos, reps), _tile_lanes(sin_lo, reps), _tile_lanes(sin_hi, reps))
    q_ref[...] = (q * (HEAD_DIM ** -0.5)).astype(BF16)

    k = _head_rms(z_ref[:, K_OFF:K_OFF + KV_WIDTH], bdk, kg_ref[...])
    k_ref[ATTN_BLOCK:ATTN_BLOCK + ts, :] = _rope(k, cos, sin_lo, sin_hi).astype(BF16)
    v_ref[ATTN_BLOCK:ATTN_BLOCK + ts, :] = z_ref[:, V_OFF:V_OFF + KV_WIDTH].astype(BF16)

    xh = jnp.concatenate([xp_ref[0], xn_ref[0]], axis=0)
    xhn = _rms(xh, nmix_ref[...]).astype(BF16)
    zh = jnp.dot(xhn, win_ref[:, K_OFF:K_OFF + 2 * KV_WIDTH], preferred_element_type=F32)
    kh = _head_rms(zh[:, 0:KV_WIDTH], bdk, kg_ref[...])
    rph = jnp.concatenate([rpp_ref[...], rpn_ref[...]], axis=0)
    kh = _rope(kh, rph[:, 0:LANES], rph[:, LANES:2 * LANES], rph[:, 2 * LANES:3 * LANES]).astype(BF16)
    vh = zh[:, KV_WIDTH:2 * KV_WIDTH].astype(BF16)
    k_ref[0:ATTN_BLOCK, :] = kh[0:ATTN_BLOCK]
    k_ref[ATTN_BLOCK + ts:2 * ATTN_BLOCK + ts, :] = kh[ATTN_BLOCK:]
    v_ref[0:ATTN_BLOCK, :] = vh[0:ATTN_BLOCK]
    v_ref[ATTN_BLOCK + ts:2 * ATTN_BLOCK + ts, :] = vh[ATTN_BLOCK:]

    rows = GQA_GROUP * ATTN_BLOCK
    keys = 3 * ATTN_BLOCK
    r = lax.broadcasted_iota(jnp.int32, (rows, keys), 0) % ATTN_BLOCK
    c = lax.broadcasted_iota(jnp.int32, (rows, keys), 1)
    band = (c >= r) & (c <= r + 2 * WINDOW)
    for qb in range(ts // ATTN_BLOCK):
        gb = i * (ts // ATTN_BLOCK) + qb
        lo = jnp.where(gb == 0, ATTN_BLOCK, 0)
        hi = jnp.where(gb == n_blocks_seq - 1, 2 * ATTN_BLOCK, keys)
        valid = band & (c >= lo) & (c < hi)
        r0 = qb * ATTN_BLOCK
        for j in range(N_KV_HEADS):
            q4 = jnp.concatenate(
                [q_ref[r0:r0 + ATTN_BLOCK, (GQA_GROUP * j + g) * HEAD_DIM:(GQA_GROUP * j + g + 1) * HEAD_DIM]
                 for g in range(GQA_GROUP)], axis=0)
            kw = k_ref[r0:r0 + keys, j * HEAD_DIM:(j + 1) * HEAD_DIM]
            vw = v_ref[r0:r0 + keys, j * HEAD_DIM:(j + 1) * HEAD_DIM]
            s = lax.dot_general(q4, kw, (((1,), (1,)), ((), ())), preferred_element_type=F32)
            s = jnp.where(valid, s, -jnp.inf)
            sk = sink_ref[j]
            m = jnp.maximum(jnp.max(s, axis=-1, keepdims=True), sk)
            p = jnp.exp(s - m)
            denom = jnp.sum(p, axis=-1, keepdims=True) + jnp.exp(sk - m)
            o = jnp.dot(p.astype(BF16), vw, preferred_element_type=F32) / denom
            for g in range(GQA_GROUP):
                h = GQA_GROUP * j + g
                attn_ref[r0:r0 + ATTN_BLOCK, h * HEAD_DIM:(h + 1) * HEAD_DIM] = (
                    o[g * ATTN_BLOCK:(g + 1) * ATTN_BLOCK].astype(BF16))

    half = GMLP_WIDTH // 2
    gpm = half // GMLP_GROUP_DIM
    lane_grp = lax.broadcasted_iota(jnp.int32, (GMLP_CHUNK, half), 1) // GMLP_GROUP_DIM
    for ch in range(ts // GMLP_CHUNK):
        c0 = ch * GMLP_CHUNK
        vg = _gelu(z_ref[c0:c0 + GMLP_CHUNK, VG_OFF:VG_OFF + GMLP_WIDTH])
        vgn = _head_rms(vg, bdq, vgain_ref[...])
        mixed = []
        for nt in range(2):
            part = vgn[:, nt * half:(nt + 1) * half]
            vexp = jnp.concatenate(
                [jnp.where(lane_grp == gl, part, 0.0).astype(BF16) for gl in range(gpm)], axis=0)
            wpart = wcat_ref[:, nt * gpm * GMLP_CHUNK:(nt + 1) * gpm * GMLP_CHUNK]
            mixed.append(jnp.dot(wpart, vexp, preferred_element_type=F32))
        mixed = jnp.concatenate(mixed, axis=-1) + bias_ref[...]
        u = _gelu(z_ref[c0:c0 + GMLP_CHUNK, U_OFF:U_OFF + GMLP_WIDTH])
        gm_ref[c0:c0 + GMLP_CHUNK, :] = (u * mixed).astype(BF16)

    ya = jnp.dot(attn_ref[...], wbr_ref[0], preferred_element_type=F32)
    yg = jnp.dot(gm_ref[...], wbr_ref[1], preferred_element_type=F32)
    ga = jax.nn.sigmoid(z_ref[:, GL_OFF:GL_OFF + D_MODEL])
    gg = jax.nn.sigmoid(z_ref[:, GL_OFF + D_MODEL:GL_OFF + 2 * D_MODEL])
    merged = (ga * ya + gg * yg).astype(BF16)
    h1 = x + jnp.dot(merged, wout_ref[...], preferred_element_type=F32)
    h1_ref[0] = h1

    hn = _rms(h1, nffn_ref[...])
    _store_packed(hn_ref, hn)

    hn_hi = hn.astype(BF16)
    hn_lo = (hn - hn_hi.astype(F32)).astype(BF16)
    w_hi = wr_ref[0]
    w_lo = wr_ref[1]
    logits = (jnp.dot(hn_hi, w_hi, preferred_element_type=F32)
              + jnp.dot(hn_hi, w_lo, preferred_element_type=F32)
              + jnp.dot(hn_lo, w_hi, preferred_element_type=F32)) + br_ref[...]
    lt = jnp.transpose(logits)[0:N_EXPERTS, :]
    eid = lax.broadcasted_iota(jnp.int32, lt.shape, 0)
    vals, ids = [], []
    for _ in range(TOP_K):
        mx = jnp.max(lt, axis=0, keepdims=True)
        am = jnp.min(jnp.where(lt == mx, eid, N_EXPERTS), axis=0, keepdims=True)
        vals.append(mx)
        ids.append(am)
        lt = jnp.where(eid == am, -jnp.inf, lt)
    ex = [jnp.exp(v - vals[0]) for v in vals]
    tot = ex[0] + ex[1] + ex[2] + ex[3]
    g8 = jnp.concatenate([e / tot for e in ex] + [jnp.zeros_like(tot)] * (8 - TOP_K), axis=0)
    gates = jnp.concatenate([g8, jnp.zeros((LANES - 8, ts), F32)], axis=0)
    gate_ref[0] = jnp.transpose(gates)

    @pl.when((pl.program_id(0) == 0) & (i == 0))
    def _():
        cnt_ref[...] = jnp.zeros_like(cnt_ref)

    sel = [eid == a for a in ids]
    member = (sel[0] | sel[1] | sel[2] | sel[3]).astype(F32)
    before = jnp.dot(member.astype(BF16), tri_ref[...], preferred_element_type=F32) + cnt_ref[:, 0:1]
    ranks = [jnp.sum(jnp.where(s_, before, 0.0), axis=0, keepdims=True).astype(jnp.int32) for s_ in sel]
    route_ref[0] = jnp.concatenate(ids + ranks, axis=0)
    cnt_ref[...] = cnt_ref[...] + jnp.sum(member, axis=1, keepdims=True)
    cnt_out_ref[...] = cnt_ref[...]


def _mixer_call(x, rope_tab, prm):
    b, s, d = x.shape
    ts = MIX_TILE
    nt = s // ts
    nb = s // ATTN_BLOCK
    per = ts // ATTN_BLOCK
    const2 = lambda bi, i: (0, 0)
    const3 = lambda bi, i: (0, 0, 0)

    def wspec(arr):
        return pl.BlockSpec(arr.shape, const2 if arr.ndim == 2 else const3)

    in_specs = [
        pl.BlockSpec((1, ts, d), lambda bi, i: (bi, i, 0)),
        pl.BlockSpec((1, ATTN_BLOCK, d), lambda bi, i: (bi, jnp.maximum(i * per - 1, 0), 0)),
        pl.BlockSpec((1, ATTN_BLOCK, d), lambda bi, i: (bi, jnp.minimum((i + 1) * per, nb - 1), 0)),
        pl.BlockSpec((ts, 3 * LANES), lambda bi, i: (i, 0)),
        pl.BlockSpec((ATTN_BLOCK, 3 * LANES), lambda bi, i: (jnp.maximum(i * per - 1, 0), 0)),
        pl.BlockSpec((ATTN_BLOCK, 3 * LANES), lambda bi, i: (jnp.minimum((i + 1) * per, nb - 1), 0)),
    ] + [wspec(a) for a in prm]
    out_shape = [
        jax.ShapeDtypeStruct((b, s, d), F32),
        jax.ShapeDtypeStruct((b * s * SUB, LANES), U32),
        jax.ShapeDtypeStruct((b, 2 * TOP_K, s), jnp.int32),
        jax.ShapeDtypeStruct((b, s, LANES), F32),
        jax.ShapeDtypeStruct((N_EXPERTS, LANES), F32),
    ]
    out_specs = [
        pl.BlockSpec((1, ts, d), lambda bi, i: (bi, i, 0)),
        pl.BlockSpec((ts * SUB, LANES), lambda bi, i: (bi * nt + i, 0)),
        pl.BlockSpec((1, 2 * TOP_K, ts), lambda bi, i: (bi, 0, i)),
        pl.BlockSpec((1, ts, LANES), lambda bi, i: (bi, i, 0)),
        pl.BlockSpec((N_EXPERTS, LANES), const2),
    ]
    scratch = [
        pltpu.VMEM((ts, IN_WIDTH), F32),
        pltpu.VMEM((ts, ATTN_WIDTH), BF16),
        pltpu.VMEM((ts + 2 * ATTN_BLOCK, KV_WIDTH), BF16),
        pltpu.VMEM((ts + 2 * ATTN_BLOCK, KV_WIDTH), BF16),
        pltpu.VMEM((ts, ATTN_WIDTH), BF16),
        pltpu.VMEM((ts, GMLP_WIDTH), BF16),
        pltpu.VMEM((N_EXPERTS, LANES), F32),
    ]
    return pl.pallas_call(
        functools.partial(_mixer_kernel, n_blocks_seq=nb),
        grid=(b, nt),
        in_specs=in_specs,
        out_specs=out_specs,
        out_shape=out_shape,
        scratch_shapes=scratch,
        compiler_params=pltpu.CompilerParams(
            dimension_semantics=("arbitrary", "arbitrary"), vmem_limit_bytes=VMEM_LIMIT),
        name="mixer",
    )(x, x, x, rope_tab, rope_tab, rope_tab, *prm)


def _sc_workers():
    info = plsc.get_sparse_core_info()
    return info.num_cores, info.num_cores * info.num_subcores


def _sc_dispatch(rows2d, slot_rows, n_out_rows):
    n_cores, n_workers = _sc_workers()
    chunk_rows = DISPATCH_TOKENS * SUB
    halves = chunk_rows // SC_CHUNK
    n_chunks = rows2d.shape[0] // chunk_rows
    per_w = n_chunks // n_workers
    idx_rows_per_choice = rows2d.shape[0] // SC_CHUNK
    mesh = plsc.VectorSubcoreMesh(core_axis_name="c", subcore_axis_name="s")

    @functools.partial(
        pl.kernel, mesh=mesh,
        out_type=jax.ShapeDtypeStruct((n_out_rows, LANES), rows2d.dtype),
        scratch_types=[pltpu.VMEM((TOP_K * halves, SC_CHUNK), jnp.int32),
                       pltpu.VMEM((chunk_rows, LANES), rows2d.dtype),
                       pltpu.SemaphoreType.DMA, pltpu.SemaphoreType.DMA],
    )
    def k(src_hbm, idx_hbm, out_hbm, idx_v, rows_v, sem_in, sem_out):
        wid = lax.axis_index("s") * n_cores + lax.axis_index("c")

        @pl.loop(0, per_w)
        def _(j):
            c = wid * per_w + j
            loads = [pltpu.async_copy(src_hbm.at[pl.ds(c * chunk_rows, chunk_rows)], rows_v, sem_in)]
            loads += [
                pltpu.async_copy(idx_hbm.at[pl.ds(kk * idx_rows_per_choice + c * halves, halves)],
                                 idx_v.at[pl.ds(kk * halves, halves)], sem_in)
                for kk in range(TOP_K)]
            for cp in loads:
                cp.wait()
            copies = [
                pltpu.async_copy(rows_v.at[pl.ds((q % halves) * SC_CHUNK, SC_CHUNK)],
                                 out_hbm.at[idx_v.at[q]], sem_out)
                for q in range(TOP_K * halves)]
            for cp in copies:
                cp.wait()

    return k(rows2d, slot_rows)


def _sc_gather(table2d, idx):
    n_cores, n_workers = _sc_workers()
    n = idx.shape[0]
    per_w = n // n_workers
    mesh = plsc.VectorSubcoreMesh(core_axis_name="c", subcore_axis_name="s")

    @functools.partial(
        pl.kernel, mesh=mesh,
        out_type=jax.ShapeDtypeStruct((n, LANES), table2d.dtype),
        scratch_types=[pltpu.VMEM((SC_CHUNK,), jnp.int32),
                       pltpu.VMEM((SC_CHUNK, LANES), table2d.dtype),
                       pltpu.SemaphoreType.DMA],
    )
    def k(table_hbm, idx_hbm, out_hbm, idx_v, rows_v, sem):
        wid = lax.axis_index("s") * n_cores + lax.axis_index("c")

        @pl.loop(0, per_w // SC_CHUNK)
        def _(j):
            base = wid * per_w + j * SC_CHUNK
            pltpu.sync_copy(idx_hbm.at[pl.ds(base, SC_CHUNK)], idx_v)
            pltpu.async_copy(table_hbm.at[idx_v], rows_v, sem).wait()
            pltpu.sync_copy(rows_v, out_hbm.at[pl.ds(base, SC_CHUNK)])

    return k(table2d, idx)


def _moe_kernel(be_ref, nv_ref, x_ref, wgu_ref, bgu_ref, wd_ref, bd_ref, y_ref, wgu_s, wd_s, slab):
    i = pl.program_id(0)
    nv = nv_ref[i]

    cw = MOE_COLS
    n_chunks = D_FF // cw

    @pl.when((i == 0) | (be_ref[i] != be_ref[jnp.maximum(i - 1, 0)]))
    def _():
        for c in range(n_chunks):
            wgu_s[:, 2 * c * cw:(2 * c + 1) * cw] = wgu_ref[0, :, c * cw:(c + 1) * cw].astype(BF16)
            wgu_s[:, (2 * c + 1) * cw:(2 * c + 2) * cw] = wgu_ref[0, :, D_FF + c * cw:D_FF + (c + 1) * cw].astype(BF16)
        half = D_FF // 2
        for c in range(D_MODEL // LANES):
            slab[c, pl.ds(0, half, stride=2), :] = wd_ref[0, 0:half, c * LANES:(c + 1) * LANES]
            slab[c, pl.ds(1, half, stride=2), :] = wd_ref[0, half:D_FF, c * LANES:(c + 1) * LANES]
        for c in range(D_MODEL // LANES):
            wd_s[:, c * LANES:(c + 1) * LANES] = slab[c].astype(BF16)

    @pl.when(nv > 0)
    def _():
        x = _load_packed(x_ref, MOE_BLOCK)
        live = lax.broadcasted_iota(jnp.int32, x.shape, 0) < nv
        xe = jnp.where(live, x, 0.0).astype(BF16)
        even = (lax.broadcasted_iota(jnp.int32, (MOE_BLOCK, cw), 1) % 2) == 0
        y = bd_ref[0]

        def gate_up(c):
            return jnp.dot(xe, wgu_s[:, 2 * c * cw:(2 * c + 2) * cw], preferred_element_type=F32)

        h_next = gate_up(0)
        for c in range(n_chunks):
            h = h_next
            if c + 1 < n_chunks:
                h_next = gate_up(c + 1)
            h_a = h[:, 0:cw] + bgu_ref[0, :, c * cw:(c + 1) * cw]
            h_b = h[:, cw:2 * cw] + bgu_ref[0, :, D_FF + c * cw:D_FF + (c + 1) * cw]
            gate = jnp.where(even, h_a, pltpu.roll(h_b, 1, 1))
            up = jnp.where(even, pltpu.roll(h_a, cw - 1, 1), h_b)
            gate = jnp.minimum(gate, SWIGLU_LIMIT)
            up = jnp.clip(up, -SWIGLU_LIMIT, SWIGLU_LIMIT)
            act = (up + 1.0) * (gate * jax.nn.sigmoid(SWIGLU_ALPHA * gate))
            y = y + jnp.dot(act.astype(BF16), wd_s[c * cw:(c + 1) * cw, :], preferred_element_type=F32)
        _store_packed(y_ref, y)


def _moe_call(block_e, n_valid, xb, wgu, bgu, wd, bd):
    n_blocks = block_e.shape[0]
    d = D_MODEL
    grid_spec = pltpu.PrefetchScalarGridSpec(
        num_scalar_prefetch=2,
        grid=(n_blocks,),
        in_specs=[
            pl.BlockSpec((MOE_BLOCK * SUB, LANES), lambda i, be, nv: (i, 0)),
            pl.BlockSpec((1, d, 2 * D_FF), lambda i, be, nv: (be[i], 0, 0)),
            pl.BlockSpec((1, 1, 2 * D_FF), lambda i, be, nv: (be[i], 0, 0)),
            pl.BlockSpec((1, D_FF, d), lambda i, be, nv: (be[i], 0, 0)),
            pl.BlockSpec((1, 1, d), lambda i, be, nv: (be[i], 0, 0)),
        ],
        out_specs=pl.BlockSpec((MOE_BLOCK * SUB, LANES), lambda i, be, nv: (i, 0)),
        scratch_shapes=[pltpu.VMEM((d, 2 * D_FF), BF16), pltpu.VMEM((D_FF, d), BF16),
                        pltpu.VMEM((d // LANES, D_FF, LANES), F32)],
    )
    return pl.pallas_call(
        _moe_kernel,
        grid_spec=grid_spec,
        out_shape=jax.ShapeDtypeStruct((n_blocks * MOE_BLOCK * SUB, LANES), U32),
        compiler_params=pltpu.CompilerParams(
            dimension_semantics=("arbitrary",), vmem_limit_bytes=VMEM_LIMIT),
        name="moe",
    )(block_e, n_valid, xb, wgu, bgu, wd, bd)


def _combine_kernel(g0_ref, g1_ref, g2_ref, g3_ref, h1_ref, gate_ref, p_ref, nple_ref, wpg_ref, wpp_ref, o_ref):
    g = gate_ref[...]
    moe = jnp.zeros((CMB_TILE, D_MODEL), F32)
    for kk, g_ref in enumerate((g0_ref, g1_ref, g2_ref, g3_ref)):
        moe = moe + _load_packed(g_ref, CMB_TILE) * g[:, kk:kk + 1]
    h2 = h1_ref[...] + moe
    hp = _rms(h2, nple_ref[...]).astype(BF16)
    gate = jax.nn.sigmoid(jnp.dot(hp, wpg_ref[...], preferred_element_type=F32))
    proj = jnp.dot(p_ref[...].astype(BF16), wpp_ref[...], preferred_element_type=F32)
    o_ref[...] = h2 + gate * proj


def _combine_call(gathered, h1_flat, gates, p_flat, nple, wpg, wpp):
    n_tok, d = h1_flat.shape
    tc = CMB_TILE
    n_tiles = n_tok // tc
    row = lambda i: (i, 0)
    const = lambda i: (0, 0)
    g_specs = [pl.BlockSpec((tc * SUB, LANES), functools.partial(lambda i, kk: (kk * n_tiles + i, 0), kk=kk))
               for kk in range(TOP_K)]
    return pl.pallas_call(
        _combine_kernel,
        grid=(n_tiles,),
        in_specs=g_specs + [
            pl.BlockSpec((tc, d), row),
            pl.BlockSpec((tc, LANES), row),
            pl.BlockSpec((tc, PLE_DIM), row),
            pl.BlockSpec((1, d), const),
            pl.BlockSpec((d, d), const),
            pl.BlockSpec((PLE_DIM, d), const),
        ],
        out_specs=pl.BlockSpec((tc, d), row),
        out_shape=jax.ShapeDtypeStruct((n_tok, d), F32),
        compiler_params=pltpu.CompilerParams(
            dimension_semantics=("arbitrary",), vmem_limit_bytes=VMEM_LIMIT),
        name="combine",
    )(gathered, gathered, gathered, gathered, h1_flat, gates, p_flat, nple, wpg, wpp)


def _rope_table(s):
    half = ROPE_DIM // 2
    inv_freq = jnp.power(ROPE_THETA, -jnp.arange(half, dtype=F32) * (2.0 / ROPE_DIM))
    ang = jnp.arange(s, dtype=F32)[:, None] * inv_freq[None, :]
    cos, sin = jnp.cos(ang), jnp.sin(ang)
    pad1 = jnp.ones((s, HEAD_DIM - ROPE_DIM), F32)
    pad0 = jnp.zeros((s, HEAD_DIM - half), F32)
    c = jnp.concatenate([cos, cos, pad1], axis=-1)
    s_lo = jnp.concatenate([-sin, pad0], axis=-1)
    s_hi = jnp.concatenate([jnp.zeros((s, half), F32), sin, pad0[:, half:]], axis=-1)
    two = lambda t: jnp.concatenate([t, t], axis=-1)
    return jnp.concatenate([two(c), two(s_lo), two(s_hi)], axis=-1)


def _routing(route, counts, n_tok):
    ids = jnp.transpose(route[:, 0:TOP_K, :], (1, 0, 2)).reshape(TOP_K, n_tok)
    ranks = jnp.transpose(route[:, TOP_K:2 * TOP_K, :], (1, 0, 2)).reshape(TOP_K, n_tok)
    counts = counts.astype(jnp.int32)
    padded = (counts + MOE_BLOCK - 1) // MOE_BLOCK * MOE_BLOCK
    pends = jnp.cumsum(padded)
    pstarts = pends - padded
    pos = ranks
    for e in range(N_EXPERTS):
        pos = pos + jnp.where(ids == e, pstarts[e], 0)
    n_slots = -(-(n_tok * TOP_K) // MOE_BLOCK) * MOE_BLOCK + N_EXPERTS * MOE_BLOCK
    n_blocks = n_slots // MOE_BLOCK
    first = jnp.arange(n_blocks, dtype=jnp.int32) * MOE_BLOCK
    block_e = jnp.clip(jnp.sum((first[:, None] >= pends[None, :]).astype(jnp.int32), axis=1), 0, N_EXPERTS - 1)
    own = block_e[:, None] == jnp.arange(N_EXPERTS, dtype=jnp.int32)[None, :]
    left = jnp.sum(jnp.where(own, (counts + pstarts)[None, :], 0), axis=1) - first
    n_valid = jnp.clip(left, 0, MOE_BLOCK)
    return pos, block_e, n_valid.astype(jnp.int32), n_slots


def _expand_rows(pos):
    spread = (jnp.arange(SUB * LANES, dtype=jnp.int32)[None, :] // SUB
              == jnp.arange(LANES, dtype=jnp.int32)[:, None]).astype(F32)
    wide = jnp.dot(pos.reshape(-1, LANES).astype(F32), spread, precision=lax.Precision.HIGHEST)
    wide = wide.astype(jnp.int32) * SUB + (jnp.arange(SUB * LANES, dtype=jnp.int32) % SUB)[None, :]
    return wide.reshape(-1, LANES)


def _group_forward(x, p_l, rope_tab, mix_prm, moe_prm, ple_prm):
    b, s, d = x.shape
    n_tok = b * s
    h1, hn_rows, route, gates, counts = _mixer_call(x, rope_tab, mix_prm)
    pos, block_e, n_valid, n_slots = _routing(route, counts[:, 0], n_tok)
    slot_rows = _expand_rows(pos)
    xb = _sc_dispatch(hn_rows, slot_rows, n_slots * SUB)
    y = _moe_call(block_e, n_valid, xb, *moe_prm)
    gathered = _sc_gather(y, slot_rows.reshape(-1))
    out = _combine_call(gathered, h1.reshape(n_tok, d), gates.reshape(n_tok, LANES),
                        p_l.reshape(n_tok, PLE_DIM), *ple_prm)
    return out.reshape(b, s, d)


def kernel(x_prompt, x_sample, p_prompt, p_sample, norm_mix, w_in, q_gain, k_gain, attn_sink, gmlp_v_gain, gmlp_w_s, gmlp_b_s, w_branch, w_out, norm_ffn, w_router, b_router, w_gate_up, b_gate_up, w_down, b_down, norm_ple, w_ple_gate, w_ple_proj):
    depth = norm_mix.shape[0]
    hp, hs = x_prompt, x_sample
    for l in range(depth):
        row = lambda a: a.reshape(1, -1)
        blockdiag = jnp.kron(jnp.eye(N_Q_HEADS, dtype=F32),
                             jnp.full((HEAD_DIM, HEAD_DIM), 1.0 / HEAD_DIM, F32)).astype(BF16)
        sink_rows = jnp.repeat(attn_sink[l].reshape(N_KV_HEADS, GQA_GROUP), ATTN_BLOCK, axis=1)[..., None]
        wcat = jnp.transpose(gmlp_w_s[l], (1, 0, 2)).reshape(GMLP_CHUNK, GMLP_GROUPS * GMLP_CHUNK).astype(BF16)
        bias_full = jnp.repeat(gmlp_b_s[l].T, GMLP_GROUP_DIM, axis=1)
        wr = jnp.pad(w_router[l], ((0, 0), (0, ROUTER_PAD - N_EXPERTS)))
        wr_hi = wr.astype(BF16)
        wr_lo = (wr - wr_hi.astype(F32)).astype(BF16)
        br = jnp.pad(b_router[l], (0, ROUTER_PAD - N_EXPERTS)).reshape(1, ROUTER_PAD)
        tri = (jnp.arange(MIX_TILE)[:, None] < jnp.arange(MIX_TILE)[None, :]).astype(BF16)
        mix_prm = (
            row(norm_mix[l]), w_in[l].astype(BF16),
            row(jnp.tile(q_gain[l], N_Q_HEADS)), row(jnp.tile(k_gain[l], N_KV_HEADS)),
            sink_rows, row(gmlp_v_gain[l]), blockdiag, wcat, bias_full,
            w_branch[l].astype(BF16), w_out[l].astype(BF16), row(norm_ffn[l]),
            jnp.stack([wr_hi, wr_lo]), br, tri,
        )
        moe_prm = (
            w_gate_up[l], b_gate_up[l].reshape(N_EXPERTS, 1, 2 * D_FF),
            w_down[l], b_down[l].reshape(N_EXPERTS, 1, D_MODEL),
        )
        ple_prm = (row(norm_ple[l]), w_ple_gate[l].astype(BF16), w_ple_proj[l].astype(BF16))
        rope_tab = _rope_table(max(hp.shape[1], hs.shape[1]))
        hp = _group_forward(hp, p_prompt[l], rope_tab, mix_prm, moe_prm, ple_prm)
        hs = _group_forward(hs, p_sample[l], rope_tab, mix_prm, moe_prm, ple_prm)
    return (hp.astype(x_prompt.dtype), hs.astype(x_sample.dtype))
```

```python
import functools

import jax
import jax.numpy as jnp
import numpy as np
from jax import lax
from jax.experimental import pallas as pl
from jax.experimental.pallas import tpu as pltpu
from jax.experimental.pallas import tpu_sc as plsc

D_MODEL = 1024
HEAD_DIM = 64
N_Q_HEADS = 8
N_KV_HEADS = 2
GQA_GROUP = N_Q_HEADS // N_KV_HEADS
ATTN_WIDTH = N_Q_HEADS * HEAD_DIM
KV_WIDTH = N_KV_HEADS * HEAD_DIM
WINDOW = 128
ATTN_BLOCK = 128
ROPE_THETA = 500000.0
ROPE_DIM = HEAD_DIM // 4
GMLP_WIDTH = D_MODEL // 2
GMLP_GROUPS = 8
GMLP_GROUP_DIM = GMLP_WIDTH // GMLP_GROUPS
GMLP_CHUNK = 128
N_BRANCH = 2
IN_WIDTH = ATTN_WIDTH + 2 * KV_WIDTH + 2 * GMLP_WIDTH + N_BRANCH * D_MODEL
N_EXPERTS = 32
TOP_K = 4
D_FF = D_MODEL
SWIGLU_ALPHA = 1.702
SWIGLU_LIMIT = 7.0
PLE_DIM = 256
EPS = 1e-6

Q_OFF = 0
K_OFF = ATTN_WIDTH
V_OFF = K_OFF + KV_WIDTH
U_OFF = V_OFF + KV_WIDTH
VG_OFF = U_OFF + GMLP_WIDTH
GL_OFF = VG_OFF + GMLP_WIDTH

LANES = 128
PACK_WORDS = D_MODEL // 2
SUB = PACK_WORDS // LANES
MIX_TILE = 512
MOE_BLOCK = 512
MOE_COLS = 256
CMB_TILE = 512
ROUTER_PAD = 128
SC_CHUNK = 128
DISPATCH_TOKENS = 64
VMEM_LIMIT = 56 * 1024 * 1024

BF16 = jnp.bfloat16
F32 = jnp.float32
U32 = jnp.uint32


def _rms(x, gain):
    return x * lax.rsqrt(jnp.mean(x * x, axis=-1, keepdims=True) + EPS) * gain


def _gelu(x):
    return 0.5 * x * (1.0 + lax.erf(x * np.float32(np.sqrt(0.5))))


def _head_rms(x, blockdiag, gain):
    ms = jnp.dot((x * x).astype(BF16), blockdiag, preferred_element_type=F32)
    return x * lax.rsqrt(ms + EPS) * gain


def _rope(x, cos, sin_lo, sin_hi):
    w = x.shape[-1]
    return x * cos + pltpu.roll(x, w - ROPE_DIM // 2, 1) * sin_lo + pltpu.roll(x, ROPE_DIM // 2, 1) * sin_hi


def _tile_lanes(t, reps):
    return t if reps == 1 else jnp.concatenate([t] * reps, axis=-1)


def _store_packed(ref, x):
    rows = x.shape[0]
    hi = lax.bitcast_convert_type(x[:, :PACK_WORDS].astype(BF16).astype(F32), U32)
    lo = lax.bitcast_convert_type(x[:, PACK_WORDS:].astype(BF16).astype(F32), U32)
    words = hi | (lo >> 16)
    for j in range(SUB):
        ref[pl.ds(j, rows, stride=SUB), :] = words[:, j * LANES:(j + 1) * LANES]


def _load_packed(ref, rows):
    words = jnp.concatenate([ref[pl.ds(j, rows, stride=SUB), :] for j in range(SUB)], axis=1)
    left = lax.bitcast_convert_type(words & np.uint32(0xFFFF0000), F32)
    right = lax.bitcast_convert_type(words << 16, F32)
    return jnp.concatenate([left, right], axis=1)


def _mixer_kernel(x_ref, xp_ref, xn_ref, rp_ref, rpp_ref, rpn_ref,
                  nmix_ref, win_ref, qg_ref, kg_ref, sink_ref, vgain_ref, bdq_ref,
                  wcat_ref, bias_ref, wbr_ref, wout_ref, nffn_ref, wr_ref, br_ref, tri_ref,
                  h1_ref, hn_ref, route_ref, gate_ref, cnt_out_ref,
                  z_ref, q_ref, k_ref, v_ref, attn_ref, gm_ref, cnt_ref, *, n_blocks_seq):
    ts = MIX_TILE
    i = pl.program_id(1)
    x = x_ref[0]
    xn = _rms(x, nmix_ref[...]).astype(BF16)
    z_ref[...] = jnp.dot(xn, win_ref[...], preferred_element_type=F32)

    cos = rp_ref[:, 0:LANES]
    sin_lo = rp_ref[:, LANES:2 * LANES]
    sin_hi = rp_ref[:, 2 * LANES:3 * LANES]
    bdq = bdq_ref[...]
    bdk = bdq_ref[0:KV_WIDTH, 0:KV_WIDTH]

    q = _head_rms(z_ref[:, Q_OFF:Q_OFF + ATTN_WIDTH], bdq, qg_ref[...])
    reps = ATTN_WIDTH // LANES
    q = _rope(q, _tile_lanes(cos, reps), _tile_lanes(sin_lo, reps), _tile_lanes(sin_hi, reps))
    q_ref[...] = (q * (HEAD_DIM ** -0.5)).astype(BF16)

    k = _head_rms(z_ref[:, K_OFF:K_OFF + KV_WIDTH], bdk, kg_ref[...])
    k_ref[ATTN_BLOCK:ATTN_BLOCK + ts, :] = _rope(k, cos, sin_lo, sin_hi).astype(BF16)
    v_ref[ATTN_BLOCK:ATTN_BLOCK + ts, :] = z_ref[:, V_OFF:V_OFF + KV_WIDTH].astype(BF16)

    xh = jnp.concatenate([xp_ref[0], xn_ref[0]], axis=0)
    xhn = _rms(xh, nmix_ref[...]).astype(BF16)
    zh = jnp.dot(xhn, win_ref[:, K_OFF:K_OFF + 2 * KV_WIDTH], preferred_element_type=F32)
    kh = _head_rms(zh[:, 0:KV_WIDTH], bdk, kg_ref[...])
    rph = jnp.concatenate([rpp_ref[...], rpn_ref[...]], axis=0)
    kh = _rope(kh, rph[:, 0:LANES], rph[:, LANES:2 * LANES], rph[:, 2 * LANES:3 * LANES]).astype(BF16)
    vh = zh[:, KV_WIDTH:2 * KV_WIDTH].astype(BF16)
    k_ref[0:ATTN_BLOCK, :] = kh[0:ATTN_BLOCK]
    k_ref[ATTN_BLOCK + ts:2 * ATTN_BLOCK + ts, :] = kh[ATTN_BLOCK:]
    v_ref[0:ATTN_BLOCK, :] = vh[0:ATTN_BLOCK]
    v_ref[ATTN_BLOCK + ts:2 * ATTN_BLOCK + ts, :] = vh[ATTN_BLOCK:]

    rows = GQA_GROUP * ATTN_BLOCK
    keys = 3 * ATTN_BLOCK
    r = lax.broadcasted_iota(jnp.int32, (rows, keys), 0) % ATTN_BLOCK
    c = lax.broadcasted_iota(jnp.int32, (rows, keys), 1)
    band = (c >= r) & (c <= r + 2 * WINDOW)
    for qb in range(ts // ATTN_BLOCK):
        gb = i * (ts // ATTN_BLOCK) + qb
        lo = jnp.where(gb == 0, ATTN_BLOCK, 0)
        hi = jnp.where(gb == n_blocks_seq - 1, 2 * ATTN_BLOCK, keys)
        valid = band & (c >= lo) & (c < hi)
        r0 = qb * ATTN_BLOCK
        for j in range(N_KV_HEADS):
            q4 = jnp.concatenate(
                [q_ref[r0:r0 + ATTN_BLOCK, (GQA_GROUP * j + g) * HEAD_DIM:(GQA_GROUP * j + g + 1) * HEAD_DIM]
                 for g in range(GQA_GROUP)], axis=0)
            kw = k_ref[r0:r0 + keys, j * HEAD_DIM:(j + 1) * HEAD_DIM]
            vw = v_ref[r0:r0 + keys, j * HEAD_DIM:(j + 1) * HEAD_DIM]
            s = lax.dot_general(q4, kw, (((1,), (1,)), ((), ())), preferred_element_type=F32)
            s = jnp.where(valid, s, -jnp.inf)
            sk = sink_ref[j]
            m = jnp.maximum(jnp.max(s, axis=-1, keepdims=True), sk)
            p = jnp.exp(s - m)
            denom = jnp.sum(p, axis=-1, keepdims=True) + jnp.exp(sk - m)
            o = jnp.dot(p.astype(BF16), vw, preferred_element_type=F32) / denom
            for g in range(GQA_GROUP):
                h = GQA_GROUP * j + g
                attn_ref[r0:r0 + ATTN_BLOCK, h * HEAD_DIM:(h + 1) * HEAD_DIM] = (
                    o[g * ATTN_BLOCK:(g + 1) * ATTN_BLOCK].astype(BF16))

    half = GMLP_WIDTH // 2
    gpm = half // GMLP_GROUP_DIM
    lane_grp = lax.broadcasted_iota(jnp.int32, (GMLP_CHUNK, half), 1) // GMLP_GROUP_DIM
    for ch in range(ts // GMLP_CHUNK):
        c0 = ch * GMLP_CHUNK
        vg = _gelu(z_ref[c0:c0 + GMLP_CHUNK, VG_OFF:VG_OFF + GMLP_WIDTH])
        vgn = _head_rms(vg, bdq, vgain_ref[...])
        mixed = []
        for nt in range(2):
            part = vgn[:, nt * half:(nt + 1) * half]
            vexp = jnp.concatenate(
                [jnp.where(lane_grp == gl, part, 0.0).astype(BF16) for gl in range(gpm)], axis=0)
            wpart = wcat_ref[:, nt * gpm * GMLP_CHUNK:(nt + 1) * gpm * GMLP_CHUNK]
            mixed.append(jnp.dot(wpart, vexp, preferred_element_type=F32))
        mixed = jnp.concatenate(mixed, axis=-1) + bias_ref[...]
        u = _gelu(z_ref[c0:c0 + GMLP_CHUNK, U_OFF:U_OFF + GMLP_WIDTH])
        gm_ref[c0:c0 + GMLP_CHUNK, :] = (u * mixed).astype(BF16)

    ya = jnp.dot(attn_ref[...], wbr_ref[0], preferred_element_type=F32)
    yg = jnp.dot(gm_ref[...], wbr_ref[1], preferred_element_type=F32)
    ga = jax.nn.sigmoid(z_ref[:, GL_OFF:GL_OFF + D_MODEL])
    gg = jax.nn.sigmoid(z_ref[:, GL_OFF + D_MODEL:GL_OFF + 2 * D_MODEL])
    merged = (ga * ya + gg * yg).astype(BF16)
    h1 = x + jnp.dot(merged, wout_ref[...], preferred_element_type=F32)
    h1_ref[0] = h1

    hn = _rms(h1, nffn_ref[...])
    _store_packed(hn_ref, hn)

    hn_hi = hn.astype(BF16)
    hn_lo = (hn - hn_hi.astype(F32)).astype(BF16)
    w_hi = wr_ref[0]
    w_lo = wr_ref[1]
    logits = (jnp.dot(hn_hi, w_hi, preferred_element_type=F32)
              + jnp.dot(hn_hi, w_lo, preferred_element_type=F32)
              + jnp.dot(hn_lo, w_hi, preferred_element_type=F32)) + br_ref[...]
    lt = jnp.transpose(logits)[0:N_EXPERTS, :]
    eid = lax.broadcasted_iota(jnp.int32, lt.shape, 0)
    vals, ids = [], []
    for _ in range(TOP_K):
        mx = jnp.max(lt, axis=0, keepdims=True)
        am = jnp.min(jnp.where(lt == mx, eid, N_EXPERTS), axis=0, keepdims=True)
        vals.append(mx)
        ids.append(am)
        lt = jnp.where(eid == am, -jnp.inf, lt)
    ex = [jnp.exp(v - vals[0]) for v in vals]
    tot = ex[0] + ex[1] + ex[2] + ex[3]
    g8 = jnp.concatenate([e / tot for e in ex] + [jnp.zeros_like(tot)] * (8 - TOP_K), axis=0)
    gates = jnp.concatenate([g8, jnp.zeros((LANES - 8, ts), F32)], axis=0)
    gate_ref[0] = jnp.transpose(gates)

    @pl.when((pl.program_id(0) == 0) & (i == 0))
    def _():
        cnt_ref[...] = jnp.zeros_like(cnt_ref)

    sel = [eid == a for a in ids]
    member = (sel[0] | sel[1] | sel[2] | sel[3]).astype(F32)
    before = jnp.dot(member.astype(BF16), tri_ref[...], preferred_element_type=F32) + cnt_ref[:, 0:1]
    ranks = [jnp.sum(jnp.where(s_, before, 0.0), axis=0, keepdims=True).astype(jnp.int32) for s_ in sel]
    route_ref[0] = jnp.concatenate(ids + ranks, axis=0)
    cnt_ref[...] = cnt_ref[...] + jnp.sum(member, axis=1, keepdims=True)
    cnt_out_ref[...] = cnt_ref[...]


def _mixer_call(x, rope_tab, prm):
    b, s, d = x.shape
    ts = MIX_TILE
    nt = s // ts
    nb = s // ATTN_BLOCK
    per = ts // ATTN_BLOCK
    const2 = lambda bi, i: (0, 0)
    const3 = lambda bi, i: (0, 0, 0)

    def wspec(arr):
        return pl.BlockSpec(arr.shape, const2 if arr.ndim == 2 else const3)

    in_specs = [
        pl.BlockSpec((1, ts, d), lambda bi, i: (bi, i, 0)),
        pl.BlockSpec((1, ATTN_BLOCK, d), lambda bi, i: (bi, jnp.maximum(i * per - 1, 0), 0)),
        pl.BlockSpec((1, ATTN_BLOCK, d), lambda bi, i: (bi, jnp.minimum((i + 1) * per, nb - 1), 0)),
        pl.BlockSpec((ts, 3 * LANES), lambda bi, i: (i, 0)),
        pl.BlockSpec((ATTN_BLOCK, 3 * LANES), lambda bi, i: (jnp.maximum(i * per - 1, 0), 0)),
        pl.BlockSpec((ATTN_BLOCK, 3 * LANES), lambda bi, i: (jnp.minimum((i + 1) * per, nb - 1), 0)),
    ] + [wspec(a) for a in prm]
    out_shape = [
        jax.ShapeDtypeStruct((b, s, d), F32),
        jax.ShapeDtypeStruct((b * s * SUB, LANES), U32),
        jax.ShapeDtypeStruct((b, 2 * TOP_K, s), jnp.int32),
        jax.ShapeDtypeStruct((b, s, LANES), F32),
        jax.ShapeDtypeStruct((N_EXPERTS, LANES), F32),
    ]
    out_specs = [
        pl.BlockSpec((1, ts, d), lambda bi, i: (bi, i, 0)),
        pl.BlockSpec((ts * SUB, LANES), lambda bi, i: (bi * nt + i, 0)),
        pl.BlockSpec((1, 2 * TOP_K, ts), lambda bi, i: (bi, 0, i)),
        pl.BlockSpec((1, ts, LANES), lambda bi, i: (bi, i, 0)),
        pl.BlockSpec((N_EXPERTS, LANES), const2),
    ]
    scratch = [
        pltpu.VMEM((ts, IN_WIDTH), F32),
        pltpu.VMEM((ts, ATTN_WIDTH), BF16),
        pltpu.VMEM((ts + 2 * ATTN_BLOCK, KV_WIDTH), BF16),
        pltpu.VMEM((ts + 2 * ATTN_BLOCK, KV_WIDTH), BF16),
        pltpu.VMEM((ts, ATTN_WIDTH), BF16),
        pltpu.VMEM((ts, GMLP_WIDTH), BF16),
        pltpu.VMEM((N_EXPERTS, LANES), F32),
    ]
    return pl.pallas_call(
        functools.partial(_mixer_kernel, n_blocks_seq=nb),
        grid=(b, nt),
        in_specs=in_specs,
        out_specs=out_specs,
        out_shape=out_shape,
        scratch_shapes=scratch,
        compiler_params=pltpu.CompilerParams(
            dimension_semantics=("arbitrary", "arbitrary"), vmem_limit_bytes=VMEM_LIMIT),
        name="mixer",
    )(x, x, x, rope_tab, rope_tab, rope_tab, *prm)


def _sc_workers():
    info = plsc.get_sparse_core_info()
    return info.num_cores, info.num_cores * info.num_subcores


def _sc_dispatch(rows2d, slot_rows, n_out_rows):
    n_cores, n_workers = _sc_workers()
    chunk_rows = DISPATCH_TOKENS * SUB
    halves = chunk_rows // SC_CHUNK
    n_chunks = rows2d.shape[0] // chunk_rows
    per_w = n_chunks // n_workers
    idx_rows_per_choice = rows2d.shape[0] // SC_CHUNK
    mesh = plsc.VectorSubcoreMesh(core_axis_name="c", subcore_axis_name="s")

    @functools.partial(
        pl.kernel, mesh=mesh,
        out_type=jax.ShapeDtypeStruct((n_out_rows, LANES), rows2d.dtype),
        scratch_types=[pltpu.VMEM((TOP_K * halves, SC_CHUNK), jnp.int32),
                       pltpu.VMEM((chunk_rows, LANES), rows2d.dtype),
                       pltpu.SemaphoreType.DMA, pltpu.SemaphoreType.DMA],
    )
    def k(src_hbm, idx_hbm, out_hbm, idx_v, rows_v, sem_in, sem_out):
        wid = lax.axis_index("s") * n_cores + lax.axis_index("c")

        @pl.loop(0, per_w)
        def _(j):
            c = wid * per_w + j
            loads = [pltpu.async_copy(src_hbm.at[pl.ds(c * chunk_rows, chunk_rows)], rows_v, sem_in)]
            loads += [
                pltpu.async_copy(idx_hbm.at[pl.ds(kk * idx_rows_per_choice + c * halves, halves)],
                                 idx_v.at[pl.ds(kk * halves, halves)], sem_in)
                for kk in range(TOP_K)]
            for cp in loads:
                cp.wait()
            copies = [
                pltpu.async_copy(rows_v.at[pl.ds((q % halves) * SC_CHUNK, SC_CHUNK)],
                                 out_hbm.at[idx_v.at[q]], sem_out)
                for q in range(TOP_K * halves)]
            for cp in copies:
                cp.wait()

    return k(rows2d, slot_rows)


def _sc_gather(table2d, idx):
    n_cores, n_workers = _sc_workers()
    n = idx.shape[0]
    per_w = n // n_workers
    mesh = plsc.VectorSubcoreMesh(core_axis_name="c", subcore_axis_name="s")

    @functools.partial(
        pl.kernel, mesh=mesh,
        out_type=jax.ShapeDtypeStruct((n, LANES), table2d.dtype),
        scratch_types=[pltpu.VMEM((SC_CHUNK,), jnp.int32),
                       pltpu.VMEM((SC_CHUNK, LANES), table2d.dtype),
                       pltpu.SemaphoreType.DMA],
    )
    def k(table_hbm, idx_hbm, out_hbm, idx_v, rows_v, sem):
        wid = lax.axis_index("s") * n_cores + lax.axis_index("c")

        @pl.loop(0, per_w // SC_CHUNK)
        def _(j):
            base = wid * per_w + j * SC_CHUNK
            pltpu.sync_copy(idx_hbm.at[pl.ds(base, SC_CHUNK)], idx_v)
            pltpu.async_copy(table_hbm.at[idx_v], rows_v, sem).wait()
            pltpu.sync_copy(rows_v, out_hbm.at[pl.ds(base, SC_CHUNK)])

    return k(table2d, idx)


def _moe_kernel(be_ref, nv_ref, x_ref, wgu_ref, bgu_ref, wd_ref, bd_ref, y_ref, wgu_s, wd_s, slab):
    i = pl.program_id(0)
    nv = nv_ref[i]

    cw = MOE_COLS
    n_chunks = D_FF // cw

    @pl.when((i == 0) | (be_ref[i] != be_ref[jnp.maximum(i - 1, 0)]))
    def _():
        for c in range(n_chunks):
            wgu_s[:, 2 * c * cw:(2 * c + 1) * cw] = wgu_ref[0, :, c * cw:(c + 1) * cw].astype(BF16)
            wgu_s[:, (2 * c + 1) * cw:(2 * c + 2) * cw] = wgu_ref[0, :, D_FF + c * cw:D_FF + (c + 1) * cw].astype(BF16)
        half = D_FF // 2
        for c in range(D_MODEL // LANES):
            slab[c, pl.ds(0, half, stride=2), :] = wd_ref[0, 0:half, c * LANES:(c + 1) * LANES]
            slab[c, pl.ds(1, half, stride=2), :] = wd_ref[0, half:D_FF, c * LANES:(c + 1) * LANES]
        for c in range(D_MODEL // LANES):
            wd_s[:, c * LANES:(c + 1) * LANES] = slab[c].astype(BF16)

    @pl.when(nv > 0)
    def _():
        x = _load_packed(x_ref, MOE_BLOCK)
        live = lax.broadcasted_iota(jnp.int32, x.shape, 0) < nv
        xe = jnp.where(live, x, 0.0).astype(BF16)
        even = (lax.broadcasted_iota(jnp.int32, (MOE_BLOCK, cw), 1) % 2) == 0
        y = bd_ref[0]

        def gate_up(c):
            return jnp.dot(xe, wgu_s[:, 2 * c * cw:(2 * c + 2) * cw], preferred_element_type=F32)

        h_next = gate_up(0)
        for c in range(n_chunks):
            h = h_next
            if c + 1 < n_chunks:
                h_next = gate_up(c + 1)
            h_a = h[:, 0:cw] + bgu_ref[0, :, c * cw:(c + 1) * cw]
            h_b = h[:, cw:2 * cw] + bgu_ref[0, :, D_FF + c * cw:D_FF + (c + 1) * cw]
            gate = jnp.where(even, h_a, pltpu.roll(h_b, 1, 1))
            up = jnp.where(even, pltpu.roll(h_a, cw - 1, 1), h_b)
            gate = jnp.minimum(gate, SWIGLU_LIMIT)
            up = jnp.clip(up, -SWIGLU_LIMIT, SWIGLU_LIMIT)
            act = (up + 1.0) * (gate * jax.nn.sigmoid(SWIGLU_ALPHA * gate))
            y = y + jnp.dot(act.astype(BF16), wd_s[c * cw:(c + 1) * cw, :], preferred_element_type=F32)
        _store_packed(y_ref, y)


def _moe_call(block_e, n_valid, xb, wgu, bgu, wd, bd):
    n_blocks = block_e.shape[0]
    d = D_MODEL
    grid_spec = pltpu.PrefetchScalarGridSpec(
        num_scalar_prefetch=2,
        grid=(n_blocks,),
        in_specs=[
            pl.BlockSpec((MOE_BLOCK * SUB, LANES), lambda i, be, nv: (i, 0)),
            pl.BlockSpec((1, d, 2 * D_FF), lambda i, be, nv: (be[i], 0, 0)),
            pl.BlockSpec((1, 1, 2 * D_FF), lambda i, be, nv: (be[i], 0, 0)),
            pl.BlockSpec((1, D_FF, d), lambda i, be, nv: (be[i], 0, 0)),
            pl.BlockSpec((1, 1, d), lambda i, be, nv: (be[i], 0, 0)),
        ],
        out_specs=pl.BlockSpec((MOE_BLOCK * SUB, LANES), lambda i, be, nv: (i, 0)),
        scratch_shapes=[pltpu.VMEM((d, 2 * D_FF), BF16), pltpu.VMEM((D_FF, d), BF16),
                        pltpu.VMEM((d // LANES, D_FF, LANES), F32)],
    )
    return pl.pallas_call(
        _moe_kernel,
        grid_spec=grid_spec,
        out_shape=jax.ShapeDtypeStruct((n_blocks * MOE_BLOCK * SUB, LANES), U32),
        compiler_params=pltpu.CompilerParams(
            dimension_semantics=("arbitrary",), vmem_limit_bytes=VMEM_LIMIT),
        name="moe",
    )(block_e, n_valid, xb, wgu, bgu, wd, bd)


def _combine_kernel(g0_ref, g1_ref, g2_ref, g3_ref, h1_ref, gate_ref, p_ref, nple_ref, wpg_ref, wpp_ref, o_ref):
    g = gate_ref[...]
    moe = jnp.zeros((CMB_TILE, D_MODEL), F32)
    for kk, g_ref in enumerate((g0_ref, g1_ref, g2_ref, g3_ref)):
        moe = moe + _load_packed(g_ref, CMB_TILE) * g[:, kk:kk + 1]
    h2 = h1_ref[...] + moe
    hp = _rms(h2, nple_ref[...]).astype(BF16)
    gate = jax.nn.sigmoid(jnp.dot(hp, wpg_ref[...], preferred_element_type=F32))
    proj = jnp.dot(p_ref[...].astype(BF16), wpp_ref[...], preferred_element_type=F32)
    o_ref[...] = h2 + gate * proj


def _combine_call(gathered, h1_flat, gates, p_flat, nple, wpg, wpp):
    n_tok, d = h1_flat.shape
    tc = CMB_TILE
    n_tiles = n_tok // tc
    row = lambda i: (i, 0)
    const = lambda i: (0, 0)
    g_specs = [pl.BlockSpec((tc * SUB, LANES), functools.partial(lambda i, kk: (kk * n_tiles + i, 0), kk=kk))
               for kk in range(TOP_K)]
    return pl.pallas_call(
        _combine_kernel,
        grid=(n_tiles,),
        in_specs=g_specs + [
            pl.BlockSpec((tc, d), row),
            pl.BlockSpec((tc, LANES), row),
            pl.BlockSpec((tc, PLE_DIM), row),
            pl.BlockSpec((1, d), const),
            pl.BlockSpec((d, d), const),
            pl.BlockSpec((PLE_DIM, d), const),
        ],
        out_specs=pl.BlockSpec((tc, d), row),
        out_shape=jax.ShapeDtypeStruct((n_tok, d), F32),
        compiler_params=pltpu.CompilerParams(
            dimension_semantics=("arbitrary",), vmem_limit_bytes=VMEM_LIMIT),
        name="combine",
    )(gathered, gathered, gathered, gathered, h1_flat, gates, p_flat, nple, wpg, wpp)


def _rope_table(s):
    half = ROPE_DIM // 2
    inv_freq = jnp.power(ROPE_THETA, -jnp.arange(half, dtype=F32) * (2.0 / ROPE_DIM))
    ang = jnp.arange(s, dtype=F32)[:, None] * inv_freq[None, :]
    cos, sin = jnp.cos(ang), jnp.sin(ang)
    pad1 = jnp.ones((s, HEAD_DIM - ROPE_DIM), F32)
    pad0 = jnp.zeros((s, HEAD_DIM - half), F32)
    c = jnp.concatenate([cos, cos, pad1], axis=-1)
    s_lo = jnp.concatenate([-sin, pad0], axis=-1)
    s_hi = jnp.concatenate([jnp.zeros((s, half), F32), sin, pad0[:, half:]], axis=-1)
    two = lambda t: jnp.concatenate([t, t], axis=-1)
    return jnp.concatenate([two(c), two(s_lo), two(s_hi)], axis=-1)


def _routing(route, counts, n_tok):
    ids = jnp.transpose(route[:, 0:TOP_K, :], (1, 0, 2)).reshape(TOP_K, n_tok)
    ranks = jnp.transpose(route[:, TOP_K:2 * TOP_K, :], (1, 0, 2)).reshape(TOP_K, n_tok)
    counts = counts.astype(jnp.int32)
    padded = (counts + MOE_BLOCK - 1) // MOE_BLOCK * MOE_BLOCK
    pends = jnp.cumsum(padded)
    pstarts = pends - padded
    pos = ranks
    for e in range(N_EXPERTS):
        pos = pos + jnp.where(ids == e, pstarts[e], 0)
    n_slots = -(-(n_tok * TOP_K) // MOE_BLOCK) * MOE_BLOCK + N_EXPERTS * MOE_BLOCK
    n_blocks = n_slots // MOE_BLOCK
    first = jnp.arange(n_blocks, dtype=jnp.int32) * MOE_BLOCK
    block_e = jnp.clip(jnp.sum((first[:, None] >= pends[None, :]).astype(jnp.int32), axis=1), 0, N_EXPERTS - 1)
    own = block_e[:, None] == jnp.arange(N_EXPERTS, dtype=jnp.int32)[None, :]
    left = jnp.sum(jnp.where(own, (counts + pstarts)[None, :], 0), axis=1) - first
    n_valid = jnp.clip(left, 0, MOE_BLOCK)
    return pos, block_e, n_valid.astype(jnp.int32), n_slots


def _expand_rows(pos):
    spread = (jnp.arange(SUB * LANES, dtype=jnp.int32)[None, :] // SUB
              == jnp.arange(LANES, dtype=jnp.int32)[:, None]).astype(F32)
    wide = jnp.dot(pos.reshape(-1, LANES).astype(F32), spread, precision=lax.Precision.HIGHEST)
    wide = wide.astype(jnp.int32) * SUB + (jnp.arange(SUB * LANES, dtype=jnp.int32) % SUB)[None, :]
    return wide.reshape(-1, LANES)


def _group_forward(x, p_l, rope_tab, mix_prm, moe_prm, ple_prm):
    b, s, d = x.shape
    n_tok = b * s
    h1, hn_rows, route, gates, counts = _mixer_call(x, rope_tab, mix_prm)
    pos, block_e, n_valid, n_slots = _routing(route, counts[:, 0], n_tok)
    slot_rows = _expand_rows(pos)
    xb = _sc_dispatch(hn_rows, slot_rows, n_slots * SUB)
    y = _moe_call(block_e, n_valid, xb, *moe_prm)
    gathered = _sc_gather(y, slot_rows.reshape(-1))
    out = _combine_call(gathered, h1.reshape(n_tok, d), gates.reshape(n_tok, LANES),
                        p_l.reshape(n_tok, PLE_DIM), *ple_prm)
    return out.reshape(b, s, d)


def kernel(x_prompt, x_sample, p_prompt, p_sample, norm_mix, w_in, q_gain, k_gain, attn_sink, gmlp_v_gain, gmlp_w_s, gmlp_b_s, w_branch, w_out, norm_ffn, w_router, b_router, w_gate_up, b_gate_up, w_down, b_down, norm_ple, w_ple_gate, w_ple_proj):
    depth = norm_mix.shape[0]
    hp, hs = x_prompt, x_sample
    for l in range(depth):
        row = lambda a: a.reshape(1, -1)
        blockdiag = jnp.kron(jnp.eye(N_Q_HEADS, dtype=F32),
                             jnp.full((HEAD_DIM, HEAD_DIM), 1.0 / HEAD_DIM, F32)).astype(BF16)
        sink_rows = jnp.repeat(attn_sink[l].reshape(N_KV_HEADS, GQA_GROUP), ATTN_BLOCK, axis=1)[..., None]
        wcat = jnp.transpose(gmlp_w_s[l], (1, 0, 2)).reshape(GMLP_CHUNK, GMLP_GROUPS * GMLP_CHUNK).astype(BF16)
        bias_full = jnp.repeat(gmlp_b_s[l].T, GMLP_GROUP_DIM, axis=1)
        wr = jnp.pad(w_router[l], ((0, 0), (0, ROUTER_PAD - N_EXPERTS)))
        wr_hi = wr.astype(BF16)
        wr_lo = (wr - wr_hi.astype(F32)).astype(BF16)
        br = jnp.pad(b_router[l], (0, ROUTER_PAD - N_EXPERTS)).reshape(1, ROUTER_PAD)
        tri = (jnp.arange(MIX_TILE)[:, None] < jnp.arange(MIX_TILE)[None, :]).astype(BF16)
        mix_prm = (
            row(norm_mix[l]), w_in[l].astype(BF16),
            row(jnp.tile(q_gain[l], N_Q_HEADS)), row(jnp.tile(k_gain[l], N_KV_HEADS)),
            sink_rows, row(gmlp_v_gain[l]), blockdiag, wcat, bias_full,
            w_branch[l].astype(BF16), w_out[l].astype(BF16), row(norm_ffn[l]),
            jnp.stack([wr_hi, wr_lo]), br, tri,
        )
        moe_prm = (
            w_gate_up[l], b_gate_up[l].reshape(N_EXPERTS, 1, 2 * D_FF),
            w_down[l], b_down[l].reshape(N_EXPERTS, 1, D_MODEL),
        )
        ple_prm = (row(norm_ple[l]), w_ple_gate[l].astype(BF16), w_ple_proj[l].astype(BF16))
        rope_tab = _rope_table(max(hp.shape[1], hs.shape[1]))
        hp = _group_forward(hp, p_prompt[l], rope_tab, mix_prm, moe_prm, ple_prm)
        hs = _group_forward(hs, p_sample[l], rope_tab, mix_prm, moe_prm, ple_prm)
    return (hp.astype(x_prompt.dtype), hs.astype(x_sample.dtype))
```

```python
import functools

import jax
import jax.numpy as jnp
import numpy as np
from jax import lax
from jax.experimental import pallas as pl
from jax.experimental.pallas import tpu as pltpu
from jax.experimental.pallas import tpu_sc as plsc

D_MODEL = 1024
HEAD_DIM = 64
N_Q_HEADS = 8
N_KV_HEADS = 2
GQA_GROUP = N_Q_HEADS // N_KV_HEADS
HEAD_ORDER = (0, 2, 1, 3)
ATTN_WIDTH = N_Q_HEADS * HEAD_DIM
KV_WIDTH = N_KV_HEADS * HEAD_DIM
WINDOW = 128
ATTN_BLOCK = 128
ROPE_THETA = 500000.0
ROPE_DIM = HEAD_DIM // 4
GMLP_WIDTH = D_MODEL // 2
GMLP_GROUPS = 8
GMLP_GROUP_DIM = GMLP_WIDTH // GMLP_GROUPS
GMLP_CHUNK = 128
N_BRANCH = 2
IN_WIDTH = ATTN_WIDTH + 2 * KV_WIDTH + 2 * GMLP_WIDTH + N_BRANCH * D_MODEL
N_EXPERTS = 32
TOP_K = 4
D_FF = D_MODEL
SWIGLU_ALPHA = 1.702
SWIGLU_LIMIT = 7.0
PLE_DIM = 256
EPS = 1e-6

Q_OFF = 0
K_OFF = ATTN_WIDTH
V_OFF = K_OFF + KV_WIDTH
U_OFF = V_OFF + KV_WIDTH
VG_OFF = U_OFF + GMLP_WIDTH
GL_OFF = VG_OFF + GMLP_WIDTH

LANES = 128
PACK_WORDS = D_MODEL // 2
SUB = PACK_WORDS // LANES
MIX_TILE = 512
MOE_BLOCK = 512
MOE_COLS = 256
CMB_TILE = 512
ROUTER_PAD = 128
SC_CHUNK = 128
DISPATCH_TOKENS = 64
GATHER_STREAMS = 4
VMEM_LIMIT = 56 * 1024 * 1024

BF16 = jnp.bfloat16
F32 = jnp.float32
U32 = jnp.uint32


def _rms(x, gain):
    return x * lax.rsqrt(jnp.mean(x * x, axis=-1, keepdims=True) + EPS) * gain


def _gelu(x):
    return 0.5 * x * (1.0 + lax.erf(x * np.float32(np.sqrt(0.5))))


def _head_rms(x, blockdiag, gain):
    ms = jnp.dot((x * x).astype(BF16), blockdiag, preferred_element_type=F32)
    return x * lax.rsqrt(ms + EPS) * gain


def _rope(x, cos, sin_lo, sin_hi):
    w = x.shape[-1]
    return x * cos + pltpu.roll(x, w - ROPE_DIM // 2, 1) * sin_lo + pltpu.roll(x, ROPE_DIM // 2, 1) * sin_hi


def _value_blocks(v):
    swapped = pltpu.roll(v, HEAD_DIM, 1)
    low = lax.broadcasted_iota(jnp.int32, v.shape, 1) < HEAD_DIM
    blocks = [jnp.where(low, v, 1.0), jnp.where(low, 1.0, swapped), jnp.where(low, swapped, 1.0), jnp.where(low, 1.0, v)]
    return jnp.concatenate(blocks, axis=1).astype(BF16)


def _tile_lanes(t, reps):
    return t if reps == 1 else jnp.concatenate([t] * reps, axis=-1)


def _store_packed(ref, x):
    rows = x.shape[0]
    hi = lax.bitcast_convert_type(x[:, :PACK_WORDS].astype(BF16).astype(F32), U32)
    lo = lax.bitcast_convert_type(x[:, PACK_WORDS:].astype(BF16).astype(F32), U32)
    words = hi | (lo >> 16)
    for j in range(SUB):
        ref[pl.ds(j, rows, stride=SUB), :] = words[:, j * LANES:(j + 1) * LANES]


def _load_packed(ref, rows):
    words = jnp.concatenate([ref[pl.ds(j, rows, stride=SUB), :] for j in range(SUB)], axis=1)
    left = lax.bitcast_convert_type(words & np.uint32(0xFFFF0000), F32)
    right = lax.bitcast_convert_type(words << 16, F32)
    return jnp.concatenate([left, right], axis=1)


def _mixer_kernel(x_ref, xp_ref, xn_ref, rp_ref, rpp_ref, rpn_ref,
                  nmix_ref, win_ref, qg_ref, kg_ref, sink_ref, sinkv_ref, vgain_ref, bdq_ref,
                  wcat_ref, bias_ref, wbr_ref, wout_ref, nffn_ref, wr_ref, br_ref, tri_ref,
                  h1_ref, hn_ref, route_ref, gate_ref, cnt_out_ref,
                  z_ref, q_ref, k_ref, v_ref, attn_ref, gm_ref, cnt_ref, *, n_blocks_seq):
    ts = MIX_TILE
    i = pl.program_id(1)
    x = x_ref[0]
    xn = _rms(x, nmix_ref[...]).astype(BF16)
    z_ref[...] = jnp.dot(xn, win_ref[...], preferred_element_type=F32)

    cos = rp_ref[:, 0:LANES]
    sin_lo = rp_ref[:, LANES:2 * LANES]
    sin_hi = rp_ref[:, 2 * LANES:3 * LANES]
    bdq = bdq_ref[...]
    bdk = bdq_ref[0:KV_WIDTH, 0:KV_WIDTH]

    q = _head_rms(z_ref[:, Q_OFF:Q_OFF + ATTN_WIDTH], bdq, qg_ref[...])
    reps = ATTN_WIDTH // LANES
    q = _rope(q, _tile_lanes(cos, reps), _tile_lanes(sin_lo, reps), _tile_lanes(sin_hi, reps))
    q_ref[...] = (q * (HEAD_DIM ** -0.5)).astype(BF16)

    k = _head_rms(z_ref[:, K_OFF:K_OFF + KV_WIDTH], bdk, kg_ref[...])
    k_ref[ATTN_BLOCK:ATTN_BLOCK + ts, :] = _rope(k, cos, sin_lo, sin_hi).astype(BF16)
    v_ref[ATTN_BLOCK:ATTN_BLOCK + ts, :] = _value_blocks(z_ref[:, V_OFF:V_OFF + KV_WIDTH])

    xh = jnp.concatenate([xp_ref[0], xn_ref[0]], axis=0)
    xhn = _rms(xh, nmix_ref[...]).astype(BF16)
    zh = jnp.dot(xhn, win_ref[:, K_OFF:K_OFF + 2 * KV_WIDTH], preferred_element_type=F32)
    kh = _head_rms(zh[:, 0:KV_WIDTH], bdk, kg_ref[...])
    rph = jnp.concatenate([rpp_ref[...], rpn_ref[...]], axis=0)
    kh = _rope(kh, rph[:, 0:LANES], rph[:, LANES:2 * LANES], rph[:, 2 * LANES:3 * LANES]).astype(BF16)
    vh = _value_blocks(zh[:, KV_WIDTH:2 * KV_WIDTH])
    k_ref[0:ATTN_BLOCK, :] = kh[0:ATTN_BLOCK]
    k_ref[ATTN_BLOCK + ts:2 * ATTN_BLOCK + ts, :] = kh[ATTN_BLOCK:]
    v_ref[0:ATTN_BLOCK, :] = vh[0:ATTN_BLOCK]
    v_ref[ATTN_BLOCK + ts:2 * ATTN_BLOCK + ts, :] = vh[ATTN_BLOCK:]

    rows = GQA_GROUP * ATTN_BLOCK
    keys = 3 * ATTN_BLOCK
    r = lax.broadcasted_iota(jnp.int32, (rows, keys), 0) % ATTN_BLOCK
    c = lax.broadcasted_iota(jnp.int32, (rows, keys), 1)
    band = (c >= r) & (c <= r + 2 * WINDOW)
    low = lax.broadcasted_iota(jnp.int32, (ATTN_BLOCK, 2 * HEAD_DIM), 1) < HEAD_DIM
    pair = 2 * ATTN_BLOCK
    for qb in range(ts // ATTN_BLOCK):
        gb = i * (ts // ATTN_BLOCK) + qb
        lo = jnp.where(gb == 0, ATTN_BLOCK, 0)
        hi = jnp.where(gb == n_blocks_seq - 1, 2 * ATTN_BLOCK, keys)
        valid = band & (c >= lo) & (c < hi)
        r0 = qb * ATTN_BLOCK
        for j in range(N_KV_HEADS):
            q4 = jnp.concatenate(
                [q_ref[r0:r0 + ATTN_BLOCK, (GQA_GROUP * j + g) * HEAD_DIM:(GQA_GROUP * j + g + 1) * HEAD_DIM]
                 for g in HEAD_ORDER], axis=0)
            kw = k_ref[r0:r0 + keys, j * HEAD_DIM:(j + 1) * HEAD_DIM]
            s = lax.dot_general(q4, kw, (((1,), (1,)), ((), ())), preferred_element_type=F32)
            s = jnp.concatenate([jnp.where(valid, s, -jnp.inf), sink_ref[j]], axis=1)
            p = jnp.exp(s - jnp.max(s, axis=-1, keepdims=True)).astype(BF16)
            outs = []
            for par in range(2):
                vw = jnp.concatenate([v_ref[r0:r0 + keys, (2 * j + par) * LANES:(2 * j + par + 1) * LANES],
                                      sinkv_ref[par]], axis=0)
                o = jnp.dot(p[par * pair:(par + 1) * pair], vw, preferred_element_type=F32)
                outs.append(o / pltpu.roll(o, HEAD_DIM, 1))
            for a in range(GQA_GROUP // 2):
                both = jnp.where(low, outs[0][a * ATTN_BLOCK:(a + 1) * ATTN_BLOCK],
                                 outs[1][a * ATTN_BLOCK:(a + 1) * ATTN_BLOCK])
                h0 = GQA_GROUP * j + 2 * a
                attn_ref[r0:r0 + ATTN_BLOCK, h0 * HEAD_DIM:(h0 + 2) * HEAD_DIM] = both.astype(BF16)

    half = GMLP_WIDTH // 2
    gpm = half // GMLP_GROUP_DIM
    lane_grp = lax.broadcasted_iota(jnp.int32, (GMLP_CHUNK, half), 1) // GMLP_GROUP_DIM
    for ch in range(ts // GMLP_CHUNK):
        c0 = ch * GMLP_CHUNK
        vg = _gelu(z_ref[c0:c0 + GMLP_CHUNK, VG_OFF:VG_OFF + GMLP_WIDTH])
        vgn = _head_rms(vg, bdq, vgain_ref[...])
        mixed = []
        for nt in range(2):
            part = vgn[:, nt * half:(nt + 1) * half]
            vexp = jnp.concatenate(
                [jnp.where(lane_grp == gl, part, 0.0).astype(BF16) for gl in range(gpm)], axis=0)
            wpart = wcat_ref[:, nt * gpm * GMLP_CHUNK:(nt + 1) * gpm * GMLP_CHUNK]
            mixed.append(jnp.dot(wpart, vexp, preferred_element_type=F32))
        mixed = jnp.concatenate(mixed, axis=-1) + bias_ref[...]
        u = _gelu(z_ref[c0:c0 + GMLP_CHUNK, U_OFF:U_OFF + GMLP_WIDTH])
        gm_ref[c0:c0 + GMLP_CHUNK, :] = (u * mixed).astype(BF16)

    ya = jnp.dot(attn_ref[...], wbr_ref[0], preferred_element_type=F32)
    yg = jnp.dot(gm_ref[...], wbr_ref[1], preferred_element_type=F32)
    ga = jax.nn.sigmoid(z_ref[:, GL_OFF:GL_OFF + D_MODEL])
    gg = jax.nn.sigmoid(z_ref[:, GL_OFF + D_MODEL:GL_OFF + 2 * D_MODEL])
    merged = (ga * ya + gg * yg).astype(BF16)
    h1 = x + jnp.dot(merged, wout_ref[...], preferred_element_type=F32)
    h1_ref[0] = h1

    hn = _rms(h1, nffn_ref[...])
    _store_packed(hn_ref, hn)

    hn_hi = hn.astype(BF16)
    hn_lo = (hn - hn_hi.astype(F32)).astype(BF16)
    w_hi = wr_ref[0]
    w_lo = wr_ref[1]
    logits = (jnp.dot(hn_hi, w_hi, preferred_element_type=F32)
              + jnp.dot(hn_hi, w_lo, preferred_element_type=F32)
              + jnp.dot(hn_lo, w_hi, preferred_element_type=F32)) + br_ref[...]
    lt = jnp.transpose(logits)[0:N_EXPERTS, :]
    eid = lax.broadcasted_iota(jnp.int32, lt.shape, 0)
    vals, ids = [], []
    for _ in range(TOP_K):
        mx = jnp.max(lt, axis=0, keepdims=True)
        am = jnp.min(jnp.where(lt == mx, eid, N_EXPERTS), axis=0, keepdims=True)
        vals.append(mx)
        ids.append(am)
        lt = jnp.where(eid == am, -jnp.inf, lt)
    ex = [jnp.exp(v - vals[0]) for v in vals]
    tot = ex[0] + ex[1] + ex[2] + ex[3]
    g8 = jnp.concatenate([e / tot for e in ex] + [jnp.zeros_like(tot)] * (8 - TOP_K), axis=0)
    gates = jnp.concatenate([g8, jnp.zeros((LANES - 8, ts), F32)], axis=0)
    gate_ref[0] = jnp.transpose(gates)

    @pl.when((pl.program_id(0) == 0) & (i == 0))
    def _():
        cnt_ref[...] = jnp.zeros_like(cnt_ref)

    sel = [eid == a for a in ids]
    member = (sel[0] | sel[1] | sel[2] | sel[3]).astype(F32)
    before = jnp.dot(member.astype(BF16), tri_ref[...], preferred_element_type=F32) + cnt_ref[:, 0:1]
    ranks = [jnp.sum(jnp.where(s_, before, 0.0), axis=0, keepdims=True).astype(jnp.int32) for s_ in sel]
    route_ref[0] = jnp.concatenate(ids + ranks, axis=0)
    cnt_ref[...] = cnt_ref[...] + jnp.sum(member, axis=1, keepdims=True)
    cnt_out_ref[...] = cnt_ref[...]


def _mixer_call(x, rope_tab, prm):
    b, s, d = x.shape
    ts = MIX_TILE
    nt = s // ts
    nb = s // ATTN_BLOCK
    per = ts // ATTN_BLOCK
    const2 = lambda bi, i: (0, 0)
    const3 = lambda bi, i: (0, 0, 0)

    def wspec(arr):
        return pl.BlockSpec(arr.shape, const2 if arr.ndim == 2 else const3)

    in_specs = [
        pl.BlockSpec((1, ts, d), lambda bi, i: (bi, i, 0)),
        pl.BlockSpec((1, ATTN_BLOCK, d), lambda bi, i: (bi, jnp.maximum(i * per - 1, 0), 0)),
        pl.BlockSpec((1, ATTN_BLOCK, d), lambda bi, i: (bi, jnp.minimum((i + 1) * per, nb - 1), 0)),
        pl.BlockSpec((ts, 3 * LANES), lambda bi, i: (i, 0)),
        pl.BlockSpec((ATTN_BLOCK, 3 * LANES), lambda bi, i: (jnp.maximum(i * per - 1, 0), 0)),
        pl.BlockSpec((ATTN_BLOCK, 3 * LANES), lambda bi, i: (jnp.minimum((i + 1) * per, nb - 1), 0)),
    ] + [wspec(a) for a in prm]
    out_shape = [
        jax.ShapeDtypeStruct((b, s, d), F32),
        jax.ShapeDtypeStruct((b * s * SUB, LANES), U32),
        jax.ShapeDtypeStruct((b, 2 * TOP_K, s), jnp.int32),
        jax.ShapeDtypeStruct((b, s, LANES), F32),
        jax.ShapeDtypeStruct((N_EXPERTS, LANES), F32),
    ]
    out_specs = [
        pl.BlockSpec((1, ts, d), lambda bi, i: (bi, i, 0)),
        pl.BlockSpec((ts * SUB, LANES), lambda bi, i: (bi * nt + i, 0)),
        pl.BlockSpec((1, 2 * TOP_K, ts), lambda bi, i: (bi, 0, i)),
        pl.BlockSpec((1, ts, LANES), lambda bi, i: (bi, i, 0)),
        pl.BlockSpec((N_EXPERTS, LANES), const2),
    ]
    scratch = [
        pltpu.VMEM((ts, IN_WIDTH), F32),
        pltpu.VMEM((ts, ATTN_WIDTH), BF16),
        pltpu.VMEM((ts + 2 * ATTN_BLOCK, KV_WIDTH), BF16),
        pltpu.VMEM((ts + 2 * ATTN_BLOCK, 2 * N_KV_HEADS * LANES), BF16),
        pltpu.VMEM((ts, ATTN_WIDTH), BF16),
        pltpu.VMEM((ts, GMLP_WIDTH), BF16),
        pltpu.VMEM((N_EXPERTS, LANES), F32),
    ]
    return pl.pallas_call(
        functools.partial(_mixer_kernel, n_blocks_seq=nb),
        grid=(b, nt),
        in_specs=in_specs,
        out_specs=out_specs,
        out_shape=out_shape,
        scratch_shapes=scratch,
        compiler_params=pltpu.CompilerParams(
            dimension_semantics=("arbitrary", "arbitrary"), vmem_limit_bytes=VMEM_LIMIT),
        name="mixer",
    )(x, x, x, rope_tab, rope_tab, rope_tab, *prm)


def _sc_workers():
    info = plsc.get_sparse_core_info()
    return info.num_cores, info.num_cores * info.num_subcores


def _sc_dispatch(rows2d, slot_rows, n_out_rows):
    n_cores, n_workers = _sc_workers()
    chunk_rows = DISPATCH_TOKENS * SUB
    halves = chunk_rows // SC_CHUNK
    n_chunks = rows2d.shape[0] // chunk_rows
    per_w = n_chunks // n_workers
    idx_rows_per_choice = rows2d.shape[0] // SC_CHUNK
    mesh = plsc.VectorSubcoreMesh(core_axis_name="c", subcore_axis_name="s")

    @functools.partial(
        pl.kernel, mesh=mesh,
        out_type=jax.ShapeDtypeStruct((n_out_rows, LANES), rows2d.dtype),
        scratch_types=[pltpu.VMEM((TOP_K * halves, SC_CHUNK), jnp.int32),
                       pltpu.VMEM((chunk_rows, LANES), rows2d.dtype),
                       pltpu.SemaphoreType.DMA, pltpu.SemaphoreType.DMA],
    )
    def k(src_hbm, idx_hbm, out_hbm, idx_v, rows_v, sem_in, sem_out):
        wid = lax.axis_index("s") * n_cores + lax.axis_index("c")

        @pl.loop(0, per_w)
        def _(j):
            c = wid * per_w + j
            loads = [pltpu.async_copy(src_hbm.at[pl.ds(c * chunk_rows, chunk_rows)], rows_v, sem_in)]
            loads += [
                pltpu.async_copy(idx_hbm.at[pl.ds(kk * idx_rows_per_choice + c * halves, halves)],
                                 idx_v.at[pl.ds(kk * halves, halves)], sem_in)
                for kk in range(TOP_K)]
            for cp in loads:
                cp.wait()
            copies = [
                pltpu.async_copy(rows_v.at[pl.ds((q % halves) * SC_CHUNK, SC_CHUNK)],
                                 out_hbm.at[idx_v.at[q]], sem_out)
                for q in range(TOP_K * halves)]
            for cp in copies:
                cp.wait()

    return k(rows2d, slot_rows)


def _sc_gather(table2d, idx2d):
    n_cores, n_workers = _sc_workers()
    n = idx2d.shape[0] * SC_CHUNK
    per_w = n // n_workers
    step_rows = GATHER_STREAMS * SC_CHUNK
    idx_tile = 8
    mesh = plsc.VectorSubcoreMesh(core_axis_name="c", subcore_axis_name="s")

    @functools.partial(
        pl.kernel, mesh=mesh,
        out_type=jax.ShapeDtypeStruct((n, LANES), table2d.dtype),
        scratch_types=[pltpu.VMEM((idx_tile, SC_CHUNK), jnp.int32),
                       pltpu.VMEM((step_rows, LANES), table2d.dtype),
                       pltpu.SemaphoreType.DMA],
    )
    def k(table_hbm, idx_hbm, out_hbm, idx_v, rows_v, sem):
        wid = lax.axis_index("s") * n_cores + lax.axis_index("c")

        @pl.loop(0, per_w // (idx_tile * SC_CHUNK))
        def _(j):
            row0 = pl.multiple_of(wid * (per_w // SC_CHUNK) + j * idx_tile, idx_tile)
            pltpu.sync_copy(idx_hbm.at[pl.ds(row0, idx_tile)], idx_v)
            for part in range(idx_tile // GATHER_STREAMS):
                copies = [pltpu.async_copy(table_hbm.at[idx_v.at[part * GATHER_STREAMS + q]],
                                           rows_v.at[pl.ds(q * SC_CHUNK, SC_CHUNK)], sem)
                          for q in range(GATHER_STREAMS)]
                for cp in copies:
                    cp.wait()
                pltpu.sync_copy(rows_v, out_hbm.at[pl.ds((row0 + part * GATHER_STREAMS) * SC_CHUNK, step_rows)])

    return k(table2d, idx2d)


def _moe_kernel(be_ref, nv_ref, x_ref, wgu_ref, bgu_ref, wd_ref, bd_ref, y_ref, wgu_s, wd_s, slab):
    i = pl.program_id(0)
    nv = nv_ref[i]

    cw = MOE_COLS
    n_chunks = D_FF // cw

    @pl.when((i == 0) | (be_ref[i] != be_ref[jnp.maximum(i - 1, 0)]))
    def _():
        for c in range(n_chunks):
            wgu_s[:, 2 * c * cw:(2 * c + 1) * cw] = wgu_ref[0, :, c * cw:(c + 1) * cw].astype(BF16)
            wgu_s[:, (2 * c + 1) * cw:(2 * c + 2) * cw] = wgu_ref[0, :, D_FF + c * cw:D_FF + (c + 1) * cw].astype(BF16)
        half = D_FF // 2
        for c in range(D_MODEL // LANES):
            slab[c, pl.ds(0, half, stride=2), :] = wd_ref[0, 0:half, c * LANES:(c + 1) * LANES]
            slab[c, pl.ds(1, half, stride=2), :] = wd_ref[0, half:D_FF, c * LANES:(c + 1) * LANES]
        for c in range(D_MODEL // LANES):
            wd_s[:, c * LANES:(c + 1) * LANES] = slab[c].astype(BF16)

    @pl.when(nv > 0)
    def _():
        x = _load_packed(x_ref, MOE_BLOCK)
        live = lax.broadcasted_iota(jnp.int32, x.shape, 0) < nv
        xe = jnp.where(live, x, 0.0).astype(BF16)
        even = (lax.broadcasted_iota(jnp.int32, (MOE_BLOCK, cw), 1) % 2) == 0
        y = bd_ref[0]

        def gate_up(c):
            return jnp.dot(xe, wgu_s[:, 2 * c * cw:(2 * c + 2) * cw], preferred_element_type=F32)

        h_next = gate_up(0)
        for c in range(n_chunks):
            h = h_next
            if c + 1 < n_chunks:
                h_next = gate_up(c + 1)
            h_a = h[:, 0:cw] + bgu_ref[0, :, c * cw:(c + 1) * cw]
            h_b = h[:, cw:2 * cw] + bgu_ref[0, :, D_FF + c * cw:D_FF + (c + 1) * cw]
            gate = jnp.where(even, h_a, pltpu.roll(h_b, 1, 1))
            up = jnp.where(even, pltpu.roll(h_a, cw - 1, 1), h_b)
            gate = jnp.minimum(gate, SWIGLU_LIMIT)
            up = jnp.clip(up, -SWIGLU_LIMIT, SWIGLU_LIMIT)
            act = (up + 1.0) * (gate * jax.nn.sigmoid(SWIGLU_ALPHA * gate))
            y = y + jnp.dot(act.astype(BF16), wd_s[c * cw:(c + 1) * cw, :], preferred_element_type=F32)
        _store_packed(y_ref, y)


def _moe_call(block_e, n_valid, xb, wgu, bgu, wd, bd):
    n_blocks = block_e.shape[0]
    d = D_MODEL
    grid_spec = pltpu.PrefetchScalarGridSpec(
        num_scalar_prefetch=2,
        grid=(n_blocks,),
        in_specs=[
            pl.BlockSpec((MOE_BLOCK * SUB, LANES), lambda i, be, nv: (i, 0)),
            pl.BlockSpec((1, d, 2 * D_FF), lambda i, be, nv: (be[i], 0, 0)),
            pl.BlockSpec((1, 1, 2 * D_FF), lambda i, be, nv: (be[i], 0, 0)),
            pl.BlockSpec((1, D_FF, d), lambda i, be, nv: (be[i], 0, 0)),
            pl.BlockSpec((1, 1, d), lambda i, be, nv: (be[i], 0, 0)),
        ],
        out_specs=pl.BlockSpec((MOE_BLOCK * SUB, LANES), lambda i, be, nv: (i, 0)),
        scratch_shapes=[pltpu.VMEM((d, 2 * D_FF), BF16), pltpu.VMEM((D_FF, d), BF16),
                        pltpu.VMEM((d // LANES, D_FF, LANES), F32)],
    )
    return pl.pallas_call(
        _moe_kernel,
        grid_spec=grid_spec,
        out_shape=jax.ShapeDtypeStruct((n_blocks * MOE_BLOCK * SUB, LANES), U32),
        compiler_params=pltpu.CompilerParams(
            dimension_semantics=("arbitrary",), vmem_limit_bytes=VMEM_LIMIT),
        name="moe",
    )(block_e, n_valid, xb, wgu, bgu, wd, bd)


def _combine_kernel(g0_ref, g1_ref, g2_ref, g3_ref, h1_ref, gate_ref, p_ref, nple_ref, wpg_ref, wpp_ref, o_ref):
    g = gate_ref[...]
    moe = jnp.zeros((CMB_TILE, D_MODEL), F32)
    for kk, g_ref in enumerate((g0_ref, g1_ref, g2_ref, g3_ref)):
        moe = moe + _load_packed(g_ref, CMB_TILE) * g[:, kk:kk + 1]
    h2 = h1_ref[...] + moe
    hp = _rms(h2, nple_ref[...]).astype(BF16)
    gate = jax.nn.sigmoid(jnp.dot(hp, wpg_ref[...], preferred_element_type=F32))
    proj = jnp.dot(p_ref[...].astype(BF16), wpp_ref[...], preferred_element_type=F32)
    o_ref[...] = h2 + gate * proj


def _combine_call(gathered, h1_flat, gates, p_flat, nple, wpg, wpp):
    n_tok, d = h1_flat.shape
    tc = CMB_TILE
    n_tiles = n_tok // tc
    row = lambda i: (i, 0)
    const = lambda i: (0, 0)
    g_specs = [pl.BlockSpec((tc * SUB, LANES), functools.partial(lambda i, kk: (kk * n_tiles + i, 0), kk=kk))
               for kk in range(TOP_K)]
    return pl.pallas_call(
        _combine_kernel,
        grid=(n_tiles,),
        in_specs=g_specs + [
            pl.BlockSpec((tc, d), row),
            pl.BlockSpec((tc, LANES), row),
            pl.BlockSpec((tc, PLE_DIM), row),
            pl.BlockSpec((1, d), const),
            pl.BlockSpec((d, d), const),
            pl.BlockSpec((PLE_DIM, d), const),
        ],
        out_specs=pl.BlockSpec((tc, d), row),
        out_shape=jax.ShapeDtypeStruct((n_tok, d), F32),
        compiler_params=pltpu.CompilerParams(
            dimension_semantics=("arbitrary",), vmem_limit_bytes=VMEM_LIMIT),
        name="combine",
    )(gathered, gathered, gathered, gathered, h1_flat, gates, p_flat, nple, wpg, wpp)


def _rope_table(s):
    half = ROPE_DIM // 2
    inv_freq = jnp.power(ROPE_THETA, -jnp.arange(half, dtype=F32) * (2.0 / ROPE_DIM))
    ang = jnp.arange(s, dtype=F32)[:, None] * inv_freq[None, :]
    cos, sin = jnp.cos(ang), jnp.sin(ang)
    pad1 = jnp.ones((s, HEAD_DIM - ROPE_DIM), F32)
    pad0 = jnp.zeros((s, HEAD_DIM - half), F32)
    c = jnp.concatenate([cos, cos, pad1], axis=-1)
    s_lo = jnp.concatenate([-sin, pad0], axis=-1)
    s_hi = jnp.concatenate([jnp.zeros((s, half), F32), sin, pad0[:, half:]], axis=-1)
    two = lambda t: jnp.concatenate([t, t], axis=-1)
    return jnp.concatenate([two(c), two(s_lo), two(s_hi)], axis=-1)


def _routing(route, counts, n_tok):
    ids = jnp.transpose(route[:, 0:TOP_K, :], (1, 0, 2)).reshape(TOP_K, n_tok)
    ranks = jnp.transpose(route[:, TOP_K:2 * TOP_K, :], (1, 0, 2)).reshape(TOP_K, n_tok)
    counts = counts.astype(jnp.int32)
    padded = (counts + MOE_BLOCK - 1) // MOE_BLOCK * MOE_BLOCK
    pends = jnp.cumsum(padded)
    pstarts = pends - padded
    pos = ranks
    for e in range(N_EXPERTS):
        pos = pos + jnp.where(ids == e, pstarts[e], 0)
    n_slots = -(-(n_tok * TOP_K) // MOE_BLOCK) * MOE_BLOCK + N_EXPERTS * MOE_BLOCK
    n_blocks = n_slots // MOE_BLOCK
    first = jnp.arange(n_blocks, dtype=jnp.int32) * MOE_BLOCK
    block_e = jnp.clip(jnp.sum((first[:, None] >= pends[None, :]).astype(jnp.int32), axis=1), 0, N_EXPERTS - 1)
    own = block_e[:, None] == jnp.arange(N_EXPERTS, dtype=jnp.int32)[None, :]
    left = jnp.sum(jnp.where(own, (counts + pstarts)[None, :], 0), axis=1) - first
    n_valid = jnp.clip(left, 0, MOE_BLOCK)
    return pos, block_e, n_valid.astype(jnp.int32), n_slots


def _expand_rows(pos):
    spread = (jnp.arange(SUB * LANES, dtype=jnp.int32)[None, :] // SUB
              == jnp.arange(LANES, dtype=jnp.int32)[:, None]).astype(F32)
    wide = jnp.dot(pos.reshape(-1, LANES).astype(F32), spread, precision=lax.Precision.HIGHEST)
    wide = wide.astype(jnp.int32) * SUB + (jnp.arange(SUB * LANES, dtype=jnp.int32) % SUB)[None, :]
    return wide.reshape(-1, LANES)


def _group_forward(x, p_l, rope_tab, mix_prm, moe_prm, ple_prm):
    b, s, d = x.shape
    n_tok = b * s
    h1, hn_rows, route, gates, counts = _mixer_call(x, rope_tab, mix_prm)
    pos, block_e, n_valid, n_slots = _routing(route, counts[:, 0], n_tok)
    slot_rows = _expand_rows(pos)
    xb = _sc_dispatch(hn_rows, slot_rows, n_slots * SUB)
    y = _moe_call(block_e, n_valid, xb, *moe_prm)
    gathered = _sc_gather(y, slot_rows)
    out = _combine_call(gathered, h1.reshape(n_tok, d), gates.reshape(n_tok, LANES),
                        p_l.reshape(n_tok, PLE_DIM), *ple_prm)
    return out.reshape(b, s, d)


def kernel(x_prompt, x_sample, p_prompt, p_sample, norm_mix, w_in, q_gain, k_gain, attn_sink, gmlp_v_gain, gmlp_w_s, gmlp_b_s, w_branch, w_out, norm_ffn, w_router, b_router, w_gate_up, b_gate_up, w_down, b_down, norm_ple, w_ple_gate, w_ple_proj):
    depth = norm_mix.shape[0]
    hp, hs = x_prompt, x_sample
    for l in range(depth):
        row = lambda a: a.reshape(1, -1)
        blockdiag = jnp.kron(jnp.eye(N_Q_HEADS, dtype=F32),
                             jnp.full((HEAD_DIM, HEAD_DIM), 1.0 / HEAD_DIM, F32)).astype(BF16)
        sink_heads = attn_sink[l].reshape(N_KV_HEADS, GQA_GROUP)[:, jnp.array(HEAD_ORDER)]
        sink_col = jnp.repeat(sink_heads, ATTN_BLOCK, axis=1)[..., None]
        sink_rows = jnp.where(jnp.arange(LANES) == 0, sink_col, -jnp.inf)
        first_row = (jnp.arange(LANES) == 0)[:, None]
        upper = (jnp.arange(LANES) >= HEAD_DIM)[None, :]
        sink_values = jnp.stack([first_row & upper, first_row & ~upper]).astype(BF16)
        wcat = jnp.transpose(gmlp_w_s[l], (1, 0, 2)).reshape(GMLP_CHUNK, GMLP_GROUPS * GMLP_CHUNK).astype(BF16)
        bias_full = jnp.repeat(gmlp_b_s[l].T, GMLP_GROUP_DIM, axis=1)
        wr = jnp.pad(w_router[l], ((0, 0), (0, ROUTER_PAD - N_EXPERTS)))
        wr_hi = wr.astype(BF16)
        wr_lo = (wr - wr_hi.astype(F32)).astype(BF16)
        br = jnp.pad(b_router[l], (0, ROUTER_PAD - N_EXPERTS)).reshape(1, ROUTER_PAD)
        tri = (jnp.arange(MIX_TILE)[:, None] < jnp.arange(MIX_TILE)[None, :]).astype(BF16)
        mix_prm = (
            row(norm_mix[l]), w_in[l].astype(BF16),
            row(jnp.tile(q_gain[l], N_Q_HEADS)), row(jnp.tile(k_gain[l], N_KV_HEADS)),
            sink_rows, sink_values, row(gmlp_v_gain[l]), blockdiag, wcat, bias_full,
            w_branch[l].astype(BF16), w_out[l].astype(BF16), row(norm_ffn[l]),
            jnp.stack([wr_hi, wr_lo]), br, tri,
        )
        moe_prm = (
            w_gate_up[l], b_gate_up[l].reshape(N_EXPERTS, 1, 2 * D_FF),
            w_down[l], b_down[l].reshape(N_EXPERTS, 1, D_MODEL),
        )
        ple_prm = (row(norm_ple[l]), w_ple_gate[l].astype(BF16), w_ple_proj[l].astype(BF16))
        rope_tab = _rope_table(max(hp.shape[1], hs.shape[1]))
        hp = _group_forward(hp, p_prompt[l], rope_tab, mix_prm, moe_prm, ple_prm)
        hs = _group_forward(hs, p_sample[l], rope_tab, mix_prm, moe_prm, ple_prm)
    return (hp.astype(x_prompt.dtype), hs.astype(x_sample.dtype))
```

```python
import functools

import jax
import jax.numpy as jnp
import numpy as np
from jax import lax
from jax.experimental import pallas as pl
from jax.experimental.pallas import tpu as pltpu
from jax.experimental.pallas import tpu_sc as plsc

D_MODEL = 1024
HEAD_DIM = 64
N_Q_HEADS = 8
N_KV_HEADS = 2
GQA_GROUP = N_Q_HEADS // N_KV_HEADS
HEAD_ORDER = (0, 2, 1, 3)
ATTN_WIDTH = N_Q_HEADS * HEAD_DIM
KV_WIDTH = N_KV_HEADS * HEAD_DIM
WINDOW = 128
ATTN_BLOCK = 128
ROPE_THETA = 500000.0
ROPE_DIM = HEAD_DIM // 4
GMLP_WIDTH = D_MODEL // 2
GMLP_GROUPS = 8
GMLP_GROUP_DIM = GMLP_WIDTH // GMLP_GROUPS
GMLP_CHUNK = 128
N_BRANCH = 2
IN_WIDTH = ATTN_WIDTH + 2 * KV_WIDTH + 2 * GMLP_WIDTH + N_BRANCH * D_MODEL
N_EXPERTS = 32
TOP_K = 4
D_FF = D_MODEL
SWIGLU_ALPHA = 1.702
SWIGLU_LIMIT = 7.0
PLE_DIM = 256
EPS = 1e-6

Q_OFF = 0
K_OFF = ATTN_WIDTH
V_OFF = K_OFF + KV_WIDTH
U_OFF = V_OFF + KV_WIDTH
VG_OFF = U_OFF + GMLP_WIDTH
GL_OFF = VG_OFF + GMLP_WIDTH

LANES = 128
PACK_WORDS = D_MODEL // 2
SUB = PACK_WORDS // LANES
MIX_TILE = 512
PROJ_COLS = 256
MOE_BLOCK = 512
MOE_COLS = 256
CMB_TILE = 512
ROUTER_PAD = 128
SC_CHUNK = 128
DISPATCH_TOKENS = 64
GATHER_STREAMS = 4
VMEM_LIMIT = 56 * 1024 * 1024

BF16 = jnp.bfloat16
F32 = jnp.float32
U32 = jnp.uint32


def _rms(x, gain):
    return x * lax.rsqrt(jnp.mean(x * x, axis=-1, keepdims=True) + EPS) * gain


def _gelu(x):
    return 0.5 * x * (1.0 + lax.erf(x * np.float32(np.sqrt(0.5))))


def _head_rms(x, blockdiag, gain):
    ms = jnp.dot((x * x).astype(BF16), blockdiag, preferred_element_type=F32)
    return x * lax.rsqrt(ms + EPS) * gain


def _rope(x, cos, sin_lo, sin_hi):
    w = x.shape[-1]
    return x * cos + pltpu.roll(x, w - ROPE_DIM // 2, 1) * sin_lo + pltpu.roll(x, ROPE_DIM // 2, 1) * sin_hi


def _value_blocks(v):
    swapped = pltpu.roll(v, HEAD_DIM, 1)
    low = lax.broadcasted_iota(jnp.int32, v.shape, 1) < HEAD_DIM
    blocks = [jnp.where(low, v, 1.0), jnp.where(low, 1.0, swapped), jnp.where(low, swapped, 1.0), jnp.where(low, 1.0, v)]
    return jnp.concatenate(blocks, axis=1).astype(BF16)


def _tile_lanes(t, reps):
    return t if reps == 1 else jnp.concatenate([t] * reps, axis=-1)


def _store_packed(ref, x):
    rows = x.shape[0]
    hi = lax.bitcast_convert_type(x[:, :PACK_WORDS].astype(BF16).astype(F32), U32)
    lo = lax.bitcast_convert_type(x[:, PACK_WORDS:].astype(BF16).astype(F32), U32)
    words = hi | (lo >> 16)
    for j in range(SUB):
        ref[pl.ds(j, rows, stride=SUB), :] = words[:, j * LANES:(j + 1) * LANES]


def _load_packed(ref, rows):
    words = jnp.concatenate([ref[pl.ds(j, rows, stride=SUB), :] for j in range(SUB)], axis=1)
    left = lax.bitcast_convert_type(words & np.uint32(0xFFFF0000), F32)
    right = lax.bitcast_convert_type(words << 16, F32)
    return jnp.concatenate([left, right], axis=1)


def _mixer_kernel(x_ref, xp_ref, xn_ref, rp_ref, rpp_ref, rpn_ref,
                  nmix_ref, win_ref, qg_ref, kg_ref, sink_ref, sinkv_ref, vgain_ref, bdq_ref,
                  wcat_ref, bias_ref, wbr_ref, wout_ref, nffn_ref, wr_ref, br_ref, tri_ref,
                  h1_ref, hn_ref, route_ref, gate_ref, cnt_out_ref,
                  z_ref, q_ref, k_ref, v_ref, attn_ref, gm_ref, cnt_ref, *, n_blocks_seq):
    ts = MIX_TILE
    i = pl.program_id(1)
    x = x_ref[0]
    xn = _rms(x, nmix_ref[...]).astype(BF16)

    def project(lo, hi):
        z_ref[:, lo:hi] = jnp.dot(xn, win_ref[:, lo:hi], preferred_element_type=F32)

    project(Q_OFF, U_OFF)
    n_items = (ts // ATTN_BLOCK) * N_KV_HEADS
    later = [(U_OFF + t * PROJ_COLS, U_OFF + (t + 1) * PROJ_COLS) for t in range((IN_WIDTH - U_OFF) // PROJ_COLS)]
    per_item = [len(later) // n_items + (1 if n < len(later) % n_items else 0) for n in range(n_items)]

    cos = rp_ref[:, 0:LANES]
    sin_lo = rp_ref[:, LANES:2 * LANES]
    sin_hi = rp_ref[:, 2 * LANES:3 * LANES]
    bdq = bdq_ref[...]
    bdk = bdq_ref[0:KV_WIDTH, 0:KV_WIDTH]

    q = _head_rms(z_ref[:, Q_OFF:Q_OFF + ATTN_WIDTH], bdq, qg_ref[...])
    reps = ATTN_WIDTH // LANES
    q = _rope(q, _tile_lanes(cos, reps), _tile_lanes(sin_lo, reps), _tile_lanes(sin_hi, reps))
    q_ref[...] = (q * (HEAD_DIM ** -0.5)).astype(BF16)

    k = _head_rms(z_ref[:, K_OFF:K_OFF + KV_WIDTH], bdk, kg_ref[...])
    k_ref[ATTN_BLOCK:ATTN_BLOCK + ts, :] = _rope(k, cos, sin_lo, sin_hi).astype(BF16)
    v_ref[ATTN_BLOCK:ATTN_BLOCK + ts, :] = _value_blocks(z_ref[:, V_OFF:V_OFF + KV_WIDTH])

    xh = jnp.concatenate([xp_ref[0], xn_ref[0]], axis=0)
    xhn = _rms(xh, nmix_ref[...]).astype(BF16)
    zh = jnp.dot(xhn, win_ref[:, K_OFF:K_OFF + 2 * KV_WIDTH], preferred_element_type=F32)
    kh = _head_rms(zh[:, 0:KV_WIDTH], bdk, kg_ref[...])
    rph = jnp.concatenate([rpp_ref[...], rpn_ref[...]], axis=0)
    kh = _rope(kh, rph[:, 0:LANES], rph[:, LANES:2 * LANES], rph[:, 2 * LANES:3 * LANES]).astype(BF16)
    vh = _value_blocks(zh[:, KV_WIDTH:2 * KV_WIDTH])
    k_ref[0:ATTN_BLOCK, :] = kh[0:ATTN_BLOCK]
    k_ref[ATTN_BLOCK + ts:2 * ATTN_BLOCK + ts, :] = kh[ATTN_BLOCK:]
    v_ref[0:ATTN_BLOCK, :] = vh[0:ATTN_BLOCK]
    v_ref[ATTN_BLOCK + ts:2 * ATTN_BLOCK + ts, :] = vh[ATTN_BLOCK:]

    rows = GQA_GROUP * ATTN_BLOCK
    keys = 3 * ATTN_BLOCK
    r = lax.broadcasted_iota(jnp.int32, (rows, keys), 0) % ATTN_BLOCK
    c = lax.broadcasted_iota(jnp.int32, (rows, keys), 1)
    band = (c >= r) & (c <= r + 2 * WINDOW)
    low = lax.broadcasted_iota(jnp.int32, (ATTN_BLOCK, 2 * HEAD_DIM), 1) < HEAD_DIM
    pair = 2 * ATTN_BLOCK
    for qb in range(ts // ATTN_BLOCK):
        gb = i * (ts // ATTN_BLOCK) + qb
        lo = jnp.where(gb == 0, ATTN_BLOCK, 0)
        hi = jnp.where(gb == n_blocks_seq - 1, 2 * ATTN_BLOCK, keys)
        valid = band & (c >= lo) & (c < hi)
        r0 = qb * ATTN_BLOCK
        for j in range(N_KV_HEADS):
            for _ in range(per_item[qb * N_KV_HEADS + j]):
                project(*later.pop(0))
            q4 = jnp.concatenate(
                [q_ref[r0:r0 + ATTN_BLOCK, (GQA_GROUP * j + g) * HEAD_DIM:(GQA_GROUP * j + g + 1) * HEAD_DIM]
                 for g in HEAD_ORDER], axis=0)
            kw = k_ref[r0:r0 + keys, j * HEAD_DIM:(j + 1) * HEAD_DIM]
            s = lax.dot_general(q4, kw, (((1,), (1,)), ((), ())), preferred_element_type=F32)
            s = jnp.concatenate([jnp.where(valid, s, -jnp.inf), sink_ref[j]], axis=1)
            p = jnp.exp(s - jnp.max(s, axis=-1, keepdims=True)).astype(BF16)
            outs = []
            for par in range(2):
                vw = jnp.concatenate([v_ref[r0:r0 + keys, (2 * j + par) * LANES:(2 * j + par + 1) * LANES],
                                      sinkv_ref[par]], axis=0)
                o = jnp.dot(p[par * pair:(par + 1) * pair], vw, preferred_element_type=F32)
                outs.append(o / pltpu.roll(o, HEAD_DIM, 1))
            for a in range(GQA_GROUP // 2):
                both = jnp.where(low, outs[0][a * ATTN_BLOCK:(a + 1) * ATTN_BLOCK],
                                 outs[1][a * ATTN_BLOCK:(a + 1) * ATTN_BLOCK])
                h0 = GQA_GROUP * j + 2 * a
                attn_ref[r0:r0 + ATTN_BLOCK, h0 * HEAD_DIM:(h0 + 2) * HEAD_DIM] = both.astype(BF16)

    half = GMLP_WIDTH // 2
    gpm = half // GMLP_GROUP_DIM
    lane_grp = lax.broadcasted_iota(jnp.int32, (GMLP_CHUNK, half), 1) // GMLP_GROUP_DIM
    for ch in range(ts // GMLP_CHUNK):
        c0 = ch * GMLP_CHUNK
        vg = _gelu(z_ref[c0:c0 + GMLP_CHUNK, VG_OFF:VG_OFF + GMLP_WIDTH])
        vgn = _head_rms(vg, bdq, vgain_ref[...])
        mixed = []
        for nt in range(2):
            part = vgn[:, nt * half:(nt + 1) * half]
            vexp = jnp.concatenate(
                [jnp.where(lane_grp == gl, part, 0.0).astype(BF16) for gl in range(gpm)], axis=0)
            wpart = wcat_ref[:, nt * gpm * GMLP_CHUNK:(nt + 1) * gpm * GMLP_CHUNK]
            mixed.append(jnp.dot(wpart, vexp, preferred_element_type=F32))
        mixed = jnp.concatenate(mixed, axis=-1) + bias_ref[...]
        u = _gelu(z_ref[c0:c0 + GMLP_CHUNK, U_OFF:U_OFF + GMLP_WIDTH])
        gm_ref[c0:c0 + GMLP_CHUNK, :] = (u * mixed).astype(BF16)

    ya = jnp.dot(attn_ref[...], wbr_ref[0], preferred_element_type=F32)
    yg = jnp.dot(gm_ref[...], wbr_ref[1], preferred_element_type=F32)
    ga = jax.nn.sigmoid(z_ref[:, GL_OFF:GL_OFF + D_MODEL])
    gg = jax.nn.sigmoid(z_ref[:, GL_OFF + D_MODEL:GL_OFF + 2 * D_MODEL])
    merged = (ga * ya + gg * yg).astype(BF16)
    h1 = x + jnp.dot(merged, wout_ref[...], preferred_element_type=F32)
    h1_ref[0] = h1

    hn = _rms(h1, nffn_ref[...])
    _store_packed(hn_ref, hn)

    hn_hi = hn.astype(BF16)
    hn_lo = (hn - hn_hi.astype(F32)).astype(BF16)
    by_hi = jnp.dot(hn_hi, wr_ref[...], preferred_element_type=F32)
    by_lo = jnp.dot(hn_lo, wr_ref[:, 0:ROUTER_PAD], preferred_element_type=F32)
    logits = by_hi[:, 0:ROUTER_PAD] + by_hi[:, ROUTER_PAD:2 * ROUTER_PAD] + by_lo + br_ref[...]
    lt = jnp.transpose(logits)[0:N_EXPERTS, :]
    eid = lax.broadcasted_iota(jnp.int32, lt.shape, 0)
    vals, ids = [], []
    for _ in range(TOP_K):
        mx = jnp.max(lt, axis=0, keepdims=True)
        am = jnp.min(jnp.where(lt == mx, eid, N_EXPERTS), axis=0, keepdims=True)
        vals.append(mx)
        ids.append(am)
        lt = jnp.where(eid == am, -jnp.inf, lt)
    ex = [jnp.exp(v - vals[0]) for v in vals]
    tot = ex[0] + ex[1] + ex[2] + ex[3]
    g8 = jnp.concatenate([e / tot for e in ex] + [jnp.zeros_like(tot)] * (8 - TOP_K), axis=0)
    gates = jnp.concatenate([g8, jnp.zeros((LANES - 8, ts), F32)], axis=0)
    gate_ref[0] = jnp.transpose(gates)

    @pl.when((pl.program_id(0) == 0) & (i == 0))
    def _():
        cnt_ref[...] = jnp.zeros_like(cnt_ref)

    sel = [eid == a for a in ids]
    member = (sel[0] | sel[1] | sel[2] | sel[3]).astype(F32)
    before = jnp.dot(member.astype(BF16), tri_ref[...], preferred_element_type=F32) + cnt_ref[:, 0:1]
    ranks = [jnp.sum(jnp.where(s_, before, 0.0), axis=0, keepdims=True).astype(jnp.int32) for s_ in sel]
    route_ref[0] = jnp.concatenate(ids + ranks, axis=0)
    cnt_ref[...] = cnt_ref[...] + jnp.sum(member, axis=1, keepdims=True)
    cnt_out_ref[...] = cnt_ref[...]


def _mixer_call(x, rope_tab, prm):
    b, s, d = x.shape
    ts = MIX_TILE
    nt = s // ts
    nb = s // ATTN_BLOCK
    per = ts // ATTN_BLOCK
    const2 = lambda bi, i: (0, 0)
    const3 = lambda bi, i: (0, 0, 0)

    def wspec(arr):
        return pl.BlockSpec(arr.shape, const2 if arr.ndim == 2 else const3)

    in_specs = [
        pl.BlockSpec((1, ts, d), lambda bi, i: (bi, i, 0)),
        pl.BlockSpec((1, ATTN_BLOCK, d), lambda bi, i: (bi, jnp.maximum(i * per - 1, 0), 0)),
        pl.BlockSpec((1, ATTN_BLOCK, d), lambda bi, i: (bi, jnp.minimum((i + 1) * per, nb - 1), 0)),
        pl.BlockSpec((ts, 3 * LANES), lambda bi, i: (i, 0)),
        pl.BlockSpec((ATTN_BLOCK, 3 * LANES), lambda bi, i: (jnp.maximum(i * per - 1, 0), 0)),
        pl.BlockSpec((ATTN_BLOCK, 3 * LANES), lambda bi, i: (jnp.minimum((i + 1) * per, nb - 1), 0)),
    ] + [wspec(a) for a in prm]
    out_shape = [
        jax.ShapeDtypeStruct((b, s, d), F32),
        jax.ShapeDtypeStruct((b * s * SUB, LANES), U32),
        jax.ShapeDtypeStruct((b, 2 * TOP_K, s), jnp.int32),
        jax.ShapeDtypeStruct((b, s, LANES), F32),
        jax.ShapeDtypeStruct((N_EXPERTS, LANES), F32),
    ]
    out_specs = [
        pl.BlockSpec((1, ts, d), lambda bi, i: (bi, i, 0)),
        pl.BlockSpec((ts * SUB, LANES), lambda bi, i: (bi * nt + i, 0)),
        pl.BlockSpec((1, 2 * TOP_K, ts), lambda bi, i: (bi, 0, i)),
        pl.BlockSpec((1, ts, LANES), lambda bi, i: (bi, i, 0)),
        pl.BlockSpec((N_EXPERTS, LANES), const2),
    ]
    scratch = [
        pltpu.VMEM((ts, IN_WIDTH), F32),
        pltpu.VMEM((ts, ATTN_WIDTH), BF16),
        pltpu.VMEM((ts + 2 * ATTN_BLOCK, KV_WIDTH), BF16),
        pltpu.VMEM((ts + 2 * ATTN_BLOCK, 2 * N_KV_HEADS * LANES), BF16),
        pltpu.VMEM((ts, ATTN_WIDTH), BF16),
        pltpu.VMEM((ts, GMLP_WIDTH), BF16),
        pltpu.VMEM((N_EXPERTS, LANES), F32),
    ]
    return pl.pallas_call(
        functools.partial(_mixer_kernel, n_blocks_seq=nb),
        grid=(b, nt),
        in_specs=in_specs,
        out_specs=out_specs,
        out_shape=out_shape,
        scratch_shapes=scratch,
        compiler_params=pltpu.CompilerParams(
            dimension_semantics=("arbitrary", "arbitrary"), vmem_limit_bytes=VMEM_LIMIT),
        name="mixer",
    )(x, x, x, rope_tab, rope_tab, rope_tab, *prm)


def _sc_workers():
    info = plsc.get_sparse_core_info()
    return info.num_cores, info.num_cores * info.num_subcores


def _sc_dispatch(rows2d, slot_rows, n_out_rows):
    n_cores, n_workers = _sc_workers()
    chunk_rows = DISPATCH_TOKENS * SUB
    halves = chunk_rows // SC_CHUNK
    n_chunks = rows2d.shape[0] // chunk_rows
    per_w = n_chunks // n_workers
    idx_rows_per_choice = rows2d.shape[0] // SC_CHUNK
    mesh = plsc.VectorSubcoreMesh(core_axis_name="c", subcore_axis_name="s")

    @functools.partial(
        pl.kernel, mesh=mesh,
        out_type=jax.ShapeDtypeStruct((n_out_rows, LANES), rows2d.dtype),
        scratch_types=[pltpu.VMEM((TOP_K * halves, SC_CHUNK), jnp.int32),
                       pltpu.VMEM((chunk_rows, LANES), rows2d.dtype),
                       pltpu.SemaphoreType.DMA, pltpu.SemaphoreType.DMA],
    )
    def k(src_hbm, idx_hbm, out_hbm, idx_v, rows_v, sem_in, sem_out):
        wid = lax.axis_index("s") * n_cores + lax.axis_index("c")

        @pl.loop(0, per_w)
        def _(j):
            c = wid * per_w + j
            loads = [pltpu.async_copy(src_hbm.at[pl.ds(c * chunk_rows, chunk_rows)], rows_v, sem_in)]
            loads += [
                pltpu.async_copy(idx_hbm.at[pl.ds(kk * idx_rows_per_choice + c * halves, halves)],
                                 idx_v.at[pl.ds(kk * halves, halves)], sem_in)
                for kk in range(TOP_K)]
            for cp in loads:
                cp.wait()
            copies = [
                pltpu.async_copy(rows_v.at[pl.ds((q % halves) * SC_CHUNK, SC_CHUNK)],
                                 out_hbm.at[idx_v.at[q]], sem_out)
                for q in range(TOP_K * halves)]
            for cp in copies:
                cp.wait()

    return k(rows2d, slot_rows)


def _sc_gather(table2d, idx2d):
    n_cores, n_workers = _sc_workers()
    n = idx2d.shape[0] * SC_CHUNK
    per_w = n // n_workers
    step_rows = GATHER_STREAMS * SC_CHUNK
    idx_tile = 8
    mesh = plsc.VectorSubcoreMesh(core_axis_name="c", subcore_axis_name="s")

    @functools.partial(
        pl.kernel, mesh=mesh,
        out_type=jax.ShapeDtypeStruct((n, LANES), table2d.dtype),
        scratch_types=[pltpu.VMEM((idx_tile, SC_CHUNK), jnp.int32),
                       pltpu.VMEM((step_rows, LANES), table2d.dtype),
                       pltpu.SemaphoreType.DMA],
    )
    def k(table_hbm, idx_hbm, out_hbm, idx_v, rows_v, sem):
        wid = lax.axis_index("s") * n_cores + lax.axis_index("c")

        @pl.loop(0, per_w // (idx_tile * SC_CHUNK))
        def _(j):
            row0 = pl.multiple_of(wid * (per_w // SC_CHUNK) + j * idx_tile, idx_tile)
            pltpu.sync_copy(idx_hbm.at[pl.ds(row0, idx_tile)], idx_v)
            for part in range(idx_tile // GATHER_STREAMS):
                copies = [pltpu.async_copy(table_hbm.at[idx_v.at[part * GATHER_STREAMS + q]],
                                           rows_v.at[pl.ds(q * SC_CHUNK, SC_CHUNK)], sem)
                          for q in range(GATHER_STREAMS)]
                for cp in copies:
                    cp.wait()
                pltpu.sync_copy(rows_v, out_hbm.at[pl.ds((row0 + part * GATHER_STREAMS) * SC_CHUNK, step_rows)])

    return k(table2d, idx2d)


def _expert_weight_copies(wgu_hbm, wd_hbm, wgu_buf, wd_buf, sem, expert, slot):
    return (pltpu.make_async_copy(wgu_hbm.at[expert], wgu_buf.at[slot], sem.at[0, slot]),
            pltpu.make_async_copy(wd_hbm.at[expert], wd_buf.at[slot], sem.at[1, slot]))


def _moe_kernel(be_ref, nv_ref, nxt_ref, par_ref, x_ref, wgu_hbm, bgu_ref, wd_hbm, bd_ref, y_ref,
                wgu_buf, wd_buf, sem, wgu_s, wd_s, slab):
    i = pl.program_id(0)
    nv = nv_ref[i]
    slot = par_ref[i]

    cw = MOE_COLS
    n_chunks = D_FF // cw
    copies = functools.partial(_expert_weight_copies, wgu_hbm, wd_hbm, wgu_buf, wd_buf, sem)

    @pl.when(i == 0)
    def _():
        for cp in copies(be_ref[0], slot):
            cp.start()

    @pl.when((i == 0) | (be_ref[i] != be_ref[jnp.maximum(i - 1, 0)]))
    def _():
        for cp in copies(be_ref[i], slot):
            cp.wait()

        @pl.when(nxt_ref[i] >= 0)
        def _():
            for cp in copies(nxt_ref[i], 1 - slot):
                cp.start()

        for c in range(n_chunks):
            wgu_s[:, 2 * c * cw:(2 * c + 1) * cw] = wgu_buf[slot, :, c * cw:(c + 1) * cw].astype(BF16)
            wgu_s[:, (2 * c + 1) * cw:(2 * c + 2) * cw] = (
                wgu_buf[slot, :, D_FF + c * cw:D_FF + (c + 1) * cw].astype(BF16))
        half = D_FF // 2
        for c in range(D_MODEL // LANES):
            slab[c, pl.ds(0, half, stride=2), :] = wd_buf[slot, 0:half, c * LANES:(c + 1) * LANES]
            slab[c, pl.ds(1, half, stride=2), :] = wd_buf[slot, half:D_FF, c * LANES:(c + 1) * LANES]
        for c in range(D_MODEL // LANES):
            wd_s[:, c * LANES:(c + 1) * LANES] = slab[c].astype(BF16)

    @pl.when(nv > 0)
    def _():
        x = _load_packed(x_ref, MOE_BLOCK)
        live = lax.broadcasted_iota(jnp.int32, x.shape, 0) < nv
        xe = jnp.where(live, x, 0.0).astype(BF16)
        even = (lax.broadcasted_iota(jnp.int32, (MOE_BLOCK, cw), 1) % 2) == 0
        y = bd_ref[0]

        def gate_up(c):
            return jnp.dot(xe, wgu_s[:, 2 * c * cw:(2 * c + 2) * cw], preferred_element_type=F32)

        h_next = gate_up(0)
        for c in range(n_chunks):
            h = h_next
            if c + 1 < n_chunks:
                h_next = gate_up(c + 1)
            h_a = h[:, 0:cw] + bgu_ref[0, :, c * cw:(c + 1) * cw]
            h_b = h[:, cw:2 * cw] + bgu_ref[0, :, D_FF + c * cw:D_FF + (c + 1) * cw]
            gate = jnp.where(even, h_a, pltpu.roll(h_b, 1, 1))
            up = jnp.where(even, pltpu.roll(h_a, cw - 1, 1), h_b)
            gate = jnp.minimum(gate, SWIGLU_LIMIT)
            up = jnp.clip(up, -SWIGLU_LIMIT, SWIGLU_LIMIT)
            act = (up + 1.0) * (gate * jax.nn.sigmoid(SWIGLU_ALPHA * gate))
            y = y + jnp.dot(act.astype(BF16), wd_s[c * cw:(c + 1) * cw, :], preferred_element_type=F32)
        _store_packed(y_ref, y)


def _moe_call(block_e, n_valid, next_e, parity, xb, wgu, bgu, wd, bd):
    n_blocks = block_e.shape[0]
    d = D_MODEL
    grid_spec = pltpu.PrefetchScalarGridSpec(
        num_scalar_prefetch=4,
        grid=(n_blocks,),
        in_specs=[
            pl.BlockSpec((MOE_BLOCK * SUB, LANES), lambda i, be, nv, nx, pr: (i, 0)),
            pl.BlockSpec(memory_space=pl.ANY),
            pl.BlockSpec((1, 1, 2 * D_FF), lambda i, be, nv, nx, pr: (be[i], 0, 0)),
            pl.BlockSpec(memory_space=pl.ANY),
            pl.BlockSpec((1, 1, d), lambda i, be, nv, nx, pr: (be[i], 0, 0)),
        ],
        out_specs=pl.BlockSpec((MOE_BLOCK * SUB, LANES), lambda i, be, nv, nx, pr: (i, 0)),
        scratch_shapes=[pltpu.VMEM((2, d, 2 * D_FF), F32), pltpu.VMEM((2, D_FF, d), F32),
                        pltpu.SemaphoreType.DMA((2, 2)),
                        pltpu.VMEM((d, 2 * D_FF), BF16), pltpu.VMEM((D_FF, d), BF16),
                        pltpu.VMEM((d // LANES, D_FF, LANES), F32)],
    )
    return pl.pallas_call(
        _moe_kernel,
        grid_spec=grid_spec,
        out_shape=jax.ShapeDtypeStruct((n_blocks * MOE_BLOCK * SUB, LANES), U32),
        compiler_params=pltpu.CompilerParams(
            dimension_semantics=("arbitrary",), vmem_limit_bytes=VMEM_LIMIT),
        name="moe",
    )(block_e, n_valid, next_e, parity, xb, wgu, bgu, wd, bd)


def _combine_kernel(g0_ref, g1_ref, g2_ref, g3_ref, h1_ref, gate_ref, p_ref, nple_ref, wpg_ref, wpp_ref, o_ref):
    g = gate_ref[...]
    moe = jnp.zeros((CMB_TILE, D_MODEL), F32)
    for kk, g_ref in enumerate((g0_ref, g1_ref, g2_ref, g3_ref)):
        moe = moe + _load_packed(g_ref, CMB_TILE) * g[:, kk:kk + 1]
    h2 = h1_ref[...] + moe
    hp = _rms(h2, nple_ref[...]).astype(BF16)
    gate = jax.nn.sigmoid(jnp.dot(hp, wpg_ref[...], preferred_element_type=F32))
    proj = jnp.dot(p_ref[...].astype(BF16), wpp_ref[...], preferred_element_type=F32)
    o_ref[...] = h2 + gate * proj


def _combine_call(gathered, h1_flat, gates, p_flat, nple, wpg, wpp):
    n_tok, d = h1_flat.shape
    tc = CMB_TILE
    n_tiles = n_tok // tc
    row = lambda i: (i, 0)
    const = lambda i: (0, 0)
    g_specs = [pl.BlockSpec((tc * SUB, LANES), functools.partial(lambda i, kk: (kk * n_tiles + i, 0), kk=kk))
               for kk in range(TOP_K)]
    return pl.pallas_call(
        _combine_kernel,
        grid=(n_tiles,),
        in_specs=g_specs + [
            pl.BlockSpec((tc, d), row),
            pl.BlockSpec((tc, LANES), row),
            pl.BlockSpec((tc, PLE_DIM), row),
            pl.BlockSpec((1, d), const),
            pl.BlockSpec((d, d), const),
            pl.BlockSpec((PLE_DIM, d), const),
        ],
        out_specs=pl.BlockSpec((tc, d), row),
        out_shape=jax.ShapeDtypeStruct((n_tok, d), F32),
        compiler_params=pltpu.CompilerParams(
            dimension_semantics=("arbitrary",), vmem_limit_bytes=VMEM_LIMIT),
        name="combine",
    )(gathered, gathered, gathered, gathered, h1_flat, gates, p_flat, nple, wpg, wpp)


def _rope_table(s):
    half = ROPE_DIM // 2
    inv_freq = jnp.power(ROPE_THETA, -jnp.arange(half, dtype=F32) * (2.0 / ROPE_DIM))
    ang = jnp.arange(s, dtype=F32)[:, None] * inv_freq[None, :]
    cos, sin = jnp.cos(ang), jnp.sin(ang)
    pad1 = jnp.ones((s, HEAD_DIM - ROPE_DIM), F32)
    pad0 = jnp.zeros((s, HEAD_DIM - half), F32)
    c = jnp.concatenate([cos, cos, pad1], axis=-1)
    s_lo = jnp.concatenate([-sin, pad0], axis=-1)
    s_hi = jnp.concatenate([jnp.zeros((s, half), F32), sin, pad0[:, half:]], axis=-1)
    two = lambda t: jnp.concatenate([t, t], axis=-1)
    return jnp.concatenate([two(c), two(s_lo), two(s_hi)], axis=-1)


def _routing(route, counts, n_tok):
    ids = jnp.transpose(route[:, 0:TOP_K, :], (1, 0, 2)).reshape(TOP_K, n_tok)
    ranks = jnp.transpose(route[:, TOP_K:2 * TOP_K, :], (1, 0, 2)).reshape(TOP_K, n_tok)
    counts = counts.astype(jnp.int32)
    padded = (counts + MOE_BLOCK - 1) // MOE_BLOCK * MOE_BLOCK
    pends = jnp.cumsum(padded)
    pstarts = pends - padded
    pos = ranks
    for e in range(N_EXPERTS):
        pos = pos + jnp.where(ids == e, pstarts[e], 0)
    n_slots = -(-(n_tok * TOP_K) // MOE_BLOCK) * MOE_BLOCK + N_EXPERTS * MOE_BLOCK
    n_blocks = n_slots // MOE_BLOCK
    first = jnp.arange(n_blocks, dtype=jnp.int32) * MOE_BLOCK
    block_e = jnp.clip(jnp.sum((first[:, None] >= pends[None, :]).astype(jnp.int32), axis=1), 0, N_EXPERTS - 1)
    own = block_e[:, None] == jnp.arange(N_EXPERTS, dtype=jnp.int32)[None, :]
    left = jnp.sum(jnp.where(own, (counts + pstarts)[None, :], 0), axis=1) - first
    n_valid = jnp.clip(left, 0, MOE_BLOCK)
    later = jnp.where(block_e[None, :] > block_e[:, None], block_e[None, :], N_EXPERTS)
    next_e = jnp.min(later, axis=1)
    next_e = jnp.where(next_e == N_EXPERTS, -1, next_e)
    present = jnp.any(own, axis=0)
    runs_before = jnp.sum((present[None, :] & (jnp.arange(N_EXPERTS)[None, :] < block_e[:, None])).astype(jnp.int32),
                          axis=1)
    parity = runs_before % 2
    return pos, (block_e, n_valid.astype(jnp.int32), next_e.astype(jnp.int32), parity.astype(jnp.int32)), n_slots


def _expand_rows(pos):
    spread = (jnp.arange(SUB * LANES, dtype=jnp.int32)[None, :] // SUB
              == jnp.arange(LANES, dtype=jnp.int32)[:, None]).astype(F32)
    wide = jnp.dot(pos.reshape(-1, LANES).astype(F32), spread, precision=lax.Precision.HIGHEST)
    wide = wide.astype(jnp.int32) * SUB + (jnp.arange(SUB * LANES, dtype=jnp.int32) % SUB)[None, :]
    return wide.reshape(-1, LANES)


def _group_forward(x, p_l, rope_tab, mix_prm, moe_prm, ple_prm):
    b, s, d = x.shape
    n_tok = b * s
    h1, hn_rows, route, gates, counts = _mixer_call(x, rope_tab, mix_prm)
    pos, block_info, n_slots = _routing(route, counts[:, 0], n_tok)
    slot_rows = _expand_rows(pos)
    xb = _sc_dispatch(hn_rows, slot_rows, n_slots * SUB)
    y = _moe_call(*block_info, xb, *moe_prm)
    gathered = _sc_gather(y, slot_rows)
    out = _combine_call(gathered, h1.reshape(n_tok, d), gates.reshape(n_tok, LANES),
                        p_l.reshape(n_tok, PLE_DIM), *ple_prm)
    return out.reshape(b, s, d)


def kernel(x_prompt, x_sample, p_prompt, p_sample, norm_mix, w_in, q_gain, k_gain, attn_sink, gmlp_v_gain, gmlp_w_s, gmlp_b_s, w_branch, w_out, norm_ffn, w_router, b_router, w_gate_up, b_gate_up, w_down, b_down, norm_ple, w_ple_gate, w_ple_proj):
    depth = norm_mix.shape[0]
    hp, hs = x_prompt, x_sample
    for l in range(depth):
        row = lambda a: a.reshape(1, -1)
        blockdiag = jnp.kron(jnp.eye(N_Q_HEADS, dtype=F32),
                             jnp.full((HEAD_DIM, HEAD_DIM), 1.0 / HEAD_DIM, F32)).astype(BF16)
        sink_heads = attn_sink[l].reshape(N_KV_HEADS, GQA_GROUP)[:, jnp.array(HEAD_ORDER)]
        sink_col = jnp.repeat(sink_heads, ATTN_BLOCK, axis=1)[..., None]
        sink_rows = jnp.where(jnp.arange(LANES) == 0, sink_col, -jnp.inf)
        first_row = (jnp.arange(LANES) == 0)[:, None]
        upper = (jnp.arange(LANES) >= HEAD_DIM)[None, :]
        sink_values = jnp.stack([first_row & upper, first_row & ~upper]).astype(BF16)
        wcat = jnp.transpose(gmlp_w_s[l], (1, 0, 2)).reshape(GMLP_CHUNK, GMLP_GROUPS * GMLP_CHUNK).astype(BF16)
        bias_full = jnp.repeat(gmlp_b_s[l].T, GMLP_GROUP_DIM, axis=1)
        wr = jnp.pad(w_router[l], ((0, 0), (0, ROUTER_PAD - N_EXPERTS)))
        wr_hi = wr.astype(BF16)
        wr_lo = (wr - wr_hi.astype(F32)).astype(BF16)
        br = jnp.pad(b_router[l], (0, ROUTER_PAD - N_EXPERTS)).reshape(1, ROUTER_PAD)
        tri = (jnp.arange(MIX_TILE)[:, None] < jnp.arange(MIX_TILE)[None, :]).astype(BF16)
        mix_prm = (
            row(norm_mix[l]), w_in[l].astype(BF16),
            row(jnp.tile(q_gain[l], N_Q_HEADS)), row(jnp.tile(k_gain[l], N_KV_HEADS)),
            sink_rows, sink_values, row(gmlp_v_gain[l]), blockdiag, wcat, bias_full,
            w_branch[l].astype(BF16), w_out[l].astype(BF16), row(norm_ffn[l]),
            jnp.concatenate([wr_hi, wr_lo], axis=1), br, tri,
        )
        moe_prm = (
            w_gate_up[l], b_gate_up[l].reshape(N_EXPERTS, 1, 2 * D_FF),
            w_down[l], b_down[l].reshape(N_EXPERTS, 1, D_MODEL),
        )
        ple_prm = (row(norm_ple[l]), w_ple_gate[l].astype(BF16), w_ple_proj[l].astype(BF16))
        rope_tab = _rope_table(max(hp.shape[1], hs.shape[1]))
        hp = _group_forward(hp, p_prompt[l], rope_tab, mix_prm, moe_prm, ple_prm)
        hs = _group_forward(hs, p_sample[l], rope_tab, mix_prm, moe_prm, ple_prm)
    return (hp.astype(x_prompt.dtype), hs.astype(x_sample.dtype))
```

```python
import functools

import jax
import jax.numpy as jnp
import numpy as np
from jax import lax
from jax.experimental import pallas as pl
from jax.experimental.pallas import tpu as pltpu
from jax.experimental.pallas import tpu_sc as plsc

D_MODEL = 1024
HEAD_DIM = 64
N_Q_HEADS = 8
N_KV_HEADS = 2
GQA_GROUP = N_Q_HEADS // N_KV_HEADS
HEAD_ORDER = (0, 2, 1, 3)
ATTN_WIDTH = N_Q_HEADS * HEAD_DIM
KV_WIDTH = N_KV_HEADS * HEAD_DIM
WINDOW = 128
ATTN_BLOCK = 128
ROPE_THETA = 500000.0
ROPE_DIM = HEAD_DIM // 4
GMLP_WIDTH = D_MODEL // 2
GMLP_GROUPS = 8
GMLP_GROUP_DIM = GMLP_WIDTH // GMLP_GROUPS
GMLP_CHUNK = 128
N_BRANCH = 2
IN_WIDTH = ATTN_WIDTH + 2 * KV_WIDTH + 2 * GMLP_WIDTH + N_BRANCH * D_MODEL
N_EXPERTS = 32
TOP_K = 4
D_FF = D_MODEL
SWIGLU_ALPHA = 1.702
SWIGLU_LIMIT = 7.0
PLE_DIM = 256
EPS = 1e-6

Q_OFF = 0
K_OFF = ATTN_WIDTH
V_OFF = K_OFF + KV_WIDTH
U_OFF = V_OFF + KV_WIDTH
VG_OFF = U_OFF + GMLP_WIDTH
GL_OFF = VG_OFF + GMLP_WIDTH

LANES = 128
PACK_WORDS = D_MODEL // 2
SUB = PACK_WORDS // LANES
MIX_TILE = 512
PROJ_COLS = 256
MOE_BLOCK = 512
MOE_COLS = 256
CMB_TILE = 512
ROUTER_PAD = 128
SC_CHUNK = 128
DISPATCH_TOKENS = 64
GATHER_STREAMS = 4
VMEM_LIMIT = 56 * 1024 * 1024

BF16 = jnp.bfloat16
F32 = jnp.float32
U32 = jnp.uint32


def _rms(x, gain):
    return x * lax.rsqrt(jnp.mean(x * x, axis=-1, keepdims=True) + EPS) * gain


def _gelu(x):
    return 0.5 * x * (1.0 + lax.erf(x * np.float32(np.sqrt(0.5))))


def _sigmoid(x):
    return 0.5 * jnp.tanh(0.5 * x) + 0.5


def _head_rms(x, blockdiag, gain):
    ms = jnp.dot((x * x).astype(BF16), blockdiag, preferred_element_type=F32)
    return x * lax.rsqrt(ms + EPS) * gain


def _rope(x, cos, sin_lo, sin_hi):
    w = x.shape[-1]
    return x * cos + pltpu.roll(x, w - ROPE_DIM // 2, 1) * sin_lo + pltpu.roll(x, ROPE_DIM // 2, 1) * sin_hi


def _value_blocks(v):
    swapped = pltpu.roll(v, HEAD_DIM, 1)
    low = lax.broadcasted_iota(jnp.int32, v.shape, 1) < HEAD_DIM
    blocks = [jnp.where(low, v, 1.0), jnp.where(low, 1.0, swapped), jnp.where(low, swapped, 1.0), jnp.where(low, 1.0, v)]
    return jnp.concatenate(blocks, axis=1).astype(BF16)


def _tile_lanes(t, reps):
    return t if reps == 1 else jnp.concatenate([t] * reps, axis=-1)


def _store_packed(ref, x):
    rows = x.shape[0]
    hi = lax.bitcast_convert_type(x[:, :PACK_WORDS].astype(BF16).astype(F32), U32)
    lo = lax.bitcast_convert_type(x[:, PACK_WORDS:].astype(BF16).astype(F32), U32)
    words = hi | (lo >> 16)
    for j in range(SUB):
        ref[pl.ds(j, rows, stride=SUB), :] = words[:, j * LANES:(j + 1) * LANES]


def _load_packed(ref, rows):
    words = jnp.concatenate([ref[pl.ds(j, rows, stride=SUB), :] for j in range(SUB)], axis=1)
    left = lax.bitcast_convert_type(words & np.uint32(0xFFFF0000), F32)
    right = lax.bitcast_convert_type(words << 16, F32)
    return jnp.concatenate([left, right], axis=1)


def _mixer_kernel(x_ref, xn_ref, rp_ref, rpn_ref,
                  nmix_ref, win_ref, qg_ref, kg_ref, sink_ref, sinkv_ref, vgain_ref, bdq_ref,
                  wcat_ref, bias_ref, wbr_ref, wout_ref, nffn_ref, wr_ref, br_ref, tri_ref,
                  h1_ref, hn_ref, route_ref, gate_ref, cnt_out_ref,
                  z_ref, q_ref, k_ref, v_ref, attn_ref, gm_ref, cnt_ref, *, n_blocks_seq):
    ts = MIX_TILE
    i = pl.program_id(1)
    x = x_ref[0]

    @pl.when(i > 0)
    def _():
        k_ref[0:ATTN_BLOCK, :] = k_ref[ts:ts + ATTN_BLOCK, :]
        v_ref[0:ATTN_BLOCK, :] = v_ref[ts:ts + ATTN_BLOCK, :]

    @pl.when(i == 0)
    def _():
        k_ref[0:ATTN_BLOCK, :] = jnp.zeros((ATTN_BLOCK, k_ref.shape[1]), BF16)
        v_ref[0:ATTN_BLOCK, :] = jnp.zeros((ATTN_BLOCK, v_ref.shape[1]), BF16)

    xn = _rms(x, nmix_ref[...]).astype(BF16)

    def project(lo, hi):
        z_ref[:, lo:hi] = jnp.dot(xn, win_ref[:, lo:hi], preferred_element_type=F32)

    project(Q_OFF, U_OFF)
    n_items = (ts // ATTN_BLOCK) * N_KV_HEADS
    later = [(U_OFF + t * PROJ_COLS, U_OFF + (t + 1) * PROJ_COLS) for t in range((IN_WIDTH - U_OFF) // PROJ_COLS)]
    per_item = [len(later) // n_items + (1 if n < len(later) % n_items else 0) for n in range(n_items)]

    cos = rp_ref[:, 0:LANES]
    sin_lo = rp_ref[:, LANES:2 * LANES]
    sin_hi = rp_ref[:, 2 * LANES:3 * LANES]
    bdq = bdq_ref[...]
    bdk = bdq_ref[0:KV_WIDTH, 0:KV_WIDTH]

    q = _head_rms(z_ref[:, Q_OFF:Q_OFF + ATTN_WIDTH], bdq, qg_ref[...])
    reps = ATTN_WIDTH // LANES
    q = _rope(q, _tile_lanes(cos, reps), _tile_lanes(sin_lo, reps), _tile_lanes(sin_hi, reps))
    q_ref[...] = (q * (HEAD_DIM ** -0.5)).astype(BF16)

    k = _head_rms(z_ref[:, K_OFF:K_OFF + KV_WIDTH], bdk, kg_ref[...])
    k_ref[ATTN_BLOCK:ATTN_BLOCK + ts, :] = _rope(k, cos, sin_lo, sin_hi).astype(BF16)
    v_ref[ATTN_BLOCK:ATTN_BLOCK + ts, :] = _value_blocks(z_ref[:, V_OFF:V_OFF + KV_WIDTH])

    xhn = _rms(xn_ref[0], nmix_ref[...]).astype(BF16)
    zh = jnp.dot(xhn, win_ref[:, K_OFF:K_OFF + 2 * KV_WIDTH], preferred_element_type=F32)
    kh = _head_rms(zh[:, 0:KV_WIDTH], bdk, kg_ref[...])
    kh = _rope(kh, rpn_ref[:, 0:LANES], rpn_ref[:, LANES:2 * LANES], rpn_ref[:, 2 * LANES:3 * LANES])
    k_ref[ATTN_BLOCK + ts:2 * ATTN_BLOCK + ts, :] = kh.astype(BF16)
    v_ref[ATTN_BLOCK + ts:2 * ATTN_BLOCK + ts, :] = _value_blocks(zh[:, KV_WIDTH:2 * KV_WIDTH])

    rows = GQA_GROUP * ATTN_BLOCK
    keys = 3 * ATTN_BLOCK
    r = lax.broadcasted_iota(jnp.int32, (rows, keys), 0) % ATTN_BLOCK
    c = lax.broadcasted_iota(jnp.int32, (rows, keys), 1)
    band = (c >= r) & (c <= r + 2 * WINDOW)
    low = lax.broadcasted_iota(jnp.int32, (ATTN_BLOCK, 2 * HEAD_DIM), 1) < HEAD_DIM
    pair = 2 * ATTN_BLOCK
    for qb in range(ts // ATTN_BLOCK):
        gb = i * (ts // ATTN_BLOCK) + qb
        lo = jnp.where(gb == 0, ATTN_BLOCK, 0)
        hi = jnp.where(gb == n_blocks_seq - 1, 2 * ATTN_BLOCK, keys)
        valid = band & (c >= lo) & (c < hi)
        r0 = qb * ATTN_BLOCK
        for j in range(N_KV_HEADS):
            for _ in range(per_item[qb * N_KV_HEADS + j]):
                project(*later.pop(0))
            q4 = jnp.concatenate(
                [q_ref[r0:r0 + ATTN_BLOCK, (GQA_GROUP * j + g) * HEAD_DIM:(GQA_GROUP * j + g + 1) * HEAD_DIM]
                 for g in HEAD_ORDER], axis=0)
            kw = k_ref[r0:r0 + keys, j * HEAD_DIM:(j + 1) * HEAD_DIM]
            s = lax.dot_general(q4, kw, (((1,), (1,)), ((), ())), preferred_element_type=F32)
            s = jnp.concatenate([jnp.where(valid, s, -jnp.inf), sink_ref[j]], axis=1)
            p = jnp.exp(s - jnp.max(s, axis=-1, keepdims=True)).astype(BF16)
            outs = []
            for par in range(2):
                vw = jnp.concatenate([v_ref[r0:r0 + keys, (2 * j + par) * LANES:(2 * j + par + 1) * LANES],
                                      sinkv_ref[par]], axis=0)
                o = jnp.dot(p[par * pair:(par + 1) * pair], vw, preferred_element_type=F32)
                outs.append(o / pltpu.roll(o, HEAD_DIM, 1))
            for a in range(GQA_GROUP // 2):
                both = jnp.where(low, outs[0][a * ATTN_BLOCK:(a + 1) * ATTN_BLOCK],
                                 outs[1][a * ATTN_BLOCK:(a + 1) * ATTN_BLOCK])
                h0 = GQA_GROUP * j + 2 * a
                attn_ref[r0:r0 + ATTN_BLOCK, h0 * HEAD_DIM:(h0 + 2) * HEAD_DIM] = both.astype(BF16)

    vgn_all = _head_rms(_gelu(z_ref[:, VG_OFF:VG_OFF + GMLP_WIDTH]), bdq, vgain_ref[...])
    half = GMLP_WIDTH // 2
    gpm = half // GMLP_GROUP_DIM
    lane_grp = lax.broadcasted_iota(jnp.int32, (GMLP_CHUNK, half), 1) // GMLP_GROUP_DIM
    mixed = []
    for ch in range(ts // GMLP_CHUNK):
        c0 = ch * GMLP_CHUNK
        parts = []
        for nt in range(2):
            part = vgn_all[c0:c0 + GMLP_CHUNK, nt * half:(nt + 1) * half]
            vexp = jnp.concatenate(
                [jnp.where(lane_grp == gl, part, 0.0).astype(BF16) for gl in range(gpm)], axis=0)
            wpart = wcat_ref[:, nt * gpm * GMLP_CHUNK:(nt + 1) * gpm * GMLP_CHUNK]
            parts.append(jnp.dot(wpart, vexp, preferred_element_type=F32))
        mixed.append(jnp.concatenate(parts, axis=-1))
    for ch in range(ts // GMLP_CHUNK):
        c0 = ch * GMLP_CHUNK
        u = _gelu(z_ref[c0:c0 + GMLP_CHUNK, U_OFF:U_OFF + GMLP_WIDTH])
        gm_ref[c0:c0 + GMLP_CHUNK, :] = (u * (mixed[ch] + bias_ref[...])).astype(BF16)

    ya = jnp.dot(attn_ref[...], wbr_ref[0], preferred_element_type=F32)
    yg = jnp.dot(gm_ref[...], wbr_ref[1], preferred_element_type=F32)
    ga = _sigmoid(z_ref[:, GL_OFF:GL_OFF + D_MODEL])
    gg = _sigmoid(z_ref[:, GL_OFF + D_MODEL:GL_OFF + 2 * D_MODEL])
    merged = (ga * ya + gg * yg).astype(BF16)
    h1 = x + jnp.dot(merged, wout_ref[...], preferred_element_type=F32)
    h1_ref[0] = h1

    hn = _rms(h1, nffn_ref[...])
    _store_packed(hn_ref, hn)

    hn_hi = hn.astype(BF16)
    hn_lo = (hn - hn_hi.astype(F32)).astype(BF16)
    by_hi = jnp.dot(hn_hi, wr_ref[...], preferred_element_type=F32)
    by_lo = jnp.dot(hn_lo, wr_ref[:, 0:ROUTER_PAD], preferred_element_type=F32)
    logits = by_hi[:, 0:ROUTER_PAD] + by_hi[:, ROUTER_PAD:2 * ROUTER_PAD] + by_lo + br_ref[...]
    lt = jnp.transpose(logits)[0:N_EXPERTS, :]
    eid = lax.broadcasted_iota(jnp.int32, lt.shape, 0)
    vals, ids = [], []
    for _ in range(TOP_K):
        mx = jnp.max(lt, axis=0, keepdims=True)
        am = jnp.min(jnp.where(lt == mx, eid, N_EXPERTS), axis=0, keepdims=True)
        vals.append(mx)
        ids.append(am)
        lt = jnp.where(eid == am, -jnp.inf, lt)
    ex = [jnp.exp(v - vals[0]) for v in vals]
    tot = ex[0] + ex[1] + ex[2] + ex[3]
    g8 = jnp.concatenate([e / tot for e in ex] + [jnp.zeros_like(tot)] * (8 - TOP_K), axis=0)
    gates = jnp.concatenate([g8, jnp.zeros((LANES - 8, ts), F32)], axis=0)
    gate_ref[0] = jnp.transpose(gates)

    @pl.when((pl.program_id(0) == 0) & (i == 0))
    def _():
        cnt_ref[...] = jnp.zeros_like(cnt_ref)

    sel = [eid == a for a in ids]
    member = (sel[0] | sel[1] | sel[2] | sel[3]).astype(F32)
    before = jnp.dot(member.astype(BF16), tri_ref[...], preferred_element_type=F32) + cnt_ref[:, 0:1]
    ranks = [jnp.sum(jnp.where(s_, before, 0.0), axis=0, keepdims=True).astype(jnp.int32) for s_ in sel]
    route_ref[0] = jnp.concatenate(ids + ranks, axis=0)
    cnt_ref[...] = cnt_ref[...] + jnp.sum(member, axis=1, keepdims=True)
    cnt_out_ref[...] = cnt_ref[...]


def _mixer_call(x, rope_tab, prm):
    b, s, d = x.shape
    ts = MIX_TILE
    nt = s // ts
    nb = s // ATTN_BLOCK
    per = ts // ATTN_BLOCK
    const2 = lambda bi, i: (0, 0)
    const3 = lambda bi, i: (0, 0, 0)

    def wspec(arr):
        return pl.BlockSpec(arr.shape, const2 if arr.ndim == 2 else const3)

    in_specs = [
        pl.BlockSpec((1, ts, d), lambda bi, i: (bi, i, 0)),
        pl.BlockSpec((1, ATTN_BLOCK, d), lambda bi, i: (bi, jnp.minimum((i + 1) * per, nb - 1), 0)),
        pl.BlockSpec((ts, 3 * LANES), lambda bi, i: (i, 0)),
        pl.BlockSpec((ATTN_BLOCK, 3 * LANES), lambda bi, i: (jnp.minimum((i + 1) * per, nb - 1), 0)),
    ] + [wspec(a) for a in prm]
    out_shape = [
        jax.ShapeDtypeStruct((b, s, d), F32),
        jax.ShapeDtypeStruct((b * s * SUB, LANES), U32),
        jax.ShapeDtypeStruct((b, 2 * TOP_K, s), jnp.int32),
        jax.ShapeDtypeStruct((b, s, LANES), F32),
        jax.ShapeDtypeStruct((N_EXPERTS, LANES), F32),
    ]
    out_specs = [
        pl.BlockSpec((1, ts, d), lambda bi, i: (bi, i, 0)),
        pl.BlockSpec((ts * SUB, LANES), lambda bi, i: (bi * nt + i, 0)),
        pl.BlockSpec((1, 2 * TOP_K, ts), lambda bi, i: (bi, 0, i)),
        pl.BlockSpec((1, ts, LANES), lambda bi, i: (bi, i, 0)),
        pl.BlockSpec((N_EXPERTS, LANES), const2),
    ]
    scratch = [
        pltpu.VMEM((ts, IN_WIDTH), F32),
        pltpu.VMEM((ts, ATTN_WIDTH), BF16),
        pltpu.VMEM((ts + 2 * ATTN_BLOCK, KV_WIDTH), BF16),
        pltpu.VMEM((ts + 2 * ATTN_BLOCK, 2 * N_KV_HEADS * LANES), BF16),
        pltpu.VMEM((ts, ATTN_WIDTH), BF16),
        pltpu.VMEM((ts, GMLP_WIDTH), BF16),
        pltpu.VMEM((N_EXPERTS, LANES), F32),
    ]
    return pl.pallas_call(
        functools.partial(_mixer_kernel, n_blocks_seq=nb),
        grid=(b, nt),
        in_specs=in_specs,
        out_specs=out_specs,
        out_shape=out_shape,
        scratch_shapes=scratch,
        compiler_params=pltpu.CompilerParams(
            dimension_semantics=("arbitrary", "arbitrary"), vmem_limit_bytes=VMEM_LIMIT),
        name="mixer",
    )(x, x, rope_tab, rope_tab, *prm)


def _sc_workers():
    info = plsc.get_sparse_core_info()
    return info.num_cores, info.num_cores * info.num_subcores


def _sc_dispatch(rows2d, slot_rows, n_out_rows):
    n_cores, n_workers = _sc_workers()
    chunk_rows = DISPATCH_TOKENS * SUB
    halves = chunk_rows // SC_CHUNK
    n_chunks = rows2d.shape[0] // chunk_rows
    per_w = n_chunks // n_workers
    idx_rows_per_choice = rows2d.shape[0] // SC_CHUNK
    mesh = plsc.VectorSubcoreMesh(core_axis_name="c", subcore_axis_name="s")

    @functools.partial(
        pl.kernel, mesh=mesh,
        out_type=jax.ShapeDtypeStruct((n_out_rows, LANES), rows2d.dtype),
        scratch_types=[pltpu.VMEM((TOP_K * halves, SC_CHUNK), jnp.int32),
                       pltpu.VMEM((chunk_rows, LANES), rows2d.dtype),
                       pltpu.SemaphoreType.DMA, pltpu.SemaphoreType.DMA],
    )
    def k(src_hbm, idx_hbm, out_hbm, idx_v, rows_v, sem_in, sem_out):
        wid = lax.axis_index("s") * n_cores + lax.axis_index("c")

        @pl.loop(0, per_w)
        def _(j):
            c = wid * per_w + j
            loads = [pltpu.async_copy(src_hbm.at[pl.ds(c * chunk_rows, chunk_rows)], rows_v, sem_in)]
            loads += [
                pltpu.async_copy(idx_hbm.at[pl.ds(kk * idx_rows_per_choice + c * halves, halves)],
                                 idx_v.at[pl.ds(kk * halves, halves)], sem_in)
                for kk in range(TOP_K)]
            for cp in loads:
                cp.wait()
            copies = [
                pltpu.async_copy(rows_v.at[pl.ds((q % halves) * SC_CHUNK, SC_CHUNK)],
                                 out_hbm.at[idx_v.at[q]], sem_out)
                for q in range(TOP_K * halves)]
            for cp in copies:
                cp.wait()

    return k(rows2d, slot_rows)


def _sc_gather(table2d, idx2d):
    n_cores, n_workers = _sc_workers()
    n = idx2d.shape[0] * SC_CHUNK
    per_w = n // n_workers
    step_rows = GATHER_STREAMS * SC_CHUNK
    idx_tile = 8
    mesh = plsc.VectorSubcoreMesh(core_axis_name="c", subcore_axis_name="s")

    @functools.partial(
        pl.kernel, mesh=mesh,
        out_type=jax.ShapeDtypeStruct((n, LANES), table2d.dtype),
        scratch_types=[pltpu.VMEM((idx_tile, SC_CHUNK), jnp.int32),
                       pltpu.VMEM((step_rows, LANES), table2d.dtype),
                       pltpu.SemaphoreType.DMA],
    )
    def k(table_hbm, idx_hbm, out_hbm, idx_v, rows_v, sem):
        wid = lax.axis_index("s") * n_cores + lax.axis_index("c")

        @pl.loop(0, per_w // (idx_tile * SC_CHUNK))
        def _(j):
            row0 = pl.multiple_of(wid * (per_w // SC_CHUNK) + j * idx_tile, idx_tile)
            pltpu.sync_copy(idx_hbm.at[pl.ds(row0, idx_tile)], idx_v)
            for part in range(idx_tile // GATHER_STREAMS):
                copies = [pltpu.async_copy(table_hbm.at[idx_v.at[part * GATHER_STREAMS + q]],
                                           rows_v.at[pl.ds(q * SC_CHUNK, SC_CHUNK)], sem)
                          for q in range(GATHER_STREAMS)]
                for cp in copies:
                    cp.wait()
                pltpu.sync_copy(rows_v, out_hbm.at[pl.ds((row0 + part * GATHER_STREAMS) * SC_CHUNK, step_rows)])

    return k(table2d, idx2d)


def _expert_weight_copies(wgu_hbm, wd_hbm, wgu_buf, wd_buf, sem, expert, slot):
    return (pltpu.make_async_copy(wgu_hbm.at[expert], wgu_buf.at[slot], sem.at[0, slot]),
            pltpu.make_async_copy(wd_hbm.at[expert], wd_buf.at[slot], sem.at[1, slot]))


def _moe_kernel(be_ref, nv_ref, nxt_ref, par_ref, x_ref, wgu_hbm, bgu_ref, wd_hbm, bd_ref, y_ref,
                wgu_buf, wd_buf, sem, wgu_s, wd_s, slab):
    i = pl.program_id(0)
    nv = nv_ref[i]
    slot = par_ref[i]

    cw = MOE_COLS
    n_chunks = D_FF // cw
    copies = functools.partial(_expert_weight_copies, wgu_hbm, wd_hbm, wgu_buf, wd_buf, sem)

    @pl.when(i == 0)
    def _():
        for cp in copies(be_ref[0], slot):
            cp.start()

    @pl.when((i == 0) | (be_ref[i] != be_ref[jnp.maximum(i - 1, 0)]))
    def _():
        for cp in copies(be_ref[i], slot):
            cp.wait()

        @pl.when(nxt_ref[i] >= 0)
        def _():
            for cp in copies(nxt_ref[i], 1 - slot):
                cp.start()

        for c in range(n_chunks):
            wgu_s[:, 2 * c * cw:(2 * c + 1) * cw] = wgu_buf[slot, :, c * cw:(c + 1) * cw].astype(BF16)
            wgu_s[:, (2 * c + 1) * cw:(2 * c + 2) * cw] = (
                wgu_buf[slot, :, D_FF + c * cw:D_FF + (c + 1) * cw].astype(BF16))
        half = D_FF // 2
        for c in range(D_MODEL // LANES):
            slab[c, pl.ds(0, half, stride=2), :] = wd_buf[slot, 0:half, c * LANES:(c + 1) * LANES]
            slab[c, pl.ds(1, half, stride=2), :] = wd_buf[slot, half:D_FF, c * LANES:(c + 1) * LANES]
        for c in range(D_MODEL // LANES):
            wd_s[:, c * LANES:(c + 1) * LANES] = slab[c].astype(BF16)

    expert = be_ref[i]

    def run_expert(rows):
        x = _load_packed(x_ref, rows)
        live = lax.broadcasted_iota(jnp.int32, x.shape, 0) < nv
        xe = jnp.where(live, x, 0.0).astype(BF16)
        even = (lax.broadcasted_iota(jnp.int32, (rows, cw), 1) % 2) == 0
        y = bd_ref[expert]

        def gate_up(c):
            return jnp.dot(xe, wgu_s[:, 2 * c * cw:(2 * c + 2) * cw], preferred_element_type=F32)

        h_next = gate_up(0)
        for c in range(n_chunks):
            h = h_next
            if c + 1 < n_chunks:
                h_next = gate_up(c + 1)
            h_a = h[:, 0:cw] + bgu_ref[expert, :, c * cw:(c + 1) * cw]
            h_b = h[:, cw:2 * cw] + bgu_ref[expert, :, D_FF + c * cw:D_FF + (c + 1) * cw]
            gate = jnp.where(even, h_a, pltpu.roll(h_b, 1, 1))
            up = jnp.where(even, pltpu.roll(h_a, cw - 1, 1), h_b)
            gate = jnp.minimum(gate, SWIGLU_LIMIT)
            up = jnp.clip(up, -SWIGLU_LIMIT, SWIGLU_LIMIT)
            act = (up + 1.0) * (gate * jax.nn.sigmoid(SWIGLU_ALPHA * gate))
            y = y + jnp.dot(act.astype(BF16), wd_s[c * cw:(c + 1) * cw, :], preferred_element_type=F32)
        _store_packed(y_ref, y)

    @pl.when(nv > MOE_BLOCK // 2)
    def _():
        run_expert(MOE_BLOCK)

    @pl.when((nv > 0) & (nv <= MOE_BLOCK // 2))
    def _():
        run_expert(MOE_BLOCK // 2)


def _moe_call(block_e, n_valid, next_e, parity, xb, wgu, bgu, wd, bd):
    n_blocks = block_e.shape[0]
    d = D_MODEL
    grid_spec = pltpu.PrefetchScalarGridSpec(
        num_scalar_prefetch=4,
        grid=(n_blocks,),
        in_specs=[
            pl.BlockSpec((MOE_BLOCK * SUB, LANES), lambda i, be, nv, nx, pr: (i, 0)),
            pl.BlockSpec(memory_space=pl.ANY),
            pl.BlockSpec((N_EXPERTS, 1, 2 * D_FF), lambda i, be, nv, nx, pr: (0, 0, 0)),
            pl.BlockSpec(memory_space=pl.ANY),
            pl.BlockSpec((N_EXPERTS, 1, d), lambda i, be, nv, nx, pr: (0, 0, 0)),
        ],
        out_specs=pl.BlockSpec((MOE_BLOCK * SUB, LANES), lambda i, be, nv, nx, pr: (i, 0)),
        scratch_shapes=[pltpu.VMEM((2, d, 2 * D_FF), F32), pltpu.VMEM((2, D_FF, d), F32),
                        pltpu.SemaphoreType.DMA((2, 2)),
                        pltpu.VMEM((d, 2 * D_FF), BF16), pltpu.VMEM((D_FF, d), BF16),
                        pltpu.VMEM((d // LANES, D_FF, LANES), F32)],
    )
    return pl.pallas_call(
        _moe_kernel,
        grid_spec=grid_spec,
        out_shape=jax.ShapeDtypeStruct((n_blocks * MOE_BLOCK * SUB, LANES), U32),
        compiler_params=pltpu.CompilerParams(
            dimension_semantics=("arbitrary",), vmem_limit_bytes=VMEM_LIMIT),
        name="moe",
    )(block_e, n_valid, next_e, parity, xb, wgu, bgu, wd, bd)


def _combine_kernel(g0_ref, g1_ref, g2_ref, g3_ref, h1_ref, gate_ref, p_ref, nple_ref, wpg_ref, wpp_ref, o_ref):
    g = gate_ref[...]
    moe = jnp.zeros((CMB_TILE, D_MODEL), F32)
    for kk, g_ref in enumerate((g0_ref, g1_ref, g2_ref, g3_ref)):
        moe = moe + _load_packed(g_ref, CMB_TILE) * g[:, kk:kk + 1]
    h2 = h1_ref[...] + moe
    hp = _rms(h2, nple_ref[...]).astype(BF16)
    gate = jax.nn.sigmoid(jnp.dot(hp, wpg_ref[...], preferred_element_type=F32))
    proj = jnp.dot(p_ref[...].astype(BF16), wpp_ref[...], preferred_element_type=F32)
    o_ref[...] = h2 + gate * proj


def _combine_call(gathered, h1_flat, gates, p_flat, nple, wpg, wpp):
    n_tok, d = h1_flat.shape
    tc = CMB_TILE
    n_tiles = n_tok // tc
    row = lambda i: (i, 0)
    const = lambda i: (0, 0)
    g_specs = [pl.BlockSpec((tc * SUB, LANES), functools.partial(lambda i, kk: (kk * n_tiles + i, 0), kk=kk))
               for kk in range(TOP_K)]
    return pl.pallas_call(
        _combine_kernel,
        grid=(n_tiles,),
        in_specs=g_specs + [
            pl.BlockSpec((tc, d), row),
            pl.BlockSpec((tc, LANES), row),
            pl.BlockSpec((tc, PLE_DIM), row),
            pl.BlockSpec((1, d), const),
            pl.BlockSpec((d, d), const),
            pl.BlockSpec((PLE_DIM, d), const),
        ],
        out_specs=pl.BlockSpec((tc, d), row),
        out_shape=jax.ShapeDtypeStruct((n_tok, d), F32),
        compiler_params=pltpu.CompilerParams(
            dimension_semantics=("arbitrary",), vmem_limit_bytes=VMEM_LIMIT),
        name="combine",
    )(gathered, gathered, gathered, gathered, h1_flat, gates, p_flat, nple, wpg, wpp)


def _rope_table(s):
    half = ROPE_DIM // 2
    inv_freq = jnp.power(ROPE_THETA, -jnp.arange(half, dtype=F32) * (2.0 / ROPE_DIM))
    ang = jnp.arange(s, dtype=F32)[:, None] * inv_freq[None, :]
    cos, sin = jnp.cos(ang), jnp.sin(ang)
    pad1 = jnp.ones((s, HEAD_DIM - ROPE_DIM), F32)
    pad0 = jnp.zeros((s, HEAD_DIM - half), F32)
    c = jnp.concatenate([cos, cos, pad1], axis=-1)
    s_lo = jnp.concatenate([-sin, pad0], axis=-1)
    s_hi = jnp.concatenate([jnp.zeros((s, half), F32), sin, pad0[:, half:]], axis=-1)
    two = lambda t: jnp.concatenate([t, t], axis=-1)
    return jnp.concatenate([two(c), two(s_lo), two(s_hi)], axis=-1)


def _routing(route, counts, n_tok):
    ids = jnp.transpose(route[:, 0:TOP_K, :], (1, 0, 2)).reshape(TOP_K, n_tok)
    ranks = jnp.transpose(route[:, TOP_K:2 * TOP_K, :], (1, 0, 2)).reshape(TOP_K, n_tok)
    counts = counts.astype(jnp.int32)
    padded = (counts + MOE_BLOCK - 1) // MOE_BLOCK * MOE_BLOCK
    pends = jnp.cumsum(padded)
    pstarts = pends - padded
    pos = ranks
    for e in range(N_EXPERTS):
        pos = pos + jnp.where(ids == e, pstarts[e], 0)
    n_slots = -(-(n_tok * TOP_K) // MOE_BLOCK) * MOE_BLOCK + N_EXPERTS * MOE_BLOCK
    n_blocks = n_slots // MOE_BLOCK
    first = jnp.arange(n_blocks, dtype=jnp.int32) * MOE_BLOCK
    block_e = jnp.clip(jnp.sum((first[:, None] >= pends[None, :]).astype(jnp.int32), axis=1), 0, N_EXPERTS - 1)
    own = block_e[:, None] == jnp.arange(N_EXPERTS, dtype=jnp.int32)[None, :]
    left = jnp.sum(jnp.where(own, (counts + pstarts)[None, :], 0), axis=1) - first
    n_valid = jnp.clip(left, 0, MOE_BLOCK)
    later = jnp.where(block_e[None, :] > block_e[:, None], block_e[None, :], N_EXPERTS)
    next_e = jnp.min(later, axis=1)
    next_e = jnp.where(next_e == N_EXPERTS, -1, next_e)
    present = jnp.any(own, axis=0)
    runs_before = jnp.sum((present[None, :] & (jnp.arange(N_EXPERTS)[None, :] < block_e[:, None])).astype(jnp.int32),
                          axis=1)
    parity = runs_before % 2
    return pos, (block_e, n_valid.astype(jnp.int32), next_e.astype(jnp.int32), parity.astype(jnp.int32)), n_slots


def _expand_rows(pos):
    spread = (jnp.arange(SUB * LANES, dtype=jnp.int32)[None, :] // SUB
              == jnp.arange(LANES, dtype=jnp.int32)[:, None]).astype(F32)
    wide = jnp.dot(pos.reshape(-1, LANES).astype(F32), spread, precision=lax.Precision.HIGHEST)
    wide = wide.astype(jnp.int32) * SUB + (jnp.arange(SUB * LANES, dtype=jnp.int32) % SUB)[None, :]
    return wide.reshape(-1, LANES)


def _group_forward(x, p_l, rope_tab, mix_prm, moe_prm, ple_prm):
    b, s, d = x.shape
    n_tok = b * s
    h1, hn_rows, route, gates, counts = _mixer_call(x, rope_tab, mix_prm)
    pos, block_info, n_slots = _routing(route, counts[:, 0], n_tok)
    slot_rows = _expand_rows(pos)
    xb = _sc_dispatch(hn_rows, slot_rows, n_slots * SUB)
    y = _moe_call(*block_info, xb, *moe_prm)
    gathered = _sc_gather(y, slot_rows)
    out = _combine_call(gathered, h1.reshape(n_tok, d), gates.reshape(n_tok, LANES),
                        p_l.reshape(n_tok, PLE_DIM), *ple_prm)
    return out.reshape(b, s, d)


def kernel(x_prompt, x_sample, p_prompt, p_sample, norm_mix, w_in, q_gain, k_gain, attn_sink, gmlp_v_gain, gmlp_w_s, gmlp_b_s, w_branch, w_out, norm_ffn, w_router, b_router, w_gate_up, b_gate_up, w_down, b_down, norm_ple, w_ple_gate, w_ple_proj):
    depth = norm_mix.shape[0]
    hp, hs = x_prompt, x_sample
    for l in range(depth):
        row = lambda a: a.reshape(1, -1)
        blockdiag = jnp.kron(jnp.eye(N_Q_HEADS, dtype=F32),
                             jnp.full((HEAD_DIM, HEAD_DIM), 1.0 / HEAD_DIM, F32)).astype(BF16)
        sink_heads = attn_sink[l].reshape(N_KV_HEADS, GQA_GROUP)[:, jnp.array(HEAD_ORDER)]
        sink_col = jnp.repeat(sink_heads, ATTN_BLOCK, axis=1)[..., None]
        sink_rows = jnp.where(jnp.arange(LANES) == 0, sink_col, -jnp.inf)
        first_row = (jnp.arange(LANES) == 0)[:, None]
        upper = (jnp.arange(LANES) >= HEAD_DIM)[None, :]
        sink_values = jnp.stack([first_row & upper, first_row & ~upper]).astype(BF16)
        wcat = jnp.transpose(gmlp_w_s[l], (1, 0, 2)).reshape(GMLP_CHUNK, GMLP_GROUPS * GMLP_CHUNK).astype(BF16)
        bias_full = jnp.repeat(gmlp_b_s[l].T, GMLP_GROUP_DIM, axis=1)
        wr = jnp.pad(w_router[l], ((0, 0), (0, ROUTER_PAD - N_EXPERTS)))
        wr_hi = wr.astype(BF16)
        wr_lo = (wr - wr_hi.astype(F32)).astype(BF16)
        br = jnp.pad(b_router[l], (0, ROUTER_PAD - N_EXPERTS)).reshape(1, ROUTER_PAD)
        tri = (jnp.arange(MIX_TILE)[:, None] < jnp.arange(MIX_TILE)[None, :]).astype(BF16)
        mix_prm = (
            row(norm_mix[l]), w_in[l].astype(BF16),
            row(jnp.tile(q_gain[l], N_Q_HEADS)), row(jnp.tile(k_gain[l], N_KV_HEADS)),
            sink_rows, sink_values, row(gmlp_v_gain[l]), blockdiag, wcat, bias_full,
            w_branch[l].astype(BF16), w_out[l].astype(BF16), row(norm_ffn[l]),
            jnp.concatenate([wr_hi, wr_lo], axis=1), br, tri,
        )
        moe_prm = (
            w_gate_up[l], b_gate_up[l].reshape(N_EXPERTS, 1, 2 * D_FF),
            w_down[l], b_down[l].reshape(N_EXPERTS, 1, D_MODEL),
        )
        ple_prm = (row(norm_ple[l]), w_ple_gate[l].astype(BF16), w_ple_proj[l].astype(BF16))
        rope_tab = _rope_table(max(hp.shape[1], hs.shape[1]))
        hp = _group_forward(hp, p_prompt[l], rope_tab, mix_prm, moe_prm, ple_prm)
        hs = _group_forward(hs, p_sample[l], rope_tab, mix_prm, moe_prm, ple_prm)
    return (hp.astype(x_prompt.dtype), hs.astype(x_sample.dtype))
```

```python
import functools

import jax
import jax.numpy as jnp
import numpy as np
from jax import lax
from jax.experimental import pallas as pl
from jax.experimental.pallas import tpu as pltpu
from jax.experimental.pallas import tpu_sc as plsc

D_MODEL = 1024
HEAD_DIM = 64
N_Q_HEADS = 8
N_KV_HEADS = 2
GQA_GROUP = N_Q_HEADS // N_KV_HEADS
HEAD_ORDER = (0, 2, 1, 3)
ATTN_WIDTH = N_Q_HEADS * HEAD_DIM
KV_WIDTH = N_KV_HEADS * HEAD_DIM
WINDOW = 128
ATTN_BLOCK = 128
ROPE_THETA = 500000.0
ROPE_DIM = HEAD_DIM // 4
GMLP_WIDTH = D_MODEL // 2
GMLP_GROUPS = 8
GMLP_GROUP_DIM = GMLP_WIDTH // GMLP_GROUPS
GMLP_CHUNK = 128
N_BRANCH = 2
IN_WIDTH = ATTN_WIDTH + 2 * KV_WIDTH + 2 * GMLP_WIDTH + N_BRANCH * D_MODEL
N_EXPERTS = 32
TOP_K = 4
D_FF = D_MODEL
SWIGLU_ALPHA = 1.702
SWIGLU_LIMIT = 7.0
PLE_DIM = 256
EPS = 1e-6

Q_OFF = 0
K_OFF = ATTN_WIDTH
V_OFF = K_OFF + KV_WIDTH
U_OFF = V_OFF + KV_WIDTH
VG_OFF = U_OFF + GMLP_WIDTH
GL_OFF = VG_OFF + GMLP_WIDTH

LANES = 128
PACK_WORDS = D_MODEL // 2
SUB = PACK_WORDS // LANES
MIX_TILE = 512
PROJ_COLS = 256
MOE_BLOCK = 512
MOE_COLS = 512
CMB_TILE = 512
ROUTER_PAD = 128
SC_CHUNK = 128
DISPATCH_TOKENS = 64
GATHER_STREAMS = 4
VMEM_LIMIT = 56 * 1024 * 1024

BF16 = jnp.bfloat16
F32 = jnp.float32
U32 = jnp.uint32


def _rms(x, gain):
    return x * lax.rsqrt(jnp.mean(x * x, axis=-1, keepdims=True) + EPS) * gain


def _gelu(x):
    return 0.5 * x * (1.0 + lax.erf(x * np.float32(np.sqrt(0.5))))


def _sigmoid(x):
    return 0.5 * jnp.tanh(0.5 * x) + 0.5


def _head_rms(x, blockdiag, gain):
    ms = jnp.dot((x * x).astype(BF16), blockdiag, preferred_element_type=F32)
    return x * lax.rsqrt(ms + EPS) * gain


def _rope(x, cos, sin_lo, sin_hi):
    w = x.shape[-1]
    return x * cos + pltpu.roll(x, w - ROPE_DIM // 2, 1) * sin_lo + pltpu.roll(x, ROPE_DIM // 2, 1) * sin_hi


def _value_blocks(v):
    swapped = pltpu.roll(v, HEAD_DIM, 1)
    low = lax.broadcasted_iota(jnp.int32, v.shape, 1) < HEAD_DIM
    blocks = [jnp.where(low, v, 1.0), jnp.where(low, 1.0, swapped), jnp.where(low, swapped, 1.0), jnp.where(low, 1.0, v)]
    return jnp.concatenate(blocks, axis=1).astype(BF16)


def _tile_lanes(t, reps):
    return t if reps == 1 else jnp.concatenate([t] * reps, axis=-1)


def _store_packed(ref, x):
    rows = x.shape[0]
    hi = lax.bitcast_convert_type(x[:, :PACK_WORDS].astype(BF16).astype(F32), U32)
    lo = lax.bitcast_convert_type(x[:, PACK_WORDS:].astype(BF16).astype(F32), U32)
    words = hi | (lo >> 16)
    for j in range(SUB):
        ref[pl.ds(j, rows, stride=SUB), :] = words[:, j * LANES:(j + 1) * LANES]


def _load_packed(ref, rows):
    words = jnp.concatenate([ref[pl.ds(j, rows, stride=SUB), :] for j in range(SUB)], axis=1)
    left = lax.bitcast_convert_type(words & np.uint32(0xFFFF0000), F32)
    right = lax.bitcast_convert_type(words << 16, F32)
    return jnp.concatenate([left, right], axis=1)


def _mixer_kernel(x_ref, xn_ref, rp_ref, rpn_ref,
                  nmix_ref, win_ref, qg_ref, kg_ref, sink_ref, sinkv_ref, vgain_ref, bdq_ref,
                  wcat_ref, bias_ref, wbr_ref, wout_ref, nffn_ref, wr_ref, br_ref, tri_ref,
                  h1_ref, hn_ref, route_ref, gate_ref, cnt_out_ref,
                  z_ref, q_ref, k_ref, v_ref, attn_ref, gm_ref, cnt_ref, *, n_blocks_seq):
    ts = MIX_TILE
    i = pl.program_id(1)
    x = x_ref[0]

    @pl.when(i > 0)
    def _():
        k_ref[0:ATTN_BLOCK, :] = k_ref[ts:ts + ATTN_BLOCK, :]
        v_ref[0:ATTN_BLOCK, :] = v_ref[ts:ts + ATTN_BLOCK, :]

    @pl.when(i == 0)
    def _():
        k_ref[0:ATTN_BLOCK, :] = jnp.zeros((ATTN_BLOCK, k_ref.shape[1]), BF16)
        v_ref[0:ATTN_BLOCK, :] = jnp.zeros((ATTN_BLOCK, v_ref.shape[1]), BF16)

    xn = _rms(x, nmix_ref[...]).astype(BF16)

    def project(lo, hi):
        z_ref[:, lo:hi] = jnp.dot(xn, win_ref[:, lo:hi], preferred_element_type=F32)

    project(Q_OFF, U_OFF)
    n_items = (ts // ATTN_BLOCK) * N_KV_HEADS
    later = [(U_OFF + t * PROJ_COLS, U_OFF + (t + 1) * PROJ_COLS) for t in range((IN_WIDTH - U_OFF) // PROJ_COLS)]
    per_item = [len(later) // n_items + (1 if n < len(later) % n_items else 0) for n in range(n_items)]

    cos = rp_ref[:, 0:LANES]
    sin_lo = rp_ref[:, LANES:2 * LANES]
    sin_hi = rp_ref[:, 2 * LANES:3 * LANES]
    bdq = bdq_ref[...]
    bdk = bdq_ref[0:KV_WIDTH, 0:KV_WIDTH]

    q = _head_rms(z_ref[:, Q_OFF:Q_OFF + ATTN_WIDTH], bdq, qg_ref[...])
    reps = ATTN_WIDTH // LANES
    q = _rope(q, _tile_lanes(cos, reps), _tile_lanes(sin_lo, reps), _tile_lanes(sin_hi, reps))
    q_ref[...] = (q * (HEAD_DIM ** -0.5)).astype(BF16)

    k = _head_rms(z_ref[:, K_OFF:K_OFF + KV_WIDTH], bdk, kg_ref[...])
    k_ref[ATTN_BLOCK:ATTN_BLOCK + ts, :] = _rope(k, cos, sin_lo, sin_hi).astype(BF16)
    v_ref[ATTN_BLOCK:ATTN_BLOCK + ts, :] = _value_blocks(z_ref[:, V_OFF:V_OFF + KV_WIDTH])

    xhn = _rms(xn_ref[0], nmix_ref[...]).astype(BF16)
    zh = jnp.dot(xhn, win_ref[:, K_OFF:K_OFF + 2 * KV_WIDTH], preferred_element_type=F32)
    kh = _head_rms(zh[:, 0:KV_WIDTH], bdk, kg_ref[...])
    kh = _rope(kh, rpn_ref[:, 0:LANES], rpn_ref[:, LANES:2 * LANES], rpn_ref[:, 2 * LANES:3 * LANES])
    k_ref[ATTN_BLOCK + ts:2 * ATTN_BLOCK + ts, :] = kh.astype(BF16)
    v_ref[ATTN_BLOCK + ts:2 * ATTN_BLOCK + ts, :] = _value_blocks(zh[:, KV_WIDTH:2 * KV_WIDTH])

    rows = GQA_GROUP * ATTN_BLOCK
    keys = 3 * ATTN_BLOCK
    r = lax.broadcasted_iota(jnp.int32, (rows, keys), 0) % ATTN_BLOCK
    c = lax.broadcasted_iota(jnp.int32, (rows, keys), 1)
    band = (c >= r) & (c <= r + 2 * WINDOW)
    low = lax.broadcasted_iota(jnp.int32, (ATTN_BLOCK, 2 * HEAD_DIM), 1) < HEAD_DIM
    pair = 2 * ATTN_BLOCK
    for qb in range(ts // ATTN_BLOCK):
        gb = i * (ts // ATTN_BLOCK) + qb
        lo = jnp.where(gb == 0, ATTN_BLOCK, 0)
        hi = jnp.where(gb == n_blocks_seq - 1, 2 * ATTN_BLOCK, keys)
        valid = band & (c >= lo) & (c < hi)
        r0 = qb * ATTN_BLOCK
        for j in range(N_KV_HEADS):
            for _ in range(per_item[qb * N_KV_HEADS + j]):
                project(*later.pop(0))
            q4 = jnp.concatenate(
                [q_ref[r0:r0 + ATTN_BLOCK, (GQA_GROUP * j + g) * HEAD_DIM:(GQA_GROUP * j + g + 1) * HEAD_DIM]
                 for g in HEAD_ORDER], axis=0)
            kw = k_ref[r0:r0 + keys, j * HEAD_DIM:(j + 1) * HEAD_DIM]
            s = lax.dot_general(q4, kw, (((1,), (1,)), ((), ())), preferred_element_type=F32)
            s = jnp.concatenate([jnp.where(valid, s, -jnp.inf), sink_ref[j]], axis=1)
            p = jnp.exp(s - jnp.max(s, axis=-1, keepdims=True)).astype(BF16)
            outs = []
            for par in range(2):
                vw = jnp.concatenate([v_ref[r0:r0 + keys, (2 * j + par) * LANES:(2 * j + par + 1) * LANES],
                                      sinkv_ref[par]], axis=0)
                o = jnp.dot(p[par * pair:(par + 1) * pair], vw, preferred_element_type=F32)
                outs.append(o / pltpu.roll(o, HEAD_DIM, 1))
            for a in range(GQA_GROUP // 2):
                both = jnp.where(low, outs[0][a * ATTN_BLOCK:(a + 1) * ATTN_BLOCK],
                                 outs[1][a * ATTN_BLOCK:(a + 1) * ATTN_BLOCK])
                h0 = GQA_GROUP * j + 2 * a
                attn_ref[r0:r0 + ATTN_BLOCK, h0 * HEAD_DIM:(h0 + 2) * HEAD_DIM] = both.astype(BF16)

    vgn_all = _head_rms(_gelu(z_ref[:, VG_OFF:VG_OFF + GMLP_WIDTH]), bdq, vgain_ref[...])
    half = GMLP_WIDTH // 2
    gpm = half // GMLP_GROUP_DIM
    lane_grp = lax.broadcasted_iota(jnp.int32, (GMLP_CHUNK, half), 1) // GMLP_GROUP_DIM
    mixed = []
    for ch in range(ts // GMLP_CHUNK):
        c0 = ch * GMLP_CHUNK
        parts = []
        for nt in range(2):
            part = vgn_all[c0:c0 + GMLP_CHUNK, nt * half:(nt + 1) * half]
            vexp = jnp.concatenate(
                [jnp.where(lane_grp == gl, part, 0.0).astype(BF16) for gl in range(gpm)], axis=0)
            wpart = wcat_ref[:, nt * gpm * GMLP_CHUNK:(nt + 1) * gpm * GMLP_CHUNK]
            parts.append(jnp.dot(wpart, vexp, preferred_element_type=F32))
        mixed.append(jnp.concatenate(parts, axis=-1))
    for ch in range(ts // GMLP_CHUNK):
        c0 = ch * GMLP_CHUNK
        u = _gelu(z_ref[c0:c0 + GMLP_CHUNK, U_OFF:U_OFF + GMLP_WIDTH])
        gm_ref[c0:c0 + GMLP_CHUNK, :] = (u * (mixed[ch] + bias_ref[...])).astype(BF16)

    ya = jnp.dot(attn_ref[...], wbr_ref[0], preferred_element_type=F32)
    yg = jnp.dot(gm_ref[...], wbr_ref[1], preferred_element_type=F32)
    ga = _sigmoid(z_ref[:, GL_OFF:GL_OFF + D_MODEL])
    gg = _sigmoid(z_ref[:, GL_OFF + D_MODEL:GL_OFF + 2 * D_MODEL])
    merged = (ga * ya + gg * yg).astype(BF16)
    h1 = x + jnp.dot(merged, wout_ref[...], preferred_element_type=F32)
    h1_ref[0] = h1

    hn = _rms(h1, nffn_ref[...])
    _store_packed(hn_ref, hn)

    hn_hi = hn.astype(BF16)
    hn_lo = (hn - hn_hi.astype(F32)).astype(BF16)
    by_hi = jnp.dot(hn_hi, wr_ref[...], preferred_element_type=F32)
    by_lo = jnp.dot(hn_lo, wr_ref[:, 0:ROUTER_PAD], preferred_element_type=F32)
    logits = by_hi[:, 0:ROUTER_PAD] + by_hi[:, ROUTER_PAD:2 * ROUTER_PAD] + by_lo + br_ref[...]
    lt = jnp.transpose(logits)[0:N_EXPERTS, :]
    eid = lax.broadcasted_iota(jnp.int32, lt.shape, 0)
    vals, ids = [], []
    for _ in range(TOP_K):
        mx = jnp.max(lt, axis=0, keepdims=True)
        am = jnp.min(jnp.where(lt == mx, eid, N_EXPERTS), axis=0, keepdims=True)
        vals.append(mx)
        ids.append(am)
        lt = jnp.where(eid == am, -jnp.inf, lt)
    ex = [jnp.exp(v - vals[0]) for v in vals]
    tot = ex[0] + ex[1] + ex[2] + ex[3]
    g8 = jnp.concatenate([e / tot for e in ex] + [jnp.zeros_like(tot)] * (8 - TOP_K), axis=0)
    gates = jnp.concatenate([g8, jnp.zeros((LANES - 8, ts), F32)], axis=0)
    gate_ref[0] = jnp.transpose(gates)

    @pl.when((pl.program_id(0) == 0) & (i == 0))
    def _():
        cnt_ref[...] = jnp.zeros_like(cnt_ref)

    sel = [eid == a for a in ids]
    member = (sel[0] | sel[1] | sel[2] | sel[3]).astype(F32)
    before = jnp.dot(member.astype(BF16), tri_ref[...], preferred_element_type=F32) + cnt_ref[:, 0:1]
    ranks = [jnp.sum(jnp.where(s_, before, 0.0), axis=0, keepdims=True).astype(jnp.int32) for s_ in sel]
    route_ref[0] = jnp.concatenate(ids + ranks, axis=0)
    cnt_ref[...] = cnt_ref[...] + jnp.sum(member, axis=1, keepdims=True)
    cnt_out_ref[...] = cnt_ref[...]


def _mixer_call(x, rope_tab, prm):
    b, s, d = x.shape
    ts = MIX_TILE
    nt = s // ts
    nb = s // ATTN_BLOCK
    per = ts // ATTN_BLOCK
    const2 = lambda bi, i: (0, 0)
    const3 = lambda bi, i: (0, 0, 0)

    def wspec(arr):
        return pl.BlockSpec(arr.shape, const2 if arr.ndim == 2 else const3)

    in_specs = [
        pl.BlockSpec((1, ts, d), lambda bi, i: (bi, i, 0)),
        pl.BlockSpec((1, ATTN_BLOCK, d), lambda bi, i: (bi, jnp.minimum((i + 1) * per, nb - 1), 0)),
        pl.BlockSpec((ts, 3 * LANES), lambda bi, i: (i, 0)),
        pl.BlockSpec((ATTN_BLOCK, 3 * LANES), lambda bi, i: (jnp.minimum((i + 1) * per, nb - 1), 0)),
    ] + [wspec(a) for a in prm]
    out_shape = [
        jax.ShapeDtypeStruct((b, s, d), F32),
        jax.ShapeDtypeStruct((b * s * SUB, LANES), U32),
        jax.ShapeDtypeStruct((b, 2 * TOP_K, s), jnp.int32),
        jax.ShapeDtypeStruct((b, s, LANES), F32),
        jax.ShapeDtypeStruct((N_EXPERTS, LANES), F32),
    ]
    out_specs = [
        pl.BlockSpec((1, ts, d), lambda bi, i: (bi, i, 0)),
        pl.BlockSpec((ts * SUB, LANES), lambda bi, i: (bi * nt + i, 0)),
        pl.BlockSpec((1, 2 * TOP_K, ts), lambda bi, i: (bi, 0, i)),
        pl.BlockSpec((1, ts, LANES), lambda bi, i: (bi, i, 0)),
        pl.BlockSpec((N_EXPERTS, LANES), const2),
    ]
    scratch = [
        pltpu.VMEM((ts, IN_WIDTH), F32),
        pltpu.VMEM((ts, ATTN_WIDTH), BF16),
        pltpu.VMEM((ts + 2 * ATTN_BLOCK, KV_WIDTH), BF16),
        pltpu.VMEM((ts + 2 * ATTN_BLOCK, 2 * N_KV_HEADS * LANES), BF16),
        pltpu.VMEM((ts, ATTN_WIDTH), BF16),
        pltpu.VMEM((ts, GMLP_WIDTH), BF16),
        pltpu.VMEM((N_EXPERTS, LANES), F32),
    ]
    return pl.pallas_call(
        functools.partial(_mixer_kernel, n_blocks_seq=nb),
        grid=(b, nt),
        in_specs=in_specs,
        out_specs=out_specs,
        out_shape=out_shape,
        scratch_shapes=scratch,
        compiler_params=pltpu.CompilerParams(
            dimension_semantics=("arbitrary", "arbitrary"), vmem_limit_bytes=VMEM_LIMIT),
        name="mixer",
    )(x, x, rope_tab, rope_tab, *prm)


def _sc_workers():
    info = plsc.get_sparse_core_info()
    return info.num_cores, info.num_cores * info.num_subcores


def _sc_dispatch(rows2d, slot_rows, n_out_rows):
    n_cores, n_workers = _sc_workers()
    chunk_rows = DISPATCH_TOKENS * SUB
    halves = chunk_rows // SC_CHUNK
    n_chunks = rows2d.shape[0] // chunk_rows
    per_w = n_chunks // n_workers
    idx_rows_per_choice = rows2d.shape[0] // SC_CHUNK
    mesh = plsc.VectorSubcoreMesh(core_axis_name="c", subcore_axis_name="s")

    @functools.partial(
        pl.kernel, mesh=mesh,
        out_type=jax.ShapeDtypeStruct((n_out_rows, LANES), rows2d.dtype),
        scratch_types=[pltpu.VMEM((TOP_K * halves, SC_CHUNK), jnp.int32),
                       pltpu.VMEM((chunk_rows, LANES), rows2d.dtype),
                       pltpu.SemaphoreType.DMA, pltpu.SemaphoreType.DMA],
    )
    def k(src_hbm, idx_hbm, out_hbm, idx_v, rows_v, sem_in, sem_out):
        wid = lax.axis_index("s") * n_cores + lax.axis_index("c")

        @pl.loop(0, per_w)
        def _(j):
            c = wid * per_w + j
            loads = [pltpu.async_copy(src_hbm.at[pl.ds(c * chunk_rows, chunk_rows)], rows_v, sem_in)]
            loads += [
                pltpu.async_copy(idx_hbm.at[pl.ds(kk * idx_rows_per_choice + c * halves, halves)],
                                 idx_v.at[pl.ds(kk * halves, halves)], sem_in)
                for kk in range(TOP_K)]
            for cp in loads:
                cp.wait()
            copies = [
                pltpu.async_copy(rows_v.at[pl.ds((q % halves) * SC_CHUNK, SC_CHUNK)],
                                 out_hbm.at[idx_v.at[q]], sem_out)
                for q in range(TOP_K * halves)]
            for cp in copies:
                cp.wait()

    return k(rows2d, slot_rows)


def _sc_gather(table2d, idx2d):
    n_cores, n_workers = _sc_workers()
    n = idx2d.shape[0] * SC_CHUNK
    per_w = n // n_workers
    step_rows = GATHER_STREAMS * SC_CHUNK
    idx_tile = 8
    mesh = plsc.VectorSubcoreMesh(core_axis_name="c", subcore_axis_name="s")

    @functools.partial(
        pl.kernel, mesh=mesh,
        out_type=jax.ShapeDtypeStruct((n, LANES), table2d.dtype),
        scratch_types=[pltpu.VMEM((idx_tile, SC_CHUNK), jnp.int32),
                       pltpu.VMEM((step_rows, LANES), table2d.dtype),
                       pltpu.SemaphoreType.DMA],
    )
    def k(table_hbm, idx_hbm, out_hbm, idx_v, rows_v, sem):
        wid = lax.axis_index("s") * n_cores + lax.axis_index("c")

        @pl.loop(0, per_w // (idx_tile * SC_CHUNK))
        def _(j):
            row0 = pl.multiple_of(wid * (per_w // SC_CHUNK) + j * idx_tile, idx_tile)
            pltpu.sync_copy(idx_hbm.at[pl.ds(row0, idx_tile)], idx_v)
            for part in range(idx_tile // GATHER_STREAMS):
                copies = [pltpu.async_copy(table_hbm.at[idx_v.at[part * GATHER_STREAMS + q]],
                                           rows_v.at[pl.ds(q * SC_CHUNK, SC_CHUNK)], sem)
                          for q in range(GATHER_STREAMS)]
                for cp in copies:
                    cp.wait()
                pltpu.sync_copy(rows_v, out_hbm.at[pl.ds((row0 + part * GATHER_STREAMS) * SC_CHUNK, step_rows)])

    return k(table2d, idx2d)


def _expert_weight_copies(wgu_hbm, wd_hbm, wgu_buf, wd_buf, sem, expert, slot):
    return (pltpu.make_async_copy(wgu_hbm.at[expert], wgu_buf.at[slot], sem.at[0, slot]),
            pltpu.make_async_copy(wd_hbm.at[expert], wd_buf.at[slot], sem.at[1, slot]))


def _moe_kernel(be_ref, nv_ref, nxt_ref, par_ref, x_ref, wgu_hbm, bgu_ref, wd_hbm, bd_ref, y_ref,
                wgu_buf, wd_buf, sem, wgu_s, wd_s, slab):
    i = pl.program_id(0)
    nv = nv_ref[i]
    slot = par_ref[i]

    cw = MOE_COLS
    n_chunks = D_FF // cw
    copies = functools.partial(_expert_weight_copies, wgu_hbm, wd_hbm, wgu_buf, wd_buf, sem)

    @pl.when(i == 0)
    def _():
        for cp in copies(be_ref[0], slot):
            cp.start()

    @pl.when((i == 0) | (be_ref[i] != be_ref[jnp.maximum(i - 1, 0)]))
    def _():
        for cp in copies(be_ref[i], slot):
            cp.wait()

        @pl.when(nxt_ref[i] >= 0)
        def _():
            for cp in copies(nxt_ref[i], 1 - slot):
                cp.start()

        for c in range(n_chunks):
            wgu_s[:, 2 * c * cw:(2 * c + 1) * cw] = wgu_buf[slot, :, c * cw:(c + 1) * cw].astype(BF16)
            wgu_s[:, (2 * c + 1) * cw:(2 * c + 2) * cw] = (
                wgu_buf[slot, :, D_FF + c * cw:D_FF + (c + 1) * cw].astype(BF16))
        half = D_FF // 2
        for c in range(D_MODEL // LANES):
            slab[c, pl.ds(0, half, stride=2), :] = wd_buf[slot, 0:half, c * LANES:(c + 1) * LANES]
            slab[c, pl.ds(1, half, stride=2), :] = wd_buf[slot, half:D_FF, c * LANES:(c + 1) * LANES]
        for c in range(D_MODEL // LANES):
            wd_s[:, c * LANES:(c + 1) * LANES] = slab[c].astype(BF16)

    expert = be_ref[i]

    def run_expert(rows):
        x = _load_packed(x_ref, rows)
        live = lax.broadcasted_iota(jnp.int32, x.shape, 0) < nv
        xe = jnp.where(live, x, 0.0).astype(BF16)
        even = (lax.broadcasted_iota(jnp.int32, (rows, cw), 1) % 2) == 0
        y = bd_ref[expert]

        def gate_up(c):
            return jnp.dot(xe, wgu_s[:, 2 * c * cw:(2 * c + 2) * cw], preferred_element_type=F32)

        h_next = gate_up(0)
        for c in range(n_chunks):
            h = h_next
            if c + 1 < n_chunks:
                h_next = gate_up(c + 1)
            h_a = h[:, 0:cw] + bgu_ref[expert, :, c * cw:(c + 1) * cw]
            h_b = h[:, cw:2 * cw] + bgu_ref[expert, :, D_FF + c * cw:D_FF + (c + 1) * cw]
            gate = jnp.where(even, h_a, pltpu.roll(h_b, 1, 1))
            up = jnp.where(even, pltpu.roll(h_a, cw - 1, 1), h_b)
            gate = jnp.minimum(gate, SWIGLU_LIMIT)
            up = jnp.clip(up, -SWIGLU_LIMIT, SWIGLU_LIMIT)
            act = (up + 1.0) * (gate * jax.nn.sigmoid(SWIGLU_ALPHA * gate))
            y = y + jnp.dot(act.astype(BF16), wd_s[c * cw:(c + 1) * cw, :], preferred_element_type=F32)
        _store_packed(y_ref, y)

    @pl.when(nv > MOE_BLOCK // 2)
    def _():
        run_expert(MOE_BLOCK)

    @pl.when((nv > 0) & (nv <= MOE_BLOCK // 2))
    def _():
        run_expert(MOE_BLOCK // 2)


def _moe_call(block_e, n_valid, next_e, parity, xb, wgu, bgu, wd, bd):
    n_blocks = block_e.shape[0]
    d = D_MODEL
    grid_spec = pltpu.PrefetchScalarGridSpec(
        num_scalar_prefetch=4,
        grid=(n_blocks,),
        in_specs=[
            pl.BlockSpec((MOE_BLOCK * SUB, LANES), lambda i, be, nv, nx, pr: (i, 0)),
            pl.BlockSpec(memory_space=pl.ANY),
            pl.BlockSpec((N_EXPERTS, 1, 2 * D_FF), lambda i, be, nv, nx, pr: (0, 0, 0)),
            pl.BlockSpec(memory_space=pl.ANY),
            pl.BlockSpec((N_EXPERTS, 1, d), lambda i, be, nv, nx, pr: (0, 0, 0)),
        ],
        out_specs=pl.BlockSpec((MOE_BLOCK * SUB, LANES), lambda i, be, nv, nx, pr: (i, 0)),
        scratch_shapes=[pltpu.VMEM((2, d, 2 * D_FF), F32), pltpu.VMEM((2, D_FF, d), F32),
                        pltpu.SemaphoreType.DMA((2, 2)),
                        pltpu.VMEM((d, 2 * D_FF), BF16), pltpu.VMEM((D_FF, d), BF16),
                        pltpu.VMEM((d // LANES, D_FF, LANES), F32)],
    )
    return pl.pallas_call(
        _moe_kernel,
        grid_spec=grid_spec,
        out_shape=jax.ShapeDtypeStruct((n_blocks * MOE_BLOCK * SUB, LANES), U32),
        compiler_params=pltpu.CompilerParams(
            dimension_semantics=("arbitrary",), vmem_limit_bytes=VMEM_LIMIT),
        name="moe",
    )(block_e, n_valid, next_e, parity, xb, wgu, bgu, wd, bd)


def _combine_kernel(g0_ref, g1_ref, g2_ref, g3_ref, h1_ref, gate_ref, p_ref, nple_ref, wpg_ref, wpp_ref, o_ref):
    g = gate_ref[...]
    moe = jnp.zeros((CMB_TILE, D_MODEL), F32)
    for kk, g_ref in enumerate((g0_ref, g1_ref, g2_ref, g3_ref)):
        moe = moe + _load_packed(g_ref, CMB_TILE) * g[:, kk:kk + 1]
    h2 = h1_ref[...] + moe
    hp = _rms(h2, nple_ref[...]).astype(BF16)
    gate = jax.nn.sigmoid(jnp.dot(hp, wpg_ref[...], preferred_element_type=F32))
    proj = jnp.dot(p_ref[...].astype(BF16), wpp_ref[...], preferred_element_type=F32)
    o_ref[...] = h2 + gate * proj


def _combine_call(gathered, h1_flat, gates, p_flat, nple, wpg, wpp):
    n_tok, d = h1_flat.shape
    tc = CMB_TILE
    n_tiles = n_tok // tc
    row = lambda i: (i, 0)
    const = lambda i: (0, 0)
    g_specs = [pl.BlockSpec((tc * SUB, LANES), functools.partial(lambda i, kk: (kk * n_tiles + i, 0), kk=kk))
               for kk in range(TOP_K)]
    return pl.pallas_call(
        _combine_kernel,
        grid=(n_tiles,),
        in_specs=g_specs + [
            pl.BlockSpec((tc, d), row),
            pl.BlockSpec((tc, LANES), row),
            pl.BlockSpec((tc, PLE_DIM), row),
            pl.BlockSpec((1, d), const),
            pl.BlockSpec((d, d), const),
            pl.BlockSpec((PLE_DIM, d), const),
        ],
        out_specs=pl.BlockSpec((tc, d), row),
        out_shape=jax.ShapeDtypeStruct((n_tok, d), F32),
        compiler_params=pltpu.CompilerParams(
            dimension_semantics=("arbitrary",), vmem_limit_bytes=VMEM_LIMIT),
        name="combine",
    )(gathered, gathered, gathered, gathered, h1_flat, gates, p_flat, nple, wpg, wpp)


def _rope_table(s):
    half = ROPE_DIM // 2
    inv_freq = jnp.power(ROPE_THETA, -jnp.arange(half, dtype=F32) * (2.0 / ROPE_DIM))
    ang = jnp.arange(s, dtype=F32)[:, None] * inv_freq[None, :]
    base = jnp.concatenate([jnp.cos(ang), jnp.sin(ang)], axis=-1)
    place = np.zeros((2 * half, 3 * LANES), np.float32)
    ones = np.zeros((1, 3 * LANES), np.float32)
    for lane in range(3 * LANES):
        kind, d = lane // LANES, lane % HEAD_DIM
        if kind == 0 and d < ROPE_DIM:
            place[d % half, lane] = 1.0
        elif kind == 0:
            ones[0, lane] = 1.0
        elif kind == 1 and d < half:
            place[half + d, lane] = -1.0
        elif kind == 2 and half <= d < ROPE_DIM:
            place[half + d - half, lane] = 1.0
    return jnp.dot(base, place, precision=lax.Precision.HIGHEST) + ones


def _routing(route, counts, n_tok):
    ids = jnp.transpose(route[:, 0:TOP_K, :], (1, 0, 2)).reshape(TOP_K, n_tok)
    ranks = jnp.transpose(route[:, TOP_K:2 * TOP_K, :], (1, 0, 2)).reshape(TOP_K, n_tok)
    counts = counts.astype(jnp.int32)
    padded = (counts + MOE_BLOCK - 1) // MOE_BLOCK * MOE_BLOCK
    pends = jnp.cumsum(padded)
    pstarts = pends - padded
    pos = ranks
    for e in range(N_EXPERTS):
        pos = pos + jnp.where(ids == e, pstarts[e], 0)
    n_slots = -(-(n_tok * TOP_K) // MOE_BLOCK) * MOE_BLOCK + N_EXPERTS * MOE_BLOCK
    n_blocks = n_slots // MOE_BLOCK
    first = jnp.arange(n_blocks, dtype=jnp.int32) * MOE_BLOCK
    block_e = jnp.clip(jnp.sum((first[:, None] >= pends[None, :]).astype(jnp.int32), axis=1), 0, N_EXPERTS - 1)
    own = block_e[:, None] == jnp.arange(N_EXPERTS, dtype=jnp.int32)[None, :]
    left = jnp.sum(jnp.where(own, (counts + pstarts)[None, :], 0), axis=1) - first
    n_valid = jnp.clip(left, 0, MOE_BLOCK)
    later = jnp.where(block_e[None, :] > block_e[:, None], block_e[None, :], N_EXPERTS)
    next_e = jnp.min(later, axis=1)
    next_e = jnp.where(next_e == N_EXPERTS, -1, next_e)
    present = jnp.any(own, axis=0)
    runs_before = jnp.sum((present[None, :] & (jnp.arange(N_EXPERTS)[None, :] < block_e[:, None])).astype(jnp.int32),
                          axis=1)
    parity = runs_before % 2
    return pos, (block_e, n_valid.astype(jnp.int32), next_e.astype(jnp.int32), parity.astype(jnp.int32)), n_slots


def _expand_rows(pos):
    spread = (jnp.arange(SUB * LANES, dtype=jnp.int32)[None, :] // SUB
              == jnp.arange(LANES, dtype=jnp.int32)[:, None]).astype(F32)
    wide = jnp.dot(pos.reshape(-1, LANES).astype(F32), spread, precision=lax.Precision.HIGHEST)
    wide = wide.astype(jnp.int32) * SUB + (jnp.arange(SUB * LANES, dtype=jnp.int32) % SUB)[None, :]
    return wide.reshape(-1, LANES)


def _group_forward(x, p_l, rope_tab, mix_prm, moe_prm, ple_prm):
    b, s, d = x.shape
    n_tok = b * s
    h1, hn_rows, route, gates, counts = _mixer_call(x, rope_tab, mix_prm)
    pos, block_info, n_slots = _routing(route, counts[:, 0], n_tok)
    slot_rows = _expand_rows(pos)
    xb = _sc_dispatch(hn_rows, slot_rows, n_slots * SUB)
    y = _moe_call(*block_info, xb, *moe_prm)
    gathered = _sc_gather(y, slot_rows)
    out = _combine_call(gathered, h1.reshape(n_tok, d), gates.reshape(n_tok, LANES),
                        p_l.reshape(n_tok, PLE_DIM), *ple_prm)
    return out.reshape(b, s, d)


def kernel(x_prompt, x_sample, p_prompt, p_sample, norm_mix, w_in, q_gain, k_gain, attn_sink, gmlp_v_gain, gmlp_w_s, gmlp_b_s, w_branch, w_out, norm_ffn, w_router, b_router, w_gate_up, b_gate_up, w_down, b_down, norm_ple, w_ple_gate, w_ple_proj):
    depth = norm_mix.shape[0]
    hp, hs = x_prompt, x_sample
    for l in range(depth):
        row = lambda a: a.reshape(1, -1)
        blockdiag = jnp.kron(jnp.eye(N_Q_HEADS, dtype=F32),
                             jnp.full((HEAD_DIM, HEAD_DIM), 1.0 / HEAD_DIM, F32)).astype(BF16)
        sink_heads = attn_sink[l].reshape(N_KV_HEADS, GQA_GROUP)[:, jnp.array(HEAD_ORDER)]
        sink_col = jnp.repeat(sink_heads, ATTN_BLOCK, axis=1)[..., None]
        sink_rows = jnp.where(jnp.arange(LANES) == 0, sink_col, -jnp.inf)
        first_row = (jnp.arange(LANES) == 0)[:, None]
        upper = (jnp.arange(LANES) >= HEAD_DIM)[None, :]
        sink_values = jnp.stack([first_row & upper, first_row & ~upper]).astype(BF16)
        wcat = jnp.transpose(gmlp_w_s[l], (1, 0, 2)).reshape(GMLP_CHUNK, GMLP_GROUPS * GMLP_CHUNK).astype(BF16)
        bias_full = jnp.repeat(gmlp_b_s[l].T, GMLP_GROUP_DIM, axis=1)
        wr = jnp.pad(w_router[l], ((0, 0), (0, ROUTER_PAD - N_EXPERTS)))
        wr_hi = wr.astype(BF16)
        wr_lo = (wr - wr_hi.astype(F32)).astype(BF16)
        br = jnp.pad(b_router[l], (0, ROUTER_PAD - N_EXPERTS)).reshape(1, ROUTER_PAD)
        tri = (jnp.arange(MIX_TILE)[:, None] < jnp.arange(MIX_TILE)[None, :]).astype(BF16)
        mix_prm = (
            row(norm_mix[l]), w_in[l].astype(BF16),
            row(jnp.tile(q_gain[l], N_Q_HEADS)), row(jnp.tile(k_gain[l], N_KV_HEADS)),
            sink_rows, sink_values, row(gmlp_v_gain[l]), blockdiag, wcat, bias_full,
            w_branch[l].astype(BF16), w_out[l].astype(BF16), row(norm_ffn[l]),
            jnp.concatenate([wr_hi, wr_lo], axis=1), br, tri,
        )
        moe_prm = (
            w_gate_up[l], b_gate_up[l].reshape(N_EXPERTS, 1, 2 * D_FF),
            w_down[l], b_down[l].reshape(N_EXPERTS, 1, D_MODEL),
        )
        ple_prm = (row(norm_ple[l]), w_ple_gate[l].astype(BF16), w_ple_proj[l].astype(BF16))
        rope_tab = _rope_table(max(hp.shape[1], hs.shape[1]))
        hp = _group_forward(hp, p_prompt[l], rope_tab, mix_prm, moe_prm, ple_prm)
        hs = _group_forward(hs, p_sample[l], rope_tab, mix_prm, moe_prm, ple_prm)
    return (hp.astype(x_prompt.dtype), hs.astype(x_sample.dtype))
```

```python
import functools

import jax
import jax.numpy as jnp
import numpy as np
from jax import lax
from jax.experimental import pallas as pl
from jax.experimental.pallas import tpu as pltpu
from jax.experimental.pallas import tpu_sc as plsc

D_MODEL = 1024
HEAD_DIM = 64
N_Q_HEADS = 8
N_KV_HEADS = 2
GQA_GROUP = N_Q_HEADS // N_KV_HEADS
HEAD_ORDER = (0, 2, 1, 3)
ATTN_WIDTH = N_Q_HEADS * HEAD_DIM
KV_WIDTH = N_KV_HEADS * HEAD_DIM
WINDOW = 128
ATTN_BLOCK = 128
ROPE_THETA = 500000.0
ROPE_DIM = HEAD_DIM // 4
GMLP_WIDTH = D_MODEL // 2
GMLP_GROUPS = 8
GMLP_GROUP_DIM = GMLP_WIDTH // GMLP_GROUPS
GMLP_CHUNK = 128
N_BRANCH = 2
IN_WIDTH = ATTN_WIDTH + 2 * KV_WIDTH + 2 * GMLP_WIDTH + N_BRANCH * D_MODEL
N_EXPERTS = 32
TOP_K = 4
D_FF = D_MODEL
SWIGLU_ALPHA = 1.702
SWIGLU_LIMIT = 7.0
PLE_DIM = 256
EPS = 1e-6

Q_OFF = 0
K_OFF = ATTN_WIDTH
V_OFF = K_OFF + KV_WIDTH
U_OFF = V_OFF + KV_WIDTH
VG_OFF = U_OFF + GMLP_WIDTH
GL_OFF = VG_OFF + GMLP_WIDTH

LANES = 128
PACK_WORDS = D_MODEL // 2
SUB = PACK_WORDS // LANES
MIX_TILE = 512
PROJ_COLS = 256
MOE_BLOCK = 512
MOE_SUB = 256
MOE_COLS = 256
CMB_TILE = 512
ROUTER_PAD = 128
SC_CHUNK = 128
DISPATCH_TOKENS = 64
GATHER_STREAMS = 4
VMEM_LIMIT = 56 * 1024 * 1024

BF16 = jnp.bfloat16
F32 = jnp.float32
U32 = jnp.uint32


def _rms(x, gain):
    return x * lax.rsqrt(jnp.mean(x * x, axis=-1, keepdims=True) + EPS) * gain


def _gelu(x):
    return 0.5 * x * (1.0 + lax.erf(x * np.float32(np.sqrt(0.5))))


def _sigmoid(x):
    return 0.5 * jnp.tanh(0.5 * x) + 0.5


def _head_rms(x, blockdiag, gain):
    ms = jnp.dot((x * x).astype(BF16), blockdiag, preferred_element_type=F32)
    return x * lax.rsqrt(ms + EPS) * gain


def _rope(x, cos, sin_lo, sin_hi):
    w = x.shape[-1]
    return x * cos + pltpu.roll(x, w - ROPE_DIM // 2, 1) * sin_lo + pltpu.roll(x, ROPE_DIM // 2, 1) * sin_hi


def _value_blocks(v):
    swapped = pltpu.roll(v, HEAD_DIM, 1)
    low = lax.broadcasted_iota(jnp.int32, v.shape, 1) < HEAD_DIM
    blocks = [jnp.where(low, v, 1.0), jnp.where(low, 1.0, swapped), jnp.where(low, swapped, 1.0), jnp.where(low, 1.0, v)]
    return jnp.concatenate(blocks, axis=1).astype(BF16)


def _tile_lanes(t, reps):
    return t if reps == 1 else jnp.concatenate([t] * reps, axis=-1)


def _store_packed(ref, x, first=0):
    rows = x.shape[0]
    hi = lax.bitcast_convert_type(x[:, :PACK_WORDS].astype(BF16).astype(F32), U32)
    lo = lax.bitcast_convert_type(x[:, PACK_WORDS:].astype(BF16).astype(F32), U32)
    words = hi | (lo >> 16)
    for j in range(SUB):
        ref[pl.ds(first * SUB + j, rows, stride=SUB), :] = words[:, j * LANES:(j + 1) * LANES]


def _load_packed(ref, rows, first=0):
    words = jnp.concatenate([ref[pl.ds(first * SUB + j, rows, stride=SUB), :] for j in range(SUB)], axis=1)
    left = lax.bitcast_convert_type(words & np.uint32(0xFFFF0000), F32)
    right = lax.bitcast_convert_type(words << 16, F32)
    return jnp.concatenate([left, right], axis=1)


def _mixer_kernel(x_ref, xn_ref, rp_ref, rpn_ref,
                  nmix_ref, win_ref, qg_ref, kg_ref, sink_ref, sinkv_ref, vgain_ref, bdq_ref,
                  wcat_ref, bias_ref, wbr_ref, wout_ref, nffn_ref, wr_ref, br_ref, tri_ref,
                  h1_ref, hn_ref, route_ref, gate_ref, cnt_out_ref,
                  z_ref, q_ref, k_ref, v_ref, attn_ref, gm_ref, cnt_ref, *, n_blocks_seq):
    ts = MIX_TILE
    i = pl.program_id(1)
    x = x_ref[0]

    @pl.when(i > 0)
    def _():
        k_ref[0:ATTN_BLOCK, :] = k_ref[ts:ts + ATTN_BLOCK, :]
        v_ref[0:ATTN_BLOCK, :] = v_ref[ts:ts + ATTN_BLOCK, :]

    @pl.when(i == 0)
    def _():
        k_ref[0:ATTN_BLOCK, :] = jnp.zeros((ATTN_BLOCK, k_ref.shape[1]), BF16)
        v_ref[0:ATTN_BLOCK, :] = jnp.zeros((ATTN_BLOCK, v_ref.shape[1]), BF16)

    xn = _rms(x, nmix_ref[...]).astype(BF16)

    def project(lo, hi):
        z_ref[:, lo:hi] = jnp.dot(xn, win_ref[:, lo:hi], preferred_element_type=F32)

    project(Q_OFF, U_OFF)
    n_items = (ts // ATTN_BLOCK) * N_KV_HEADS
    later = [(U_OFF + t * PROJ_COLS, U_OFF + (t + 1) * PROJ_COLS) for t in range((IN_WIDTH - U_OFF) // PROJ_COLS)]
    per_item = [len(later) // n_items + (1 if n < len(later) % n_items else 0) for n in range(n_items)]

    cos = rp_ref[:, 0:LANES]
    sin_lo = rp_ref[:, LANES:2 * LANES]
    sin_hi = rp_ref[:, 2 * LANES:3 * LANES]
    bdq = bdq_ref[...]
    bdk = bdq_ref[0:KV_WIDTH, 0:KV_WIDTH]

    q = _head_rms(z_ref[:, Q_OFF:Q_OFF + ATTN_WIDTH], bdq, qg_ref[...])
    reps = ATTN_WIDTH // LANES
    q = _rope(q, _tile_lanes(cos, reps), _tile_lanes(sin_lo, reps), _tile_lanes(sin_hi, reps))
    q_ref[...] = (q * (HEAD_DIM ** -0.5)).astype(BF16)

    k = _head_rms(z_ref[:, K_OFF:K_OFF + KV_WIDTH], bdk, kg_ref[...])
    k_ref[ATTN_BLOCK:ATTN_BLOCK + ts, :] = _rope(k, cos, sin_lo, sin_hi).astype(BF16)
    v_ref[ATTN_BLOCK:ATTN_BLOCK + ts, :] = _value_blocks(z_ref[:, V_OFF:V_OFF + KV_WIDTH])

    xhn = _rms(xn_ref[0], nmix_ref[...]).astype(BF16)
    zh = jnp.dot(xhn, win_ref[:, K_OFF:K_OFF + 2 * KV_WIDTH], preferred_element_type=F32)
    kh = _head_rms(zh[:, 0:KV_WIDTH], bdk, kg_ref[...])
    kh = _rope(kh, rpn_ref[:, 0:LANES], rpn_ref[:, LANES:2 * LANES], rpn_ref[:, 2 * LANES:3 * LANES])
    k_ref[ATTN_BLOCK + ts:2 * ATTN_BLOCK + ts, :] = kh.astype(BF16)
    v_ref[ATTN_BLOCK + ts:2 * ATTN_BLOCK + ts, :] = _value_blocks(zh[:, KV_WIDTH:2 * KV_WIDTH])

    rows = GQA_GROUP * ATTN_BLOCK
    keys = 3 * ATTN_BLOCK
    r = lax.broadcasted_iota(jnp.int32, (rows, keys), 0) % ATTN_BLOCK
    c = lax.broadcasted_iota(jnp.int32, (rows, keys), 1)
    band = (c >= r) & (c <= r + 2 * WINDOW)
    low = lax.broadcasted_iota(jnp.int32, (ATTN_BLOCK, 2 * HEAD_DIM), 1) < HEAD_DIM
    pair = 2 * ATTN_BLOCK
    for qb in range(ts // ATTN_BLOCK):
        gb = i * (ts // ATTN_BLOCK) + qb
        lo = jnp.where(gb == 0, ATTN_BLOCK, 0)
        hi = jnp.where(gb == n_blocks_seq - 1, 2 * ATTN_BLOCK, keys)
        valid = band & (c >= lo) & (c < hi)
        r0 = qb * ATTN_BLOCK
        for j in range(N_KV_HEADS):
            for _ in range(per_item[qb * N_KV_HEADS + j]):
                project(*later.pop(0))
            q4 = jnp.concatenate(
                [q_ref[r0:r0 + ATTN_BLOCK, (GQA_GROUP * j + g) * HEAD_DIM:(GQA_GROUP * j + g + 1) * HEAD_DIM]
                 for g in HEAD_ORDER], axis=0)
            kw = k_ref[r0:r0 + keys, j * HEAD_DIM:(j + 1) * HEAD_DIM]
            s = lax.dot_general(q4, kw, (((1,), (1,)), ((), ())), preferred_element_type=F32)
            s = jnp.concatenate([jnp.where(valid, s, -jnp.inf), sink_ref[j]], axis=1)
            p = jnp.exp(s - jnp.max(s, axis=-1, keepdims=True)).astype(BF16)
            outs = []
            for par in range(2):
                vw = jnp.concatenate([v_ref[r0:r0 + keys, (2 * j + par) * LANES:(2 * j + par + 1) * LANES],
                                      sinkv_ref[par]], axis=0)
                o = jnp.dot(p[par * pair:(par + 1) * pair], vw, preferred_element_type=F32)
                outs.append(o / pltpu.roll(o, HEAD_DIM, 1))
            for a in range(GQA_GROUP // 2):
                both = jnp.where(low, outs[0][a * ATTN_BLOCK:(a + 1) * ATTN_BLOCK],
                                 outs[1][a * ATTN_BLOCK:(a + 1) * ATTN_BLOCK])
                h0 = GQA_GROUP * j + 2 * a
                attn_ref[r0:r0 + ATTN_BLOCK, h0 * HEAD_DIM:(h0 + 2) * HEAD_DIM] = both.astype(BF16)

    vgn_all = _head_rms(_gelu(z_ref[:, VG_OFF:VG_OFF + GMLP_WIDTH]), bdq, vgain_ref[...])
    half = GMLP_WIDTH // 2
    gpm = half // GMLP_GROUP_DIM
    lane_grp = lax.broadcasted_iota(jnp.int32, (GMLP_CHUNK, half), 1) // GMLP_GROUP_DIM
    mixed = []
    for ch in range(ts // GMLP_CHUNK):
        c0 = ch * GMLP_CHUNK
        parts = []
        for nt in range(2):
            part = vgn_all[c0:c0 + GMLP_CHUNK, nt * half:(nt + 1) * half]
            vexp = jnp.concatenate(
                [jnp.where(lane_grp == gl, part, 0.0).astype(BF16) for gl in range(gpm)], axis=0)
            wpart = wcat_ref[:, nt * gpm * GMLP_CHUNK:(nt + 1) * gpm * GMLP_CHUNK]
            parts.append(jnp.dot(wpart, vexp, preferred_element_type=F32))
        mixed.append(jnp.concatenate(parts, axis=-1))
    for ch in range(ts // GMLP_CHUNK):
        c0 = ch * GMLP_CHUNK
        u = _gelu(z_ref[c0:c0 + GMLP_CHUNK, U_OFF:U_OFF + GMLP_WIDTH])
        gm_ref[c0:c0 + GMLP_CHUNK, :] = (u * (mixed[ch] + bias_ref[...])).astype(BF16)

    ya = jnp.dot(attn_ref[...], wbr_ref[0], preferred_element_type=F32)
    yg = jnp.dot(gm_ref[...], wbr_ref[1], preferred_element_type=F32)
    ga = _sigmoid(z_ref[:, GL_OFF:GL_OFF + D_MODEL])
    gg = _sigmoid(z_ref[:, GL_OFF + D_MODEL:GL_OFF + 2 * D_MODEL])
    merged = (ga * ya + gg * yg).astype(BF16)
    h1 = x + jnp.dot(merged, wout_ref[...], preferred_element_type=F32)
    h1_ref[0] = h1

    hn = _rms(h1, nffn_ref[...])
    _store_packed(hn_ref, hn)

    hn_hi = hn.astype(BF16)
    hn_lo = (hn - hn_hi.astype(F32)).astype(BF16)
    by_hi = jnp.dot(hn_hi, wr_ref[...], preferred_element_type=F32)
    by_lo = jnp.dot(hn_lo, wr_ref[:, 0:ROUTER_PAD], preferred_element_type=F32)
    logits = by_hi[:, 0:ROUTER_PAD] + by_hi[:, ROUTER_PAD:2 * ROUTER_PAD] + by_lo + br_ref[...]
    lt = jnp.transpose(logits)[0:N_EXPERTS, :]
    eid = lax.broadcasted_iota(jnp.int32, lt.shape, 0)
    vals, ids = [], []
    for _ in range(TOP_K):
        mx = jnp.max(lt, axis=0, keepdims=True)
        am = jnp.min(jnp.where(lt == mx, eid, N_EXPERTS), axis=0, keepdims=True)
        vals.append(mx)
        ids.append(am)
        lt = jnp.where(eid == am, -jnp.inf, lt)
    ex = [jnp.exp(v - vals[0]) for v in vals]
    tot = ex[0] + ex[1] + ex[2] + ex[3]
    g8 = jnp.concatenate([e / tot for e in ex] + [jnp.zeros_like(tot)] * (8 - TOP_K), axis=0)
    gates = jnp.concatenate([g8, jnp.zeros((LANES - 8, ts), F32)], axis=0)
    gate_ref[0] = jnp.transpose(gates)

    @pl.when((pl.program_id(0) == 0) & (i == 0))
    def _():
        cnt_ref[...] = jnp.zeros_like(cnt_ref)

    sel = [eid == a for a in ids]
    member = (sel[0] | sel[1] | sel[2] | sel[3]).astype(F32)
    before = jnp.dot(member.astype(BF16), tri_ref[...], preferred_element_type=F32) + cnt_ref[:, 0:1]
    ranks = [jnp.sum(jnp.where(s_, before, 0.0), axis=0, keepdims=True).astype(jnp.int32) for s_ in sel]
    route_ref[...] = jnp.concatenate(ids + ranks, axis=0)
    cnt_ref[...] = cnt_ref[...] + jnp.sum(member, axis=1, keepdims=True)
    cnt_out_ref[...] = cnt_ref[...]


def _mixer_call(x, rope_tab, prm):
    b, s, d = x.shape
    ts = MIX_TILE
    nt = s // ts
    nb = s // ATTN_BLOCK
    per = ts // ATTN_BLOCK
    const2 = lambda bi, i: (0, 0)
    const3 = lambda bi, i: (0, 0, 0)

    def wspec(arr):
        return pl.BlockSpec(arr.shape, const2 if arr.ndim == 2 else const3)

    in_specs = [
        pl.BlockSpec((1, ts, d), lambda bi, i: (bi, i, 0)),
        pl.BlockSpec((1, ATTN_BLOCK, d), lambda bi, i: (bi, jnp.minimum((i + 1) * per, nb - 1), 0)),
        pl.BlockSpec((ts, 3 * LANES), lambda bi, i: (i, 0)),
        pl.BlockSpec((ATTN_BLOCK, 3 * LANES), lambda bi, i: (jnp.minimum((i + 1) * per, nb - 1), 0)),
    ] + [wspec(a) for a in prm]
    out_shape = [
        jax.ShapeDtypeStruct((b, s, d), F32),
        jax.ShapeDtypeStruct((b * s * SUB, LANES), U32),
        jax.ShapeDtypeStruct((2 * TOP_K, b * s), jnp.int32),
        jax.ShapeDtypeStruct((b, s, LANES), F32),
        jax.ShapeDtypeStruct((N_EXPERTS, LANES), F32),
    ]
    out_specs = [
        pl.BlockSpec((1, ts, d), lambda bi, i: (bi, i, 0)),
        pl.BlockSpec((ts * SUB, LANES), lambda bi, i: (bi * nt + i, 0)),
        pl.BlockSpec((2 * TOP_K, ts), lambda bi, i: (0, bi * nt + i)),
        pl.BlockSpec((1, ts, LANES), lambda bi, i: (bi, i, 0)),
        pl.BlockSpec((N_EXPERTS, LANES), const2),
    ]
    scratch = [
        pltpu.VMEM((ts, IN_WIDTH), F32),
        pltpu.VMEM((ts, ATTN_WIDTH), BF16),
        pltpu.VMEM((ts + 2 * ATTN_BLOCK, KV_WIDTH), BF16),
        pltpu.VMEM((ts + 2 * ATTN_BLOCK, 2 * N_KV_HEADS * LANES), BF16),
        pltpu.VMEM((ts, ATTN_WIDTH), BF16),
        pltpu.VMEM((ts, GMLP_WIDTH), BF16),
        pltpu.VMEM((N_EXPERTS, LANES), F32),
    ]
    return pl.pallas_call(
        functools.partial(_mixer_kernel, n_blocks_seq=nb),
        grid=(b, nt),
        in_specs=in_specs,
        out_specs=out_specs,
        out_shape=out_shape,
        scratch_shapes=scratch,
        compiler_params=pltpu.CompilerParams(
            dimension_semantics=("arbitrary", "arbitrary"), vmem_limit_bytes=VMEM_LIMIT),
        name="mixer",
    )(x, x, rope_tab, rope_tab, *prm)


def _sc_workers():
    info = plsc.get_sparse_core_info()
    return info.num_cores, info.num_cores * info.num_subcores


def _sc_dispatch(rows2d, slot_rows, n_out_rows):
    n_cores, n_workers = _sc_workers()
    chunk_rows = DISPATCH_TOKENS * SUB
    halves = chunk_rows // SC_CHUNK
    n_chunks = rows2d.shape[0] // chunk_rows
    per_w = n_chunks // n_workers
    idx_rows_per_choice = rows2d.shape[0] // SC_CHUNK
    mesh = plsc.VectorSubcoreMesh(core_axis_name="c", subcore_axis_name="s")

    @functools.partial(
        pl.kernel, mesh=mesh,
        out_type=jax.ShapeDtypeStruct((n_out_rows, LANES), rows2d.dtype),
        scratch_types=[pltpu.VMEM((TOP_K * halves, SC_CHUNK), jnp.int32),
                       pltpu.VMEM((chunk_rows, LANES), rows2d.dtype),
                       pltpu.SemaphoreType.DMA, pltpu.SemaphoreType.DMA],
    )
    def k(src_hbm, idx_hbm, out_hbm, idx_v, rows_v, sem_in, sem_out):
        wid = lax.axis_index("s") * n_cores + lax.axis_index("c")

        @pl.loop(0, per_w)
        def _(j):
            c = wid * per_w + j
            loads = [pltpu.async_copy(src_hbm.at[pl.ds(c * chunk_rows, chunk_rows)], rows_v, sem_in)]
            loads += [
                pltpu.async_copy(idx_hbm.at[pl.ds(kk * idx_rows_per_choice + c * halves, halves)],
                                 idx_v.at[pl.ds(kk * halves, halves)], sem_in)
                for kk in range(TOP_K)]
            for cp in loads:
                cp.wait()
            copies = [
                pltpu.async_copy(rows_v.at[pl.ds((q % halves) * SC_CHUNK, SC_CHUNK)],
                                 out_hbm.at[idx_v.at[q]], sem_out)
                for q in range(TOP_K * halves)]
            for cp in copies:
                cp.wait()

    return k(rows2d, slot_rows)


def _sc_gather(table2d, idx2d):
    n_cores, n_workers = _sc_workers()
    n = idx2d.shape[0] * SC_CHUNK
    per_w = n // n_workers
    step_rows = GATHER_STREAMS * SC_CHUNK
    idx_tile = 8
    mesh = plsc.VectorSubcoreMesh(core_axis_name="c", subcore_axis_name="s")

    @functools.partial(
        pl.kernel, mesh=mesh,
        out_type=jax.ShapeDtypeStruct((n, LANES), table2d.dtype),
        scratch_types=[pltpu.VMEM((idx_tile, SC_CHUNK), jnp.int32),
                       pltpu.VMEM((step_rows, LANES), table2d.dtype),
                       pltpu.SemaphoreType.DMA],
    )
    def k(table_hbm, idx_hbm, out_hbm, idx_v, rows_v, sem):
        wid = lax.axis_index("s") * n_cores + lax.axis_index("c")

        @pl.loop(0, per_w // (idx_tile * SC_CHUNK))
        def _(j):
            row0 = pl.multiple_of(wid * (per_w // SC_CHUNK) + j * idx_tile, idx_tile)
            pltpu.sync_copy(idx_hbm.at[pl.ds(row0, idx_tile)], idx_v)
            for part in range(idx_tile // GATHER_STREAMS):
                copies = [pltpu.async_copy(table_hbm.at[idx_v.at[part * GATHER_STREAMS + q]],
                                           rows_v.at[pl.ds(q * SC_CHUNK, SC_CHUNK)], sem)
                          for q in range(GATHER_STREAMS)]
                for cp in copies:
                    cp.wait()
                pltpu.sync_copy(rows_v, out_hbm.at[pl.ds((row0 + part * GATHER_STREAMS) * SC_CHUNK, step_rows)])

    return k(table2d, idx2d)


def _expert_weight_copies(wgu_hbm, wd_hbm, wgu_buf, wd_buf, sem, expert, slot):
    return (pltpu.make_async_copy(wgu_hbm.at[expert], wgu_buf.at[slot], sem.at[0, slot]),
            pltpu.make_async_copy(wd_hbm.at[expert], wd_buf.at[slot], sem.at[1, slot]))


def _moe_kernel(be_ref, nv_ref, nxt_ref, par_ref, x_ref, wgu_hbm, bgu_ref, wd_hbm, bd_ref, y_ref,
                wgu_buf, wd_buf, sem, wgu_s, wd_s, slab):
    i = pl.program_id(0)
    nv = nv_ref[i]
    slot = par_ref[i]

    cw = MOE_COLS
    n_chunks = D_FF // cw
    copies = functools.partial(_expert_weight_copies, wgu_hbm, wd_hbm, wgu_buf, wd_buf, sem)

    @pl.when(i == 0)
    def _():
        for cp in copies(be_ref[0], slot):
            cp.start()

    @pl.when((i == 0) | (be_ref[i] != be_ref[jnp.maximum(i - 1, 0)]))
    def _():
        for cp in copies(be_ref[i], slot):
            cp.wait()

        @pl.when(nxt_ref[i] >= 0)
        def _():
            for cp in copies(nxt_ref[i], 1 - slot):
                cp.start()

        for c in range(n_chunks):
            wgu_s[:, 2 * c * cw:(2 * c + 1) * cw] = wgu_buf[slot, :, c * cw:(c + 1) * cw].astype(BF16)
            wgu_s[:, (2 * c + 1) * cw:(2 * c + 2) * cw] = (
                wgu_buf[slot, :, D_FF + c * cw:D_FF + (c + 1) * cw].astype(BF16))
        half = D_FF // 2
        for c in range(D_MODEL // LANES):
            slab[c, pl.ds(0, half, stride=2), :] = wd_buf[slot, 0:half, c * LANES:(c + 1) * LANES]
            slab[c, pl.ds(1, half, stride=2), :] = wd_buf[slot, half:D_FF, c * LANES:(c + 1) * LANES]
        for c in range(D_MODEL // LANES):
            wd_s[:, c * LANES:(c + 1) * LANES] = slab[c].astype(BF16)

    expert = be_ref[i]

    def run_expert(rows):
        n_sub = rows // MOE_SUB
        even = (lax.broadcasted_iota(jnp.int32, (MOE_SUB, cw), 1) % 2) == 0

        def unpack(sb):
            x = _load_packed(x_ref, MOE_SUB, sb * MOE_SUB)
            live = lax.broadcasted_iota(jnp.int32, x.shape, 0) < nv - sb * MOE_SUB
            return jnp.where(live, x, 0.0).astype(BF16)

        def gate_up(xe, c):
            return jnp.dot(xe, wgu_s[:, 2 * c * cw:(2 * c + 2) * cw], preferred_element_type=F32)

        xe = unpack(0)
        h_next = gate_up(xe, 0)
        for sb in range(n_sub):
            y = bd_ref[expert]
            xe_next = None
            for c in range(n_chunks):
                h = h_next
                if c + 1 < n_chunks:
                    h_next = gate_up(xe, c + 1)
                elif sb + 1 < n_sub:
                    h_next = gate_up(xe_next, 0)
                if c == 0 and sb + 1 < n_sub:
                    xe_next = unpack(sb + 1)
                h_a = h[:, 0:cw] + bgu_ref[expert, :, c * cw:(c + 1) * cw]
                h_b = h[:, cw:2 * cw] + bgu_ref[expert, :, D_FF + c * cw:D_FF + (c + 1) * cw]
                gate = jnp.where(even, h_a, pltpu.roll(h_b, 1, 1))
                up = jnp.where(even, pltpu.roll(h_a, cw - 1, 1), h_b)
                gate = jnp.minimum(gate, SWIGLU_LIMIT)
                up = jnp.clip(up, -SWIGLU_LIMIT, SWIGLU_LIMIT)
                act = (up + 1.0) * (gate * jax.nn.sigmoid(SWIGLU_ALPHA * gate))
                y = y + jnp.dot(act.astype(BF16), wd_s[c * cw:(c + 1) * cw, :], preferred_element_type=F32)
            _store_packed(y_ref, y, sb * MOE_SUB)
            xe = xe_next

    @pl.when(nv > MOE_BLOCK // 2)
    def _():
        run_expert(MOE_BLOCK)

    @pl.when((nv > 0) & (nv <= MOE_BLOCK // 2))
    def _():
        run_expert(MOE_BLOCK // 2)


def _moe_call(block_e, n_valid, next_e, parity, xb, wgu, bgu, wd, bd):
    n_blocks = block_e.shape[0]
    d = D_MODEL
    grid_spec = pltpu.PrefetchScalarGridSpec(
        num_scalar_prefetch=4,
        grid=(n_blocks,),
        in_specs=[
            pl.BlockSpec((MOE_BLOCK * SUB, LANES), lambda i, be, nv, nx, pr: (i, 0)),
            pl.BlockSpec(memory_space=pl.ANY),
            pl.BlockSpec((N_EXPERTS, 1, 2 * D_FF), lambda i, be, nv, nx, pr: (0, 0, 0)),
            pl.BlockSpec(memory_space=pl.ANY),
            pl.BlockSpec((N_EXPERTS, 1, d), lambda i, be, nv, nx, pr: (0, 0, 0)),
        ],
        out_specs=pl.BlockSpec((MOE_BLOCK * SUB, LANES), lambda i, be, nv, nx, pr: (i, 0)),
        scratch_shapes=[pltpu.VMEM((2, d, 2 * D_FF), F32), pltpu.VMEM((2, D_FF, d), F32),
                        pltpu.SemaphoreType.DMA((2, 2)),
                        pltpu.VMEM((d, 2 * D_FF), BF16), pltpu.VMEM((D_FF, d), BF16),
                        pltpu.VMEM((d // LANES, D_FF, LANES), F32)],
    )
    return pl.pallas_call(
        _moe_kernel,
        grid_spec=grid_spec,
        out_shape=jax.ShapeDtypeStruct((n_blocks * MOE_BLOCK * SUB, LANES), U32),
        compiler_params=pltpu.CompilerParams(
            dimension_semantics=("arbitrary",), vmem_limit_bytes=VMEM_LIMIT),
        name="moe",
    )(block_e, n_valid, next_e, parity, xb, wgu, bgu, wd, bd)


def _combine_kernel(g0_ref, g1_ref, g2_ref, g3_ref, h1_ref, gate_ref, p_ref, nple_ref, wpg_ref, wpp_ref, o_ref):
    g = gate_ref[...]
    moe = jnp.zeros((CMB_TILE, D_MODEL), F32)
    for kk, g_ref in enumerate((g0_ref, g1_ref, g2_ref, g3_ref)):
        moe = moe + _load_packed(g_ref, CMB_TILE) * g[:, kk:kk + 1]
    h2 = h1_ref[...] + moe
    hp = _rms(h2, nple_ref[...]).astype(BF16)
    gate = jax.nn.sigmoid(jnp.dot(hp, wpg_ref[...], preferred_element_type=F32))
    proj = jnp.dot(p_ref[...].astype(BF16), wpp_ref[...], preferred_element_type=F32)
    o_ref[...] = h2 + gate * proj


def _combine_call(gathered, h1_flat, gates, p_flat, nple, wpg, wpp):
    n_tok, d = h1_flat.shape
    tc = CMB_TILE
    n_tiles = n_tok // tc
    row = lambda i: (i, 0)
    const = lambda i: (0, 0)
    g_specs = [pl.BlockSpec((tc * SUB, LANES), functools.partial(lambda i, kk: (kk * n_tiles + i, 0), kk=kk))
               for kk in range(TOP_K)]
    return pl.pallas_call(
        _combine_kernel,
        grid=(n_tiles,),
        in_specs=g_specs + [
            pl.BlockSpec((tc, d), row),
            pl.BlockSpec((tc, LANES), row),
            pl.BlockSpec((tc, PLE_DIM), row),
            pl.BlockSpec((1, d), const),
            pl.BlockSpec((d, d), const),
            pl.BlockSpec((PLE_DIM, d), const),
        ],
        out_specs=pl.BlockSpec((tc, d), row),
        out_shape=jax.ShapeDtypeStruct((n_tok, d), F32),
        compiler_params=pltpu.CompilerParams(
            dimension_semantics=("arbitrary",), vmem_limit_bytes=VMEM_LIMIT),
        name="combine",
    )(gathered, gathered, gathered, gathered, h1_flat, gates, p_flat, nple, wpg, wpp)


def _rope_table(s):
    half = ROPE_DIM // 2
    inv_freq = jnp.power(ROPE_THETA, -jnp.arange(half, dtype=F32) * (2.0 / ROPE_DIM))
    ang = jnp.arange(s, dtype=F32)[:, None] * inv_freq[None, :]
    base = jnp.concatenate([jnp.cos(ang), jnp.sin(ang)], axis=-1)
    place = np.zeros((2 * half, 3 * LANES), np.float32)
    ones = np.zeros((1, 3 * LANES), np.float32)
    for lane in range(3 * LANES):
        kind, d = lane // LANES, lane % HEAD_DIM
        if kind == 0 and d < ROPE_DIM:
            place[d % half, lane] = 1.0
        elif kind == 0:
            ones[0, lane] = 1.0
        elif kind == 1 and d < half:
            place[half + d, lane] = -1.0
        elif kind == 2 and half <= d < ROPE_DIM:
            place[half + d - half, lane] = 1.0
    return jnp.dot(base, place, precision=lax.Precision.HIGHEST) + ones


def _routing(route, counts, n_tok):
    ids = route[0:TOP_K]
    ranks = route[TOP_K:2 * TOP_K]
    counts = counts.astype(jnp.int32)
    padded = (counts + MOE_BLOCK - 1) // MOE_BLOCK * MOE_BLOCK
    pends = jnp.cumsum(padded)
    pstarts = pends - padded
    pos = ranks
    for e in range(N_EXPERTS):
        pos = pos + jnp.where(ids == e, pstarts[e], 0)
    n_slots = -(-(n_tok * TOP_K) // MOE_BLOCK) * MOE_BLOCK + N_EXPERTS * MOE_BLOCK
    n_blocks = n_slots // MOE_BLOCK
    first = jnp.arange(n_blocks, dtype=jnp.int32) * MOE_BLOCK
    block_e = jnp.clip(jnp.sum((first[:, None] >= pends[None, :]).astype(jnp.int32), axis=1), 0, N_EXPERTS - 1)
    own = block_e[:, None] == jnp.arange(N_EXPERTS, dtype=jnp.int32)[None, :]
    left = jnp.sum(jnp.where(own, (counts + pstarts)[None, :], 0), axis=1) - first
    n_valid = jnp.clip(left, 0, MOE_BLOCK)
    later = jnp.where(block_e[None, :] > block_e[:, None], block_e[None, :], N_EXPERTS)
    next_e = jnp.min(later, axis=1)
    next_e = jnp.where(next_e == N_EXPERTS, -1, next_e)
    present = jnp.any(own, axis=0)
    runs_before = jnp.sum((present[None, :] & (jnp.arange(N_EXPERTS)[None, :] < block_e[:, None])).astype(jnp.int32),
                          axis=1)
    parity = runs_before % 2
    return pos, (block_e, n_valid.astype(jnp.int32), next_e.astype(jnp.int32), parity.astype(jnp.int32)), n_slots


def _expand_rows(pos):
    spread = (jnp.arange(SUB * LANES, dtype=jnp.int32)[None, :] // SUB
              == jnp.arange(LANES, dtype=jnp.int32)[:, None]).astype(F32)
    wide = jnp.dot(pos.reshape(-1, LANES).astype(F32), spread, precision=lax.Precision.HIGHEST)
    wide = wide.astype(jnp.int32) * SUB + (jnp.arange(SUB * LANES, dtype=jnp.int32) % SUB)[None, :]
    return wide.reshape(-1, LANES)


def _group_forward(x, p_l, rope_tab, mix_prm, moe_prm, ple_prm):
    b, s, d = x.shape
    n_tok = b * s
    h1, hn_rows, route, gates, counts = _mixer_call(x, rope_tab, mix_prm)
    pos, block_info, n_slots = _routing(route, counts[:, 0], n_tok)
    slot_rows = _expand_rows(pos)
    xb = _sc_dispatch(hn_rows, slot_rows, n_slots * SUB)
    y = _moe_call(*block_info, xb, *moe_prm)
    gathered = _sc_gather(y, slot_rows)
    out = _combine_call(gathered, h1.reshape(n_tok, d), gates.reshape(n_tok, LANES),
                        p_l.reshape(n_tok, PLE_DIM), *ple_prm)
    return out.reshape(b, s, d)


def kernel(x_prompt, x_sample, p_prompt, p_sample, norm_mix, w_in, q_gain, k_gain, attn_sink, gmlp_v_gain, gmlp_w_s, gmlp_b_s, w_branch, w_out, norm_ffn, w_router, b_router, w_gate_up, b_gate_up, w_down, b_down, norm_ple, w_ple_gate, w_ple_proj):
    depth = norm_mix.shape[0]
    hp, hs = x_prompt, x_sample
    for l in range(depth):
        row = lambda a: a.reshape(1, -1)
        blockdiag = jnp.kron(jnp.eye(N_Q_HEADS, dtype=F32),
                             jnp.full((HEAD_DIM, HEAD_DIM), 1.0 / HEAD_DIM, F32)).astype(BF16)
        sink_heads = attn_sink[l].reshape(N_KV_HEADS, GQA_GROUP)[:, jnp.array(HEAD_ORDER)]
        sink_col = jnp.repeat(sink_heads, ATTN_BLOCK, axis=1)[..., None]
        sink_rows = jnp.where(jnp.arange(LANES) == 0, sink_col, -jnp.inf)
        first_row = (jnp.arange(LANES) == 0)[:, None]
        upper = (jnp.arange(LANES) >= HEAD_DIM)[None, :]
        sink_values = jnp.stack([first_row & upper, first_row & ~upper]).astype(BF16)
        wcat = jnp.transpose(gmlp_w_s[l], (1, 0, 2)).reshape(GMLP_CHUNK, GMLP_GROUPS * GMLP_CHUNK).astype(BF16)
        bias_full = jnp.repeat(gmlp_b_s[l].T, GMLP_GROUP_DIM, axis=1)
        wr = jnp.pad(w_router[l], ((0, 0), (0, ROUTER_PAD - N_EXPERTS)))
        wr_hi = wr.astype(BF16)
        wr_lo = (wr - wr_hi.astype(F32)).astype(BF16)
        br = jnp.pad(b_router[l], (0, ROUTER_PAD - N_EXPERTS)).reshape(1, ROUTER_PAD)
        tri = (jnp.arange(MIX_TILE)[:, None] < jnp.arange(MIX_TILE)[None, :]).astype(BF16)
        mix_prm = (
            row(norm_mix[l]), w_in[l].astype(BF16),
            row(jnp.tile(q_gain[l], N_Q_HEADS)), row(jnp.tile(k_gain[l], N_KV_HEADS)),
            sink_rows, sink_values, row(gmlp_v_gain[l]), blockdiag, wcat, bias_full,
            w_branch[l].astype(BF16), w_out[l].astype(BF16), row(norm_ffn[l]),
            jnp.concatenate([wr_hi, wr_lo], axis=1), br, tri,
        )
        moe_prm = (
            w_gate_up[l], b_gate_up[l].reshape(N_EXPERTS, 1, 2 * D_FF),
            w_down[l], b_down[l].reshape(N_EXPERTS, 1, D_MODEL),
        )
        ple_prm = (row(norm_ple[l]), w_ple_gate[l].astype(BF16), w_ple_proj[l].astype(BF16))
        rope_tab = _rope_table(max(hp.shape[1], hs.shape[1]))
        hp = _group_forward(hp, p_prompt[l], rope_tab, mix_prm, moe_prm, ple_prm)
        hs = _group_forward(hs, p_sample[l], rope_tab, mix_prm, moe_prm, ple_prm)
    return (hp.astype(x_prompt.dtype), hs.astype(x_sample.dtype))
```

```python
import functools

import jax
import jax.numpy as jnp
import numpy as np
from jax import lax
from jax.experimental import pallas as pl
from jax.experimental.pallas import tpu as pltpu
from jax.experimental.pallas import tpu_sc as plsc

D_MODEL = 1024
HEAD_DIM = 64
N_Q_HEADS = 8
N_KV_HEADS = 2
GQA_GROUP = N_Q_HEADS // N_KV_HEADS
HEAD_ORDER = (0, 2, 1, 3)
ATTN_WIDTH = N_Q_HEADS * HEAD_DIM
KV_WIDTH = N_KV_HEADS * HEAD_DIM
WINDOW = 128
ATTN_BLOCK = 128
ROPE_THETA = 500000.0
ROPE_DIM = HEAD_DIM // 4
GMLP_WIDTH = D_MODEL // 2
GMLP_GROUPS = 8
GMLP_GROUP_DIM = GMLP_WIDTH // GMLP_GROUPS
GMLP_CHUNK = 128
N_BRANCH = 2
IN_WIDTH = ATTN_WIDTH + 2 * KV_WIDTH + 2 * GMLP_WIDTH + N_BRANCH * D_MODEL
N_EXPERTS = 32
TOP_K = 4
D_FF = D_MODEL
SWIGLU_ALPHA = 1.702
SWIGLU_LIMIT = 7.0
PLE_DIM = 256
EPS = 1e-6

Q_OFF = 0
K_OFF = ATTN_WIDTH
V_OFF = K_OFF + KV_WIDTH
U_OFF = V_OFF + KV_WIDTH
VG_OFF = U_OFF + GMLP_WIDTH
GL_OFF = VG_OFF + GMLP_WIDTH

LANES = 128
PACK_WORDS = D_MODEL // 2
SUB = PACK_WORDS // LANES
MIX_TILE = 512
PROJ_COLS = 256
MOE_BLOCK = 512
MOE_COLS = 256
CMB_TILE = 1024
ROUTER_PAD = 128
SC_CHUNK = 128
DISPATCH_TOKENS = 64
GATHER_STREAMS = 4
VMEM_LIMIT = 56 * 1024 * 1024

BF16 = jnp.bfloat16
F32 = jnp.float32
U32 = jnp.uint32


def _rms(x, gain):
    return x * lax.rsqrt(jnp.mean(x * x, axis=-1, keepdims=True) + EPS) * gain


def _gelu(x):
    return 0.5 * x * (1.0 + lax.erf(x * np.float32(np.sqrt(0.5))))


def _sigmoid(x):
    return 0.5 * jnp.tanh(0.5 * x) + 0.5


def _head_rms(x, blockdiag, gain):
    ms = jnp.dot((x * x).astype(BF16), blockdiag, preferred_element_type=F32)
    return x * lax.rsqrt(ms + EPS) * gain


def _rope(x, cos, sin_lo, sin_hi):
    w = x.shape[-1]
    return x * cos + pltpu.roll(x, w - ROPE_DIM // 2, 1) * sin_lo + pltpu.roll(x, ROPE_DIM // 2, 1) * sin_hi


def _value_blocks(v):
    swapped = pltpu.roll(v, HEAD_DIM, 1)
    low = lax.broadcasted_iota(jnp.int32, v.shape, 1) < HEAD_DIM
    blocks = [jnp.where(low, v, 1.0), jnp.where(low, 1.0, swapped), jnp.where(low, swapped, 1.0), jnp.where(low, 1.0, v)]
    return jnp.concatenate(blocks, axis=1).astype(BF16)


def _tile_lanes(t, reps):
    return t if reps == 1 else jnp.concatenate([t] * reps, axis=-1)


def _store_packed(ref, x):
    rows = x.shape[0]
    hi = lax.bitcast_convert_type(x[:, :PACK_WORDS].astype(BF16).astype(F32), U32)
    lo = lax.bitcast_convert_type(x[:, PACK_WORDS:].astype(BF16).astype(F32), U32)
    words = hi | (lo >> 16)
    for j in range(SUB):
        ref[pl.ds(j, rows, stride=SUB), :] = words[:, j * LANES:(j + 1) * LANES]


def _load_packed(ref, rows):
    words = jnp.concatenate([ref[pl.ds(j, rows, stride=SUB), :] for j in range(SUB)], axis=1)
    left = lax.bitcast_convert_type(words & np.uint32(0xFFFF0000), F32)
    right = lax.bitcast_convert_type(words << 16, F32)
    return jnp.concatenate([left, right], axis=1)


def _mixer_kernel(x_ref, xn_ref, rp_ref, rpn_ref,
                  nmix_ref, win_ref, qg_ref, kg_ref, sink_ref, sinkv_ref, vgain_ref, bdq_ref,
                  wcat_ref, bias_ref, wbr_ref, wout_ref, nffn_ref, wr_ref, br_ref, tri_ref,
                  h1_ref, hn_ref, route_ref, gate_ref, cnt_out_ref,
                  z_ref, q_ref, k_ref, v_ref, attn_ref, gm_ref, cnt_ref, *, n_blocks_seq):
    ts = MIX_TILE
    i = pl.program_id(1)
    x = x_ref[0]

    @pl.when(i > 0)
    def _():
        k_ref[0:ATTN_BLOCK, :] = k_ref[ts:ts + ATTN_BLOCK, :]
        v_ref[0:ATTN_BLOCK, :] = v_ref[ts:ts + ATTN_BLOCK, :]

    @pl.when(i == 0)
    def _():
        k_ref[0:ATTN_BLOCK, :] = jnp.zeros((ATTN_BLOCK, k_ref.shape[1]), BF16)
        v_ref[0:ATTN_BLOCK, :] = jnp.zeros((ATTN_BLOCK, v_ref.shape[1]), BF16)

    xn = _rms(x, nmix_ref[...]).astype(BF16)

    def project(lo, hi):
        z_ref[:, lo:hi] = jnp.dot(xn, win_ref[:, lo:hi], preferred_element_type=F32)

    project(Q_OFF, U_OFF)
    n_items = (ts // ATTN_BLOCK) * N_KV_HEADS
    later = [(U_OFF + t * PROJ_COLS, U_OFF + (t + 1) * PROJ_COLS) for t in range((IN_WIDTH - U_OFF) // PROJ_COLS)]
    per_item = [len(later) // n_items + (1 if n < len(later) % n_items else 0) for n in range(n_items)]

    cos = rp_ref[:, 0:LANES]
    sin_lo = rp_ref[:, LANES:2 * LANES]
    sin_hi = rp_ref[:, 2 * LANES:3 * LANES]
    bdq = bdq_ref[...]
    bdk = bdq_ref[0:KV_WIDTH, 0:KV_WIDTH]

    q = _head_rms(z_ref[:, Q_OFF:Q_OFF + ATTN_WIDTH], bdq, qg_ref[...])
    reps = ATTN_WIDTH // LANES
    q = _rope(q, _tile_lanes(cos, reps), _tile_lanes(sin_lo, reps), _tile_lanes(sin_hi, reps))
    q_ref[...] = (q * (HEAD_DIM ** -0.5)).astype(BF16)

    k = _head_rms(z_ref[:, K_OFF:K_OFF + KV_WIDTH], bdk, kg_ref[...])
    k_ref[ATTN_BLOCK:ATTN_BLOCK + ts, :] = _rope(k, cos, sin_lo, sin_hi).astype(BF16)
    v_ref[ATTN_BLOCK:ATTN_BLOCK + ts, :] = _value_blocks(z_ref[:, V_OFF:V_OFF + KV_WIDTH])

    xhn = _rms(xn_ref[0], nmix_ref[...]).astype(BF16)
    zh = jnp.dot(xhn, win_ref[:, K_OFF:K_OFF + 2 * KV_WIDTH], preferred_element_type=F32)
    kh = _head_rms(zh[:, 0:KV_WIDTH], bdk, kg_ref[...])
    kh = _rope(kh, rpn_ref[:, 0:LANES], rpn_ref[:, LANES:2 * LANES], rpn_ref[:, 2 * LANES:3 * LANES])
    k_ref[ATTN_BLOCK + ts:2 * ATTN_BLOCK + ts, :] = kh.astype(BF16)
    v_ref[ATTN_BLOCK + ts:2 * ATTN_BLOCK + ts, :] = _value_blocks(zh[:, KV_WIDTH:2 * KV_WIDTH])

    rows = GQA_GROUP * ATTN_BLOCK
    keys = 3 * ATTN_BLOCK
    r = lax.broadcasted_iota(jnp.int32, (rows, keys), 0) % ATTN_BLOCK
    c = lax.broadcasted_iota(jnp.int32, (rows, keys), 1)
    band = (c >= r) & (c <= r + 2 * WINDOW)
    low = lax.broadcasted_iota(jnp.int32, (ATTN_BLOCK, 2 * HEAD_DIM), 1) < HEAD_DIM
    pair = 2 * ATTN_BLOCK
    for qb in range(ts // ATTN_BLOCK):
        gb = i * (ts // ATTN_BLOCK) + qb
        lo = jnp.where(gb == 0, ATTN_BLOCK, 0)
        hi = jnp.where(gb == n_blocks_seq - 1, 2 * ATTN_BLOCK, keys)
        valid = band & (c >= lo) & (c < hi)
        r0 = qb * ATTN_BLOCK
        for j in range(N_KV_HEADS):
            for _ in range(per_item[qb * N_KV_HEADS + j]):
                project(*later.pop(0))
            q4 = jnp.concatenate(
                [q_ref[r0:r0 + ATTN_BLOCK, (GQA_GROUP * j + g) * HEAD_DIM:(GQA_GROUP * j + g + 1) * HEAD_DIM]
                 for g in HEAD_ORDER], axis=0)
            kw = k_ref[r0:r0 + keys, j * HEAD_DIM:(j + 1) * HEAD_DIM]
            s = lax.dot_general(q4, kw, (((1,), (1,)), ((), ())), preferred_element_type=F32)
            s = jnp.concatenate([jnp.where(valid, s, -jnp.inf), sink_ref[j]], axis=1)
            p = jnp.exp(s - jnp.max(s, axis=-1, keepdims=True)).astype(BF16)
            outs = []
            for par in range(2):
                vw = jnp.concatenate([v_ref[r0:r0 + keys, (2 * j + par) * LANES:(2 * j + par + 1) * LANES],
                                      sinkv_ref[par]], axis=0)
                o = jnp.dot(p[par * pair:(par + 1) * pair], vw, preferred_element_type=F32)
                outs.append(o / pltpu.roll(o, HEAD_DIM, 1))
            for a in range(GQA_GROUP // 2):
                both = jnp.where(low, outs[0][a * ATTN_BLOCK:(a + 1) * ATTN_BLOCK],
                                 outs[1][a * ATTN_BLOCK:(a + 1) * ATTN_BLOCK])
                h0 = GQA_GROUP * j + 2 * a
                attn_ref[r0:r0 + ATTN_BLOCK, h0 * HEAD_DIM:(h0 + 2) * HEAD_DIM] = both.astype(BF16)

    vgn_all = _head_rms(_gelu(z_ref[:, VG_OFF:VG_OFF + GMLP_WIDTH]), bdq, vgain_ref[...])
    half = GMLP_WIDTH // 2
    gpm = half // GMLP_GROUP_DIM
    lane_grp = lax.broadcasted_iota(jnp.int32, (GMLP_CHUNK, half), 1) // GMLP_GROUP_DIM
    mixed = []
    for ch in range(ts // GMLP_CHUNK):
        c0 = ch * GMLP_CHUNK
        parts = []
        for nt in range(2):
            part = vgn_all[c0:c0 + GMLP_CHUNK, nt * half:(nt + 1) * half]
            vexp = jnp.concatenate(
                [jnp.where(lane_grp == gl, part, 0.0).astype(BF16) for gl in range(gpm)], axis=0)
            wpart = wcat_ref[:, nt * gpm * GMLP_CHUNK:(nt + 1) * gpm * GMLP_CHUNK]
            parts.append(jnp.dot(wpart, vexp, preferred_element_type=F32))
        mixed.append(jnp.concatenate(parts, axis=-1))
    for ch in range(ts // GMLP_CHUNK):
        c0 = ch * GMLP_CHUNK
        u = _gelu(z_ref[c0:c0 + GMLP_CHUNK, U_OFF:U_OFF + GMLP_WIDTH])
        gm_ref[c0:c0 + GMLP_CHUNK, :] = (u * (mixed[ch] + bias_ref[...])).astype(BF16)

    ya = jnp.dot(attn_ref[...], wbr_ref[0], preferred_element_type=F32)
    yg = jnp.dot(gm_ref[...], wbr_ref[1], preferred_element_type=F32)
    ga = _sigmoid(z_ref[:, GL_OFF:GL_OFF + D_MODEL])
    gg = _sigmoid(z_ref[:, GL_OFF + D_MODEL:GL_OFF + 2 * D_MODEL])
    merged = (ga * ya + gg * yg).astype(BF16)
    h1 = x + jnp.dot(merged, wout_ref[...], preferred_element_type=F32)
    h1_ref[0] = h1

    hn = _rms(h1, nffn_ref[...])
    _store_packed(hn_ref, hn)

    hn_hi = hn.astype(BF16)
    hn_lo = (hn - hn_hi.astype(F32)).astype(BF16)
    by_hi = jnp.dot(hn_hi, wr_ref[...], preferred_element_type=F32)
    by_lo = jnp.dot(hn_lo, wr_ref[:, 0:ROUTER_PAD], preferred_element_type=F32)
    logits = by_hi[:, 0:ROUTER_PAD] + by_hi[:, ROUTER_PAD:2 * ROUTER_PAD] + by_lo + br_ref[...]
    lt = jnp.transpose(logits)[0:N_EXPERTS, :]
    eid = lax.broadcasted_iota(jnp.int32, lt.shape, 0)
    vals, ids = [], []
    for _ in range(TOP_K):
        mx = jnp.max(lt, axis=0, keepdims=True)
        am = jnp.min(jnp.where(lt == mx, eid, N_EXPERTS), axis=0, keepdims=True)
        vals.append(mx)
        ids.append(am)
        lt = jnp.where(eid == am, -jnp.inf, lt)
    ex = [jnp.exp(v - vals[0]) for v in vals]
    tot = ex[0] + ex[1] + ex[2] + ex[3]
    g8 = jnp.concatenate([e / tot for e in ex] + [jnp.zeros_like(tot)] * (8 - TOP_K), axis=0)
    gates = jnp.concatenate([g8, jnp.zeros((LANES - 8, ts), F32)], axis=0)
    gate_ref[0] = jnp.transpose(gates)

    @pl.when((pl.program_id(0) == 0) & (i == 0))
    def _():
        cnt_ref[...] = jnp.zeros_like(cnt_ref)

    sel = [eid == a for a in ids]
    member = (sel[0] | sel[1] | sel[2] | sel[3]).astype(F32)
    before = jnp.dot(member.astype(BF16), tri_ref[...], preferred_element_type=F32) + cnt_ref[:, 0:1]
    ranks = [jnp.sum(jnp.where(s_, before, 0.0), axis=0, keepdims=True).astype(jnp.int32) for s_ in sel]
    route_ref[...] = jnp.concatenate(ids + ranks, axis=0)
    cnt_ref[...] = cnt_ref[...] + jnp.sum(member, axis=1, keepdims=True)
    cnt_out_ref[...] = cnt_ref[...]


def _mixer_call(x, rope_tab, prm):
    b, s, d = x.shape
    ts = MIX_TILE
    nt = s // ts
    nb = s // ATTN_BLOCK
    per = ts // ATTN_BLOCK
    const2 = lambda bi, i: (0, 0)
    const3 = lambda bi, i: (0, 0, 0)

    def wspec(arr):
        return pl.BlockSpec(arr.shape, const2 if arr.ndim == 2 else const3)

    in_specs = [
        pl.BlockSpec((1, ts, d), lambda bi, i: (bi, i, 0)),
        pl.BlockSpec((1, ATTN_BLOCK, d), lambda bi, i: (bi, jnp.minimum((i + 1) * per, nb - 1), 0)),
        pl.BlockSpec((ts, 3 * LANES), lambda bi, i: (i, 0)),
        pl.BlockSpec((ATTN_BLOCK, 3 * LANES), lambda bi, i: (jnp.minimum((i + 1) * per, nb - 1), 0)),
    ] + [wspec(a) for a in prm]
    out_shape = [
        jax.ShapeDtypeStruct((b, s, d), F32),
        jax.ShapeDtypeStruct((b * s * SUB, LANES), U32),
        jax.ShapeDtypeStruct((2 * TOP_K, b * s), jnp.int32),
        jax.ShapeDtypeStruct((b, s, LANES), F32),
        jax.ShapeDtypeStruct((N_EXPERTS, LANES), F32),
    ]
    out_specs = [
        pl.BlockSpec((1, ts, d), lambda bi, i: (bi, i, 0)),
        pl.BlockSpec((ts * SUB, LANES), lambda bi, i: (bi * nt + i, 0)),
        pl.BlockSpec((2 * TOP_K, ts), lambda bi, i: (0, bi * nt + i)),
        pl.BlockSpec((1, ts, LANES), lambda bi, i: (bi, i, 0)),
        pl.BlockSpec((N_EXPERTS, LANES), const2),
    ]
    scratch = [
        pltpu.VMEM((ts, IN_WIDTH), F32),
        pltpu.VMEM((ts, ATTN_WIDTH), BF16),
        pltpu.VMEM((ts + 2 * ATTN_BLOCK, KV_WIDTH), BF16),
        pltpu.VMEM((ts + 2 * ATTN_BLOCK, 2 * N_KV_HEADS * LANES), BF16),
        pltpu.VMEM((ts, ATTN_WIDTH), BF16),
        pltpu.VMEM((ts, GMLP_WIDTH), BF16),
        pltpu.VMEM((N_EXPERTS, LANES), F32),
    ]
    return pl.pallas_call(
        functools.partial(_mixer_kernel, n_blocks_seq=nb),
        grid=(b, nt),
        in_specs=in_specs,
        out_specs=out_specs,
        out_shape=out_shape,
        scratch_shapes=scratch,
        compiler_params=pltpu.CompilerParams(
            dimension_semantics=("arbitrary", "arbitrary"), vmem_limit_bytes=VMEM_LIMIT),
        name="mixer",
    )(x, x, rope_tab, rope_tab, *prm)


def _sc_workers():
    info = plsc.get_sparse_core_info()
    return info.num_cores, info.num_cores * info.num_subcores


def _sc_dispatch(rows2d, slot_rows, n_out_rows):
    n_cores, n_workers = _sc_workers()
    chunk_rows = DISPATCH_TOKENS * SUB
    halves = chunk_rows // SC_CHUNK
    n_chunks = rows2d.shape[0] // chunk_rows
    per_w = n_chunks // n_workers
    idx_rows_per_choice = rows2d.shape[0] // SC_CHUNK
    mesh = plsc.VectorSubcoreMesh(core_axis_name="c", subcore_axis_name="s")

    @functools.partial(
        pl.kernel, mesh=mesh,
        out_type=jax.ShapeDtypeStruct((n_out_rows, LANES), rows2d.dtype),
        scratch_types=[pltpu.VMEM((TOP_K * halves, SC_CHUNK), jnp.int32),
                       pltpu.VMEM((chunk_rows, LANES), rows2d.dtype),
                       pltpu.SemaphoreType.DMA, pltpu.SemaphoreType.DMA],
    )
    def k(src_hbm, idx_hbm, out_hbm, idx_v, rows_v, sem_in, sem_out):
        wid = lax.axis_index("s") * n_cores + lax.axis_index("c")

        @pl.loop(0, per_w)
        def _(j):
            c = wid * per_w + j
            loads = [pltpu.async_copy(src_hbm.at[pl.ds(c * chunk_rows, chunk_rows)], rows_v, sem_in)]
            loads += [
                pltpu.async_copy(idx_hbm.at[pl.ds(kk * idx_rows_per_choice + c * halves, halves)],
                                 idx_v.at[pl.ds(kk * halves, halves)], sem_in)
                for kk in range(TOP_K)]
            for cp in loads:
                cp.wait()
            copies = [
                pltpu.async_copy(rows_v.at[pl.ds((q % halves) * SC_CHUNK, SC_CHUNK)],
                                 out_hbm.at[idx_v.at[q]], sem_out)
                for q in range(TOP_K * halves)]
            for cp in copies:
                cp.wait()

    return k(rows2d, slot_rows)


def _sc_gather(table2d, idx2d):
    n_cores, n_workers = _sc_workers()
    n = idx2d.shape[0] * SC_CHUNK
    per_w = n // n_workers
    step_rows = GATHER_STREAMS * SC_CHUNK
    idx_tile = 8
    mesh = plsc.VectorSubcoreMesh(core_axis_name="c", subcore_axis_name="s")

    @functools.partial(
        pl.kernel, mesh=mesh,
        out_type=jax.ShapeDtypeStruct((n, LANES), table2d.dtype),
        scratch_types=[pltpu.VMEM((idx_tile, SC_CHUNK), jnp.int32),
                       pltpu.VMEM((step_rows, LANES), table2d.dtype),
                       pltpu.SemaphoreType.DMA],
    )
    def k(table_hbm, idx_hbm, out_hbm, idx_v, rows_v, sem):
        wid = lax.axis_index("s") * n_cores + lax.axis_index("c")

        @pl.loop(0, per_w // (idx_tile * SC_CHUNK))
        def _(j):
            row0 = pl.multiple_of(wid * (per_w // SC_CHUNK) + j * idx_tile, idx_tile)
            pltpu.sync_copy(idx_hbm.at[pl.ds(row0, idx_tile)], idx_v)
            for part in range(idx_tile // GATHER_STREAMS):
                copies = [pltpu.async_copy(table_hbm.at[idx_v.at[part * GATHER_STREAMS + q]],
                                           rows_v.at[pl.ds(q * SC_CHUNK, SC_CHUNK)], sem)
                          for q in range(GATHER_STREAMS)]
                for cp in copies:
                    cp.wait()
                pltpu.sync_copy(rows_v, out_hbm.at[pl.ds((row0 + part * GATHER_STREAMS) * SC_CHUNK, step_rows)])

    return k(table2d, idx2d)


def _expert_weight_copies(wgu_hbm, wd_hbm, wgu_buf, wd_buf, sem, expert, slot):
    return (pltpu.make_async_copy(wgu_hbm.at[expert], wgu_buf.at[slot], sem.at[0, slot]),
            pltpu.make_async_copy(wd_hbm.at[expert], wd_buf.at[slot], sem.at[1, slot]))


def _moe_kernel(be_ref, nv_ref, nxt_ref, par_ref, x_ref, wgu_hbm, bgu_ref, wd_hbm, bd_ref, y_ref,
                wgu_buf, wd_buf, sem, wgu_s, wd_s, slab):
    i = pl.program_id(0)
    nv = nv_ref[i]
    slot = par_ref[i]

    cw = MOE_COLS
    n_chunks = D_FF // cw
    copies = functools.partial(_expert_weight_copies, wgu_hbm, wd_hbm, wgu_buf, wd_buf, sem)

    @pl.when(i == 0)
    def _():
        for cp in copies(be_ref[0], slot):
            cp.start()

    @pl.when((i == 0) | (be_ref[i] != be_ref[jnp.maximum(i - 1, 0)]))
    def _():
        for cp in copies(be_ref[i], slot):
            cp.wait()

        @pl.when(nxt_ref[i] >= 0)
        def _():
            for cp in copies(nxt_ref[i], 1 - slot):
                cp.start()

        for c in range(n_chunks):
            wgu_s[:, 2 * c * cw:(2 * c + 1) * cw] = wgu_buf[slot, :, c * cw:(c + 1) * cw].astype(BF16)
            wgu_s[:, (2 * c + 1) * cw:(2 * c + 2) * cw] = (
                wgu_buf[slot, :, D_FF + c * cw:D_FF + (c + 1) * cw].astype(BF16))
        half = D_FF // 2
        for c in range(D_MODEL // LANES):
            slab[c, pl.ds(0, half, stride=2), :] = wd_buf[slot, 0:half, c * LANES:(c + 1) * LANES]
            slab[c, pl.ds(1, half, stride=2), :] = wd_buf[slot, half:D_FF, c * LANES:(c + 1) * LANES]
        for c in range(D_MODEL // LANES):
            wd_s[:, c * LANES:(c + 1) * LANES] = slab[c].astype(BF16)

    expert = be_ref[i]

    def run_expert(rows):
        x = _load_packed(x_ref, rows)
        live = lax.broadcasted_iota(jnp.int32, x.shape, 0) < nv
        xe = jnp.where(live, x, 0.0).astype(BF16)
        even = (lax.broadcasted_iota(jnp.int32, (rows, cw), 1) % 2) == 0
        y = bd_ref[expert]

        def gate_up(c):
            return jnp.dot(xe, wgu_s[:, 2 * c * cw:(2 * c + 2) * cw], preferred_element_type=F32)

        h_next = gate_up(0)
        for c in range(n_chunks):
            h = h_next
            if c + 1 < n_chunks:
                h_next = gate_up(c + 1)
            h_a = h[:, 0:cw] + bgu_ref[expert, :, c * cw:(c + 1) * cw]
            h_b = h[:, cw:2 * cw] + bgu_ref[expert, :, D_FF + c * cw:D_FF + (c + 1) * cw]
            gate = jnp.where(even, h_a, pltpu.roll(h_b, 1, 1))
            up = jnp.where(even, pltpu.roll(h_a, cw - 1, 1), h_b)
            gate = jnp.minimum(gate, SWIGLU_LIMIT)
            up = jnp.clip(up, -SWIGLU_LIMIT, SWIGLU_LIMIT)
            act = (up + 1.0) * (gate * jax.nn.sigmoid(SWIGLU_ALPHA * gate))
            y = y + jnp.dot(act.astype(BF16), wd_s[c * cw:(c + 1) * cw, :], preferred_element_type=F32)
        _store_packed(y_ref, y)

    @pl.when(nv > MOE_BLOCK // 2)
    def _():
        run_expert(MOE_BLOCK)

    @pl.when((nv > 0) & (nv <= MOE_BLOCK // 2))
    def _():
        run_expert(MOE_BLOCK // 2)


def _moe_call(block_e, n_valid, next_e, parity, xb, wgu, bgu, wd, bd):
    n_blocks = block_e.shape[0]
    d = D_MODEL
    grid_spec = pltpu.PrefetchScalarGridSpec(
        num_scalar_prefetch=4,
        grid=(n_blocks,),
        in_specs=[
            pl.BlockSpec((MOE_BLOCK * SUB, LANES), lambda i, be, nv, nx, pr: (i, 0)),
            pl.BlockSpec(memory_space=pl.ANY),
            pl.BlockSpec((N_EXPERTS, 1, 2 * D_FF), lambda i, be, nv, nx, pr: (0, 0, 0)),
            pl.BlockSpec(memory_space=pl.ANY),
            pl.BlockSpec((N_EXPERTS, 1, d), lambda i, be, nv, nx, pr: (0, 0, 0)),
        ],
        out_specs=pl.BlockSpec((MOE_BLOCK * SUB, LANES), lambda i, be, nv, nx, pr: (i, 0)),
        scratch_shapes=[pltpu.VMEM((2, d, 2 * D_FF), F32), pltpu.VMEM((2, D_FF, d), F32),
                        pltpu.SemaphoreType.DMA((2, 2)),
                        pltpu.VMEM((d, 2 * D_FF), BF16), pltpu.VMEM((D_FF, d), BF16),
                        pltpu.VMEM((d // LANES, D_FF, LANES), F32)],
    )
    return pl.pallas_call(
        _moe_kernel,
        grid_spec=grid_spec,
        out_shape=jax.ShapeDtypeStruct((n_blocks * MOE_BLOCK * SUB, LANES), U32),
        compiler_params=pltpu.CompilerParams(
            dimension_semantics=("arbitrary",), vmem_limit_bytes=VMEM_LIMIT),
        name="moe",
    )(block_e, n_valid, next_e, parity, xb, wgu, bgu, wd, bd)


def _combine_kernel(g0_ref, g1_ref, g2_ref, g3_ref, h1_ref, gate_ref, p_ref, nple_ref, wpg_ref, wpp_ref, o_ref):
    g = gate_ref[...]
    moe = jnp.zeros((CMB_TILE, D_MODEL), F32)
    for kk, g_ref in enumerate((g0_ref, g1_ref, g2_ref, g3_ref)):
        moe = moe + _load_packed(g_ref, CMB_TILE) * g[:, kk:kk + 1]
    h2 = h1_ref[...] + moe
    hp = _rms(h2, nple_ref[...]).astype(BF16)
    gate = jax.nn.sigmoid(jnp.dot(hp, wpg_ref[...], preferred_element_type=F32))
    proj = jnp.dot(p_ref[...].astype(BF16), wpp_ref[...], preferred_element_type=F32)
    o_ref[...] = h2 + gate * proj


def _combine_call(gathered, h1_flat, gates, p_flat, nple, wpg, wpp):
    n_tok, d = h1_flat.shape
    tc = CMB_TILE
    n_tiles = n_tok // tc
    row = lambda i: (i, 0)
    const = lambda i: (0, 0)
    g_specs = [pl.BlockSpec((tc * SUB, LANES), functools.partial(lambda i, kk: (kk * n_tiles + i, 0), kk=kk))
               for kk in range(TOP_K)]
    return pl.pallas_call(
        _combine_kernel,
        grid=(n_tiles,),
        in_specs=g_specs + [
            pl.BlockSpec((tc, d), row),
            pl.BlockSpec((tc, LANES), row),
            pl.BlockSpec((tc, PLE_DIM), row),
            pl.BlockSpec((1, d), const),
            pl.BlockSpec((d, d), const),
            pl.BlockSpec((PLE_DIM, d), const),
        ],
        out_specs=pl.BlockSpec((tc, d), row),
        out_shape=jax.ShapeDtypeStruct((n_tok, d), F32),
        compiler_params=pltpu.CompilerParams(
            dimension_semantics=("arbitrary",), vmem_limit_bytes=VMEM_LIMIT),
        name="combine",
    )(gathered, gathered, gathered, gathered, h1_flat, gates, p_flat, nple, wpg, wpp)


def _rope_table(s):
    half = ROPE_DIM // 2
    inv_freq = jnp.power(ROPE_THETA, -jnp.arange(half, dtype=F32) * (2.0 / ROPE_DIM))
    ang = jnp.arange(s, dtype=F32)[:, None] * inv_freq[None, :]
    base = jnp.concatenate([jnp.cos(ang), jnp.sin(ang)], axis=-1)
    place = np.zeros((2 * half, 3 * LANES), np.float32)
    ones = np.zeros((1, 3 * LANES), np.float32)
    for lane in range(3 * LANES):
        kind, d = lane // LANES, lane % HEAD_DIM
        if kind == 0 and d < ROPE_DIM:
            place[d % half, lane] = 1.0
        elif kind == 0:
            ones[0, lane] = 1.0
        elif kind == 1 and d < half:
            place[half + d, lane] = -1.0
        elif kind == 2 and half <= d < ROPE_DIM:
            place[d, lane] = 1.0
    return jnp.dot(base, place, precision=lax.Precision.HIGHEST) + ones


def _routing(route, counts, n_tok):
    ids = route[0:TOP_K]
    ranks = route[TOP_K:2 * TOP_K]
    counts = counts.astype(jnp.int32)
    padded = (counts + MOE_BLOCK - 1) // MOE_BLOCK * MOE_BLOCK
    pends = jnp.cumsum(padded)
    pstarts = pends - padded
    pos = ranks
    for e in range(N_EXPERTS):
        pos = pos + jnp.where(ids == e, pstarts[e], 0)
    n_slots = -(-(n_tok * TOP_K) // MOE_BLOCK) * MOE_BLOCK + N_EXPERTS * MOE_BLOCK
    n_blocks = n_slots // MOE_BLOCK
    first = jnp.arange(n_blocks, dtype=jnp.int32) * MOE_BLOCK
    block_e = jnp.clip(jnp.sum((first[:, None] >= pends[None, :]).astype(jnp.int32), axis=1), 0, N_EXPERTS - 1)
    own = block_e[:, None] == jnp.arange(N_EXPERTS, dtype=jnp.int32)[None, :]
    left = jnp.sum(jnp.where(own, (counts + pstarts)[None, :], 0), axis=1) - first
    n_valid = jnp.clip(left, 0, MOE_BLOCK)
    later = jnp.where(block_e[None, :] > block_e[:, None], block_e[None, :], N_EXPERTS)
    next_e = jnp.min(later, axis=1)
    next_e = jnp.where(next_e == N_EXPERTS, -1, next_e)
    present = jnp.any(own, axis=0)
    runs_before = jnp.sum((present[None, :] & (jnp.arange(N_EXPERTS)[None, :] < block_e[:, None])).astype(jnp.int32),
                          axis=1)
    parity = runs_before % 2
    return pos, (block_e, n_valid.astype(jnp.int32), next_e.astype(jnp.int32), parity.astype(jnp.int32)), n_slots


def _expand_rows(pos):
    spread = (jnp.arange(SUB * LANES, dtype=jnp.int32)[None, :] // SUB
              == jnp.arange(LANES, dtype=jnp.int32)[:, None]).astype(F32)
    wide = jnp.dot(pos.reshape(-1, LANES).astype(F32), spread, precision=lax.Precision.HIGHEST)
    wide = wide.astype(jnp.int32) * SUB + (jnp.arange(SUB * LANES, dtype=jnp.int32) % SUB)[None, :]
    return wide.reshape(-1, LANES)


def _group_forward(x, p_l, rope_tab, mix_prm, moe_prm, ple_prm):
    b, s, d = x.shape
    n_tok = b * s
    h1, hn_rows, route, gates, counts = _mixer_call(x, rope_tab, mix_prm)
    pos, block_info, n_slots = _routing(route, counts[:, 0], n_tok)
    slot_rows = _expand_rows(pos)
    xb = _sc_dispatch(hn_rows, slot_rows, n_slots * SUB)
    y = _moe_call(*block_info, xb, *moe_prm)
    gathered = _sc_gather(y, slot_rows)
    out = _combine_call(gathered, h1.reshape(n_tok, d), gates.reshape(n_tok, LANES),
                        p_l.reshape(n_tok, PLE_DIM), *ple_prm)
    return out.reshape(b, s, d)


def kernel(x_prompt, x_sample, p_prompt, p_sample, norm_mix, w_in, q_gain, k_gain, attn_sink, gmlp_v_gain, gmlp_w_s, gmlp_b_s, w_branch, w_out, norm_ffn, w_router, b_router, w_gate_up, b_gate_up, w_down, b_down, norm_ple, w_ple_gate, w_ple_proj):
    depth = norm_mix.shape[0]
    hp, hs = x_prompt, x_sample
    for l in range(depth):
        row = lambda a: a.reshape(1, -1)
        blockdiag = jnp.kron(jnp.eye(N_Q_HEADS, dtype=F32),
                             jnp.full((HEAD_DIM, HEAD_DIM), 1.0 / HEAD_DIM, F32)).astype(BF16)
        sink_heads = attn_sink[l].reshape(N_KV_HEADS, GQA_GROUP)[:, jnp.array(HEAD_ORDER)]
        sink_col = jnp.repeat(sink_heads, ATTN_BLOCK, axis=1)[..., None]
        sink_rows = jnp.where(jnp.arange(LANES) == 0, sink_col, -jnp.inf)
        first_row = (jnp.arange(LANES) == 0)[:, None]
        upper = (jnp.arange(LANES) >= HEAD_DIM)[None, :]
        sink_values = jnp.stack([first_row & upper, first_row & ~upper]).astype(BF16)
        wcat = jnp.transpose(gmlp_w_s[l], (1, 0, 2)).reshape(GMLP_CHUNK, GMLP_GROUPS * GMLP_CHUNK).astype(BF16)
        bias_full = jnp.repeat(gmlp_b_s[l].T, GMLP_GROUP_DIM, axis=1)
        wr = jnp.pad(w_router[l], ((0, 0), (0, ROUTER_PAD - N_EXPERTS)))
        wr_hi = wr.astype(BF16)
        wr_lo = (wr - wr_hi.astype(F32)).astype(BF16)
        br = jnp.pad(b_router[l], (0, ROUTER_PAD - N_EXPERTS)).reshape(1, ROUTER_PAD)
        tri = (jnp.arange(MIX_TILE)[:, None] < jnp.arange(MIX_TILE)[None, :]).astype(BF16)
        mix_prm = (
            row(norm_mix[l]), w_in[l].astype(BF16),
            row(jnp.tile(q_gain[l], N_Q_HEADS)), row(jnp.tile(k_gain[l], N_KV_HEADS)),
            sink_rows, sink_values, row(gmlp_v_gain[l]), blockdiag, wcat, bias_full,
            w_branch[l].astype(BF16), w_out[l].astype(BF16), row(norm_ffn[l]),
            jnp.concatenate([wr_hi, wr_lo], axis=1), br, tri,
        )
        moe_prm = (
            w_gate_up[l], b_gate_up[l].reshape(N_EXPERTS, 1, 2 * D_FF),
            w_down[l], b_down[l].reshape(N_EXPERTS, 1, D_MODEL),
        )
        ple_prm = (row(norm_ple[l]), w_ple_gate[l].astype(BF16), w_ple_proj[l].astype(BF16))
        rope_tab = _rope_table(max(hp.shape[1], hs.shape[1]))
        hp = _group_forward(hp, p_prompt[l], rope_tab, mix_prm, moe_prm, ple_prm)
        hs = _group_forward(hs, p_sample[l], rope_tab, mix_prm, moe_prm, ple_prm)
    return (hp.astype(x_prompt.dtype), hs.astype(x_sample.dtype))
```

```python
import functools

import jax
import jax.numpy as jnp
import numpy as np
from jax import lax
from jax.experimental import pallas as pl
from jax.experimental.pallas import tpu as pltpu
from jax.experimental.pallas import tpu_sc as plsc

D_MODEL = 1024
HEAD_DIM = 64
N_Q_HEADS = 8
N_KV_HEADS = 2
GQA_GROUP = N_Q_HEADS // N_KV_HEADS
HEAD_ORDER = (0, 2, 1, 3)
ATTN_WIDTH = N_Q_HEADS * HEAD_DIM
KV_WIDTH = N_KV_HEADS * HEAD_DIM
WINDOW = 128
ATTN_BLOCK = 128
ROPE_THETA = 500000.0
ROPE_DIM = HEAD_DIM // 4
GMLP_WIDTH = D_MODEL // 2
GMLP_GROUPS = 8
GMLP_GROUP_DIM = GMLP_WIDTH // GMLP_GROUPS
GMLP_CHUNK = 128
N_BRANCH = 2
IN_WIDTH = ATTN_WIDTH + 2 * KV_WIDTH + 2 * GMLP_WIDTH + N_BRANCH * D_MODEL
N_EXPERTS = 32
TOP_K = 4
D_FF = D_MODEL
SWIGLU_ALPHA = 1.702
SWIGLU_LIMIT = 7.0
PLE_DIM = 256
EPS = 1e-6

Q_OFF = 0
K_OFF = ATTN_WIDTH
V_OFF = K_OFF + KV_WIDTH
U_OFF = V_OFF + KV_WIDTH
VG_OFF = U_OFF + GMLP_WIDTH
GL_OFF = VG_OFF + GMLP_WIDTH

LANES = 128
PACK_WORDS = D_MODEL // 2
SUB = PACK_WORDS // LANES
MIX_TILE = 512
PROJ_COLS = 256
MOE_BLOCK = 512
MOE_COLS = 256
CMB_TILE = 1024
ROUTER_PAD = 128
SC_CHUNK = 128
DISPATCH_TOKENS = 64
GATHER_STREAMS = 4
VMEM_LIMIT = 56 * 1024 * 1024

BF16 = jnp.bfloat16
F32 = jnp.float32
U32 = jnp.uint32


def _rms(x, gain):
    return x * lax.rsqrt(jnp.mean(x * x, axis=-1, keepdims=True) + EPS) * gain


def _gelu(x):
    return 0.5 * x * (1.0 + lax.erf(x * np.float32(np.sqrt(0.5))))


def _sigmoid(x):
    return 0.5 * jnp.tanh(0.5 * x) + 0.5


def _head_rms(x, blockdiag, gain):
    ms = jnp.dot((x * x).astype(BF16), blockdiag, preferred_element_type=F32)
    return x * lax.rsqrt(ms + EPS) * gain


def _rope(x, cos, sin_lo, sin_hi):
    w = x.shape[-1]
    return x * cos + pltpu.roll(x, w - ROPE_DIM // 2, 1) * sin_lo + pltpu.roll(x, ROPE_DIM // 2, 1) * sin_hi


def _value_blocks(v):
    swapped = pltpu.roll(v, HEAD_DIM, 1)
    low = lax.broadcasted_iota(jnp.int32, v.shape, 1) < HEAD_DIM
    blocks = [jnp.where(low, v, 1.0), jnp.where(low, 1.0, swapped), jnp.where(low, swapped, 1.0), jnp.where(low, 1.0, v)]
    return jnp.concatenate(blocks, axis=1).astype(BF16)


def _tile_lanes(t, reps):
    return t if reps == 1 else jnp.concatenate([t] * reps, axis=-1)


def _store_packed(ref, x):
    rows = x.shape[0]
    hi = lax.bitcast_convert_type(x[:, :PACK_WORDS].astype(BF16).astype(F32), U32)
    lo = lax.bitcast_convert_type(x[:, PACK_WORDS:].astype(BF16).astype(F32), U32)
    words = hi | (lo >> 16)
    for j in range(SUB):
        ref[pl.ds(j, rows, stride=SUB), :] = words[:, j * LANES:(j + 1) * LANES]


def _load_packed(ref, rows):
    words = jnp.concatenate([ref[pl.ds(j, rows, stride=SUB), :] for j in range(SUB)], axis=1)
    left = lax.bitcast_convert_type(words & np.uint32(0xFFFF0000), F32)
    right = lax.bitcast_convert_type(words << 16, F32)
    return jnp.concatenate([left, right], axis=1)


def _mixer_kernel(x_ref, xn_ref, rp_ref, rpn_ref,
                  nmix_ref, win_ref, qg_ref, kg_ref, sink_ref, sinkv_ref, vgain_ref, bdq_ref,
                  wcat_ref, bias_ref, wbr_ref, wout_ref, nffn_ref, wr_ref, br_ref, tri_ref,
                  h1_ref, hn_ref, route_ref, gate_ref, cnt_out_ref,
                  z_ref, q_ref, k_ref, v_ref, attn_ref, gm_ref, cnt_ref, *, n_blocks_seq):
    ts = MIX_TILE
    i = pl.program_id(1)
    x = x_ref[0]

    @pl.when(i > 0)
    def _():
        k_ref[0:ATTN_BLOCK, :] = k_ref[ts:ts + ATTN_BLOCK, :]
        v_ref[0:ATTN_BLOCK, :] = v_ref[ts:ts + ATTN_BLOCK, :]

    @pl.when(i == 0)
    def _():
        k_ref[0:ATTN_BLOCK, :] = jnp.zeros((ATTN_BLOCK, k_ref.shape[1]), BF16)
        v_ref[0:ATTN_BLOCK, :] = jnp.zeros((ATTN_BLOCK, v_ref.shape[1]), BF16)

    xn = _rms(x, nmix_ref[...]).astype(BF16)

    def project(lo, hi):
        z_ref[:, lo:hi] = jnp.dot(xn, win_ref[:, lo:hi], preferred_element_type=F32)

    project(Q_OFF, U_OFF)
    n_items = (ts // ATTN_BLOCK) * N_KV_HEADS
    later = [(U_OFF + t * PROJ_COLS, U_OFF + (t + 1) * PROJ_COLS) for t in range((IN_WIDTH - U_OFF) // PROJ_COLS)]
    per_item = [len(later) // n_items + (1 if n < len(later) % n_items else 0) for n in range(n_items)]

    cos = rp_ref[:, 0:LANES]
    sin_lo = rp_ref[:, LANES:2 * LANES]
    sin_hi = rp_ref[:, 2 * LANES:3 * LANES]
    bdq = bdq_ref[...]
    bdk = bdq_ref[0:KV_WIDTH, 0:KV_WIDTH]

    q = _head_rms(z_ref[:, Q_OFF:Q_OFF + ATTN_WIDTH], bdq, qg_ref[...])
    reps = ATTN_WIDTH // LANES
    q = _rope(q, _tile_lanes(cos, reps), _tile_lanes(sin_lo, reps), _tile_lanes(sin_hi, reps))
    q_ref[...] = (q * (HEAD_DIM ** -0.5)).astype(BF16)

    k = _head_rms(z_ref[:, K_OFF:K_OFF + KV_WIDTH], bdk, kg_ref[...])
    k_ref[ATTN_BLOCK:ATTN_BLOCK + ts, :] = _rope(k, cos, sin_lo, sin_hi).astype(BF16)
    v_ref[ATTN_BLOCK:ATTN_BLOCK + ts, :] = _value_blocks(z_ref[:, V_OFF:V_OFF + KV_WIDTH])

    xhn = _rms(xn_ref[0], nmix_ref[...]).astype(BF16)
    zh = jnp.dot(xhn, win_ref[:, K_OFF:K_OFF + 2 * KV_WIDTH], preferred_element_type=F32)
    kh = _head_rms(zh[:, 0:KV_WIDTH], bdk, kg_ref[...])
    kh = _rope(kh, rpn_ref[:, 0:LANES], rpn_ref[:, LANES:2 * LANES], rpn_ref[:, 2 * LANES:3 * LANES])
    k_ref[ATTN_BLOCK + ts:2 * ATTN_BLOCK + ts, :] = kh.astype(BF16)
    v_ref[ATTN_BLOCK + ts:2 * ATTN_BLOCK + ts, :] = _value_blocks(zh[:, KV_WIDTH:2 * KV_WIDTH])

    rows = GQA_GROUP * ATTN_BLOCK
    keys = 3 * ATTN_BLOCK
    r = lax.broadcasted_iota(jnp.int32, (rows, keys), 0) % ATTN_BLOCK
    c = lax.broadcasted_iota(jnp.int32, (rows, keys), 1)
    band = (c >= r) & (c <= r + 2 * WINDOW)
    low = lax.broadcasted_iota(jnp.int32, (ATTN_BLOCK, 2 * HEAD_DIM), 1) < HEAD_DIM
    pair = 2 * ATTN_BLOCK
    for qb in range(ts // ATTN_BLOCK):
        gb = i * (ts // ATTN_BLOCK) + qb
        lo = jnp.where(gb == 0, ATTN_BLOCK, 0)
        hi = jnp.where(gb == n_blocks_seq - 1, 2 * ATTN_BLOCK, keys)
        valid = band & (c >= lo) & (c < hi)
        r0 = qb * ATTN_BLOCK
        for j in range(N_KV_HEADS):
            for _ in range(per_item[qb * N_KV_HEADS + j]):
                project(*later.pop(0))
            q4 = jnp.concatenate(
                [q_ref[r0:r0 + ATTN_BLOCK, (GQA_GROUP * j + g) * HEAD_DIM:(GQA_GROUP * j + g + 1) * HEAD_DIM]
                 for g in HEAD_ORDER], axis=0)
            kw = k_ref[r0:r0 + keys, j * HEAD_DIM:(j + 1) * HEAD_DIM]
            s = lax.dot_general(q4, kw, (((1,), (1,)), ((), ())), preferred_element_type=F32)
            s = jnp.concatenate([jnp.where(valid, s, -jnp.inf), sink_ref[j]], axis=1)
            p = jnp.exp(s - jnp.max(s, axis=-1, keepdims=True)).astype(BF16)
            outs = []
            for par in range(2):
                vw = jnp.concatenate([v_ref[r0:r0 + keys, (2 * j + par) * LANES:(2 * j + par + 1) * LANES],
                                      sinkv_ref[par]], axis=0)
                o = jnp.dot(p[par * pair:(par + 1) * pair], vw, preferred_element_type=F32)
                outs.append(o / pltpu.roll(o, HEAD_DIM, 1))
            for a in range(GQA_GROUP // 2):
                both = jnp.where(low, outs[0][a * ATTN_BLOCK:(a + 1) * ATTN_BLOCK],
                                 outs[1][a * ATTN_BLOCK:(a + 1) * ATTN_BLOCK])
                h0 = GQA_GROUP * j + 2 * a
                attn_ref[r0:r0 + ATTN_BLOCK, h0 * HEAD_DIM:(h0 + 2) * HEAD_DIM] = both.astype(BF16)

    vgn_all = _head_rms(_gelu(z_ref[:, VG_OFF:VG_OFF + GMLP_WIDTH]), bdq, vgain_ref[...])
    half = GMLP_WIDTH // 2
    gpm = half // GMLP_GROUP_DIM
    lane_grp = lax.broadcasted_iota(jnp.int32, (GMLP_CHUNK, half), 1) // GMLP_GROUP_DIM
    mixed = []
    for ch in range(ts // GMLP_CHUNK):
        c0 = ch * GMLP_CHUNK
        parts = []
        for nt in range(2):
            part = vgn_all[c0:c0 + GMLP_CHUNK, nt * half:(nt + 1) * half]
            vexp = jnp.concatenate(
                [jnp.where(lane_grp == gl, part, 0.0).astype(BF16) for gl in range(gpm)], axis=0)
            wpart = wcat_ref[:, nt * gpm * GMLP_CHUNK:(nt + 1) * gpm * GMLP_CHUNK]
            parts.append(jnp.dot(wpart, vexp, preferred_element_type=F32))
        mixed.append(jnp.concatenate(parts, axis=-1))
    for ch in range(ts // GMLP_CHUNK):
        c0 = ch * GMLP_CHUNK
        u = _gelu(z_ref[c0:c0 + GMLP_CHUNK, U_OFF:U_OFF + GMLP_WIDTH])
        gm_ref[c0:c0 + GMLP_CHUNK, :] = (u * (mixed[ch] + bias_ref[...])).astype(BF16)

    ya = jnp.dot(attn_ref[...], wbr_ref[0], preferred_element_type=F32)
    yg = jnp.dot(gm_ref[...], wbr_ref[1], preferred_element_type=F32)
    ga = _sigmoid(z_ref[:, GL_OFF:GL_OFF + D_MODEL])
    gg = _sigmoid(z_ref[:, GL_OFF + D_MODEL:GL_OFF + 2 * D_MODEL])
    merged = (ga * ya + gg * yg).astype(BF16)
    h1 = x + jnp.dot(merged, wout_ref[...], preferred_element_type=F32)
    h1_ref[0] = h1

    hn = _rms(h1, nffn_ref[...])
    _store_packed(hn_ref, hn)

    hn_hi = hn.astype(BF16)
    hn_lo = (hn - hn_hi.astype(F32)).astype(BF16)
    by_hi = jnp.dot(hn_hi, wr_ref[...], preferred_element_type=F32)
    by_lo = jnp.dot(hn_lo, wr_ref[:, 0:ROUTER_PAD], preferred_element_type=F32)
    logits = by_hi[:, 0:ROUTER_PAD] + by_hi[:, ROUTER_PAD:2 * ROUTER_PAD] + by_lo + br_ref[...]
    lt = jnp.transpose(logits)[0:N_EXPERTS, :]
    eid = lax.broadcasted_iota(jnp.int32, lt.shape, 0)
    vals, ids = [], []
    for _ in range(TOP_K):
        mx = jnp.max(lt, axis=0, keepdims=True)
        am = jnp.min(jnp.where(lt == mx, eid, N_EXPERTS), axis=0, keepdims=True)
        vals.append(mx)
        ids.append(am)
        lt = jnp.where(eid == am, -jnp.inf, lt)
    ex = [jnp.exp(v - vals[0]) for v in vals]
    tot = ex[0] + ex[1] + ex[2] + ex[3]
    gate_ref[...] = jnp.concatenate([e / tot for e in ex] + [jnp.zeros_like(tot)] * (8 - TOP_K), axis=0)

    @pl.when((pl.program_id(0) == 0) & (i == 0))
    def _():
        cnt_ref[...] = jnp.zeros_like(cnt_ref)

    sel = [eid == a for a in ids]
    member = (sel[0] | sel[1] | sel[2] | sel[3]).astype(F32)
    before = jnp.dot(member.astype(BF16), tri_ref[...], preferred_element_type=F32) + cnt_ref[:, 0:1]
    ranks = [jnp.sum(jnp.where(s_, before, 0.0), axis=0, keepdims=True).astype(jnp.int32) for s_ in sel]
    route_ref[...] = jnp.concatenate(ids + ranks, axis=0)
    cnt_ref[...] = cnt_ref[...] + jnp.sum(member, axis=1, keepdims=True)
    cnt_out_ref[...] = cnt_ref[...]


def _mixer_call(x, rope_tab, prm):
    b, s, d = x.shape
    ts = MIX_TILE
    nt = s // ts
    nb = s // ATTN_BLOCK
    per = ts // ATTN_BLOCK
    const2 = lambda bi, i: (0, 0)
    const3 = lambda bi, i: (0, 0, 0)

    def wspec(arr):
        return pl.BlockSpec(arr.shape, const2 if arr.ndim == 2 else const3)

    in_specs = [
        pl.BlockSpec((1, ts, d), lambda bi, i: (bi, i, 0)),
        pl.BlockSpec((1, ATTN_BLOCK, d), lambda bi, i: (bi, jnp.minimum((i + 1) * per, nb - 1), 0)),
        pl.BlockSpec((ts, 3 * LANES), lambda bi, i: (i, 0)),
        pl.BlockSpec((ATTN_BLOCK, 3 * LANES), lambda bi, i: (jnp.minimum((i + 1) * per, nb - 1), 0)),
    ] + [wspec(a) for a in prm]
    out_shape = [
        jax.ShapeDtypeStruct((b, s, d), F32),
        jax.ShapeDtypeStruct((b * s * SUB, LANES), U32),
        jax.ShapeDtypeStruct((2 * TOP_K, b * s), jnp.int32),
        jax.ShapeDtypeStruct((2 * TOP_K, b * s), F32),
        jax.ShapeDtypeStruct((N_EXPERTS, LANES), F32),
    ]
    out_specs = [
        pl.BlockSpec((1, ts, d), lambda bi, i: (bi, i, 0)),
        pl.BlockSpec((ts * SUB, LANES), lambda bi, i: (bi * nt + i, 0)),
        pl.BlockSpec((2 * TOP_K, ts), lambda bi, i: (0, bi * nt + i)),
        pl.BlockSpec((2 * TOP_K, ts), lambda bi, i: (0, bi * nt + i)),
        pl.BlockSpec((N_EXPERTS, LANES), const2),
    ]
    scratch = [
        pltpu.VMEM((ts, IN_WIDTH), F32),
        pltpu.VMEM((ts, ATTN_WIDTH), BF16),
        pltpu.VMEM((ts + 2 * ATTN_BLOCK, KV_WIDTH), BF16),
        pltpu.VMEM((ts + 2 * ATTN_BLOCK, 2 * N_KV_HEADS * LANES), BF16),
        pltpu.VMEM((ts, ATTN_WIDTH), BF16),
        pltpu.VMEM((ts, GMLP_WIDTH), BF16),
        pltpu.VMEM((N_EXPERTS, LANES), F32),
    ]
    return pl.pallas_call(
        functools.partial(_mixer_kernel, n_blocks_seq=nb),
        grid=(b, nt),
        in_specs=in_specs,
        out_specs=out_specs,
        out_shape=out_shape,
        scratch_shapes=scratch,
        compiler_params=pltpu.CompilerParams(
            dimension_semantics=("arbitrary", "arbitrary"), vmem_limit_bytes=VMEM_LIMIT),
        name="mixer",
    )(x, x, rope_tab, rope_tab, *prm)


def _sc_workers():
    info = plsc.get_sparse_core_info()
    return info.num_cores, info.num_cores * info.num_subcores


def _sc_dispatch(rows2d, slot_rows, n_out_rows):
    n_cores, n_workers = _sc_workers()
    chunk_rows = DISPATCH_TOKENS * SUB
    halves = chunk_rows // SC_CHUNK
    n_chunks = rows2d.shape[0] // chunk_rows
    per_w = n_chunks // n_workers
    idx_rows_per_choice = rows2d.shape[0] // SC_CHUNK
    mesh = plsc.VectorSubcoreMesh(core_axis_name="c", subcore_axis_name="s")

    @functools.partial(
        pl.kernel, mesh=mesh,
        out_type=jax.ShapeDtypeStruct((n_out_rows, LANES), rows2d.dtype),
        scratch_types=[pltpu.VMEM((TOP_K * halves, SC_CHUNK), jnp.int32),
                       pltpu.VMEM((chunk_rows, LANES), rows2d.dtype),
                       pltpu.SemaphoreType.DMA, pltpu.SemaphoreType.DMA],
    )
    def k(src_hbm, idx_hbm, out_hbm, idx_v, rows_v, sem_in, sem_out):
        wid = lax.axis_index("s") * n_cores + lax.axis_index("c")

        @pl.loop(0, per_w)
        def _(j):
            c = wid * per_w + j
            loads = [pltpu.async_copy(src_hbm.at[pl.ds(c * chunk_rows, chunk_rows)], rows_v, sem_in)]
            loads += [
                pltpu.async_copy(idx_hbm.at[pl.ds(kk * idx_rows_per_choice + c * halves, halves)],
                                 idx_v.at[pl.ds(kk * halves, halves)], sem_in)
                for kk in range(TOP_K)]
            for cp in loads:
                cp.wait()
            copies = [
                pltpu.async_copy(rows_v.at[pl.ds((q % halves) * SC_CHUNK, SC_CHUNK)],
                                 out_hbm.at[idx_v.at[q]], sem_out)
                for q in range(TOP_K * halves)]
            for cp in copies:
                cp.wait()

    return k(rows2d, slot_rows)


def _sc_gather(table2d, idx2d):
    n_cores, n_workers = _sc_workers()
    n = idx2d.shape[0] * SC_CHUNK
    per_w = n // n_workers
    step_rows = GATHER_STREAMS * SC_CHUNK
    idx_tile = 8
    mesh = plsc.VectorSubcoreMesh(core_axis_name="c", subcore_axis_name="s")

    @functools.partial(
        pl.kernel, mesh=mesh,
        out_type=jax.ShapeDtypeStruct((n, LANES), table2d.dtype),
        scratch_types=[pltpu.VMEM((idx_tile, SC_CHUNK), jnp.int32),
                       pltpu.VMEM((step_rows, LANES), table2d.dtype),
                       pltpu.SemaphoreType.DMA],
    )
    def k(table_hbm, idx_hbm, out_hbm, idx_v, rows_v, sem):
        wid = lax.axis_index("s") * n_cores + lax.axis_index("c")

        @pl.loop(0, per_w // (idx_tile * SC_CHUNK))
        def _(j):
            row0 = pl.multiple_of(wid * (per_w // SC_CHUNK) + j * idx_tile, idx_tile)
            pltpu.sync_copy(idx_hbm.at[pl.ds(row0, idx_tile)], idx_v)
            for part in range(idx_tile // GATHER_STREAMS):
                copies = [pltpu.async_copy(table_hbm.at[idx_v.at[part * GATHER_STREAMS + q]],
                                           rows_v.at[pl.ds(q * SC_CHUNK, SC_CHUNK)], sem)
                          for q in range(GATHER_STREAMS)]
                for cp in copies:
                    cp.wait()
                pltpu.sync_copy(rows_v, out_hbm.at[pl.ds((row0 + part * GATHER_STREAMS) * SC_CHUNK, step_rows)])

    return k(table2d, idx2d)


def _expert_weight_copies(wgu_hbm, wd_hbm, wgu_buf, wd_buf, sem, expert, slot):
    return (pltpu.make_async_copy(wgu_hbm.at[expert], wgu_buf.at[slot], sem.at[0, slot]),
            pltpu.make_async_copy(wd_hbm.at[expert], wd_buf.at[slot], sem.at[1, slot]))


def _moe_kernel(be_ref, nv_ref, nxt_ref, par_ref, x_ref, wgu_hbm, bgu_ref, wd_hbm, bd_ref, y_ref,
                wgu_buf, wd_buf, sem, wgu_s, wd_s, slab):
    i = pl.program_id(0)
    nv = nv_ref[i]
    slot = par_ref[i]

    cw = MOE_COLS
    n_chunks = D_FF // cw
    copies = functools.partial(_expert_weight_copies, wgu_hbm, wd_hbm, wgu_buf, wd_buf, sem)

    @pl.when(i == 0)
    def _():
        for cp in copies(be_ref[0], slot):
            cp.start()

    @pl.when((i == 0) | (be_ref[i] != be_ref[jnp.maximum(i - 1, 0)]))
    def _():
        for cp in copies(be_ref[i], slot):
            cp.wait()

        @pl.when(nxt_ref[i] >= 0)
        def _():
            for cp in copies(nxt_ref[i], 1 - slot):
                cp.start()

        for c in range(n_chunks):
            wgu_s[:, 2 * c * cw:(2 * c + 1) * cw] = wgu_buf[slot, :, c * cw:(c + 1) * cw].astype(BF16)
            wgu_s[:, (2 * c + 1) * cw:(2 * c + 2) * cw] = (
                wgu_buf[slot, :, D_FF + c * cw:D_FF + (c + 1) * cw].astype(BF16))
        half = D_FF // 2
        for c in range(D_MODEL // LANES):
            slab[c, pl.ds(0, half, stride=2), :] = wd_buf[slot, 0:half, c * LANES:(c + 1) * LANES]
            slab[c, pl.ds(1, half, stride=2), :] = wd_buf[slot, half:D_FF, c * LANES:(c + 1) * LANES]
        for c in range(D_MODEL // LANES):
            wd_s[:, c * LANES:(c + 1) * LANES] = slab[c].astype(BF16)

    expert = be_ref[i]

    def run_expert(rows):
        x = _load_packed(x_ref, rows)
        live = lax.broadcasted_iota(jnp.int32, x.shape, 0) < nv
        xe = jnp.where(live, x, 0.0).astype(BF16)
        even = (lax.broadcasted_iota(jnp.int32, (rows, cw), 1) % 2) == 0
        y = bd_ref[expert]

        def gate_up(c):
            return jnp.dot(xe, wgu_s[:, 2 * c * cw:(2 * c + 2) * cw], preferred_element_type=F32)

        h_next = gate_up(0)
        for c in range(n_chunks):
            h = h_next
            if c + 1 < n_chunks:
                h_next = gate_up(c + 1)
            h_a = h[:, 0:cw] + bgu_ref[expert, :, c * cw:(c + 1) * cw]
            h_b = h[:, cw:2 * cw] + bgu_ref[expert, :, D_FF + c * cw:D_FF + (c + 1) * cw]
            gate = jnp.where(even, h_a, pltpu.roll(h_b, 1, 1))
            up = jnp.where(even, pltpu.roll(h_a, cw - 1, 1), h_b)
            gate = jnp.minimum(gate, SWIGLU_LIMIT)
            up = jnp.clip(up, -SWIGLU_LIMIT, SWIGLU_LIMIT)
            act = (up + 1.0) * (gate * jax.nn.sigmoid(SWIGLU_ALPHA * gate))
            y = y + jnp.dot(act.astype(BF16), wd_s[c * cw:(c + 1) * cw, :], preferred_element_type=F32)
        _store_packed(y_ref, y)

    @pl.when(nv > MOE_BLOCK // 2)
    def _():
        run_expert(MOE_BLOCK)

    @pl.when((nv > 0) & (nv <= MOE_BLOCK // 2))
    def _():
        run_expert(MOE_BLOCK // 2)


def _moe_call(block_e, n_valid, next_e, parity, xb, wgu, bgu, wd, bd):
    n_blocks = block_e.shape[0]
    d = D_MODEL
    grid_spec = pltpu.PrefetchScalarGridSpec(
        num_scalar_prefetch=4,
        grid=(n_blocks,),
        in_specs=[
            pl.BlockSpec((MOE_BLOCK * SUB, LANES), lambda i, be, nv, nx, pr: (i, 0)),
            pl.BlockSpec(memory_space=pl.ANY),
            pl.BlockSpec((N_EXPERTS, 1, 2 * D_FF), lambda i, be, nv, nx, pr: (0, 0, 0)),
            pl.BlockSpec(memory_space=pl.ANY),
            pl.BlockSpec((N_EXPERTS, 1, d), lambda i, be, nv, nx, pr: (0, 0, 0)),
        ],
        out_specs=pl.BlockSpec((MOE_BLOCK * SUB, LANES), lambda i, be, nv, nx, pr: (i, 0)),
        scratch_shapes=[pltpu.VMEM((2, d, 2 * D_FF), F32), pltpu.VMEM((2, D_FF, d), F32),
                        pltpu.SemaphoreType.DMA((2, 2)),
                        pltpu.VMEM((d, 2 * D_FF), BF16), pltpu.VMEM((D_FF, d), BF16),
                        pltpu.VMEM((d // LANES, D_FF, LANES), F32)],
    )
    return pl.pallas_call(
        _moe_kernel,
        grid_spec=grid_spec,
        out_shape=jax.ShapeDtypeStruct((n_blocks * MOE_BLOCK * SUB, LANES), U32),
        compiler_params=pltpu.CompilerParams(
            dimension_semantics=("arbitrary",), vmem_limit_bytes=VMEM_LIMIT),
        name="moe",
    )(block_e, n_valid, next_e, parity, xb, wgu, bgu, wd, bd)


def _combine_kernel(g0_ref, g1_ref, g2_ref, g3_ref, h1_ref, gate_ref, p_ref, nple_ref, wpg_ref, wpp_ref, o_ref):
    g = jnp.transpose(jnp.concatenate([gate_ref[...], jnp.zeros((LANES - 2 * TOP_K, CMB_TILE), F32)], axis=0))
    moe = jnp.zeros((CMB_TILE, D_MODEL), F32)
    for kk, g_ref in enumerate((g0_ref, g1_ref, g2_ref, g3_ref)):
        moe = moe + _load_packed(g_ref, CMB_TILE) * g[:, kk:kk + 1]
    h2 = h1_ref[...] + moe
    hp = _rms(h2, nple_ref[...]).astype(BF16)
    gate = jax.nn.sigmoid(jnp.dot(hp, wpg_ref[...], preferred_element_type=F32))
    proj = jnp.dot(p_ref[...].astype(BF16), wpp_ref[...], preferred_element_type=F32)
    o_ref[...] = h2 + gate * proj


def _combine_call(gathered, h1_flat, gates, p_flat, nple, wpg, wpp):
    n_tok, d = h1_flat.shape
    tc = CMB_TILE
    n_tiles = n_tok // tc
    row = lambda i: (i, 0)
    const = lambda i: (0, 0)
    g_specs = [pl.BlockSpec((tc * SUB, LANES), functools.partial(lambda i, kk: (kk * n_tiles + i, 0), kk=kk))
               for kk in range(TOP_K)]
    return pl.pallas_call(
        _combine_kernel,
        grid=(n_tiles,),
        in_specs=g_specs + [
            pl.BlockSpec((tc, d), row),
            pl.BlockSpec((2 * TOP_K, tc), lambda i: (0, i)),
            pl.BlockSpec((tc, PLE_DIM), row),
            pl.BlockSpec((1, d), const),
            pl.BlockSpec((d, d), const),
            pl.BlockSpec((PLE_DIM, d), const),
        ],
        out_specs=pl.BlockSpec((tc, d), row),
        out_shape=jax.ShapeDtypeStruct((n_tok, d), F32),
        compiler_params=pltpu.CompilerParams(
            dimension_semantics=("arbitrary",), vmem_limit_bytes=VMEM_LIMIT),
        name="combine",
    )(gathered, gathered, gathered, gathered, h1_flat, gates, p_flat, nple, wpg, wpp)


def _rope_table(s):
    half = ROPE_DIM // 2
    inv_freq = jnp.power(ROPE_THETA, -jnp.arange(half, dtype=F32) * (2.0 / ROPE_DIM))
    ang = jnp.arange(s, dtype=F32)[:, None] * inv_freq[None, :]
    base = jnp.concatenate([jnp.cos(ang), jnp.sin(ang)], axis=-1)
    place = np.zeros((2 * half, 3 * LANES), np.float32)
    ones = np.zeros((1, 3 * LANES), np.float32)
    for lane in range(3 * LANES):
        kind, d = lane // LANES, lane % HEAD_DIM
        if kind == 0 and d < ROPE_DIM:
            place[d % half, lane] = 1.0
        elif kind == 0:
            ones[0, lane] = 1.0
        elif kind == 1 and d < half:
            place[half + d, lane] = -1.0
        elif kind == 2 and half <= d < ROPE_DIM:
            place[d, lane] = 1.0
    return jnp.dot(base, place, precision=lax.Precision.HIGHEST) + ones


def _routing(route, counts, n_tok):
    ids = route[0:TOP_K]
    ranks = route[TOP_K:2 * TOP_K]
    counts = counts.astype(jnp.int32)
    padded = (counts + MOE_BLOCK - 1) // MOE_BLOCK * MOE_BLOCK
    pends = jnp.cumsum(padded)
    pstarts = pends - padded
    pos = ranks
    for e in range(N_EXPERTS):
        pos = pos + jnp.where(ids == e, pstarts[e], 0)
    n_slots = -(-(n_tok * TOP_K) // MOE_BLOCK) * MOE_BLOCK + N_EXPERTS * MOE_BLOCK
    n_blocks = n_slots // MOE_BLOCK
    first = jnp.arange(n_blocks, dtype=jnp.int32) * MOE_BLOCK
    block_e = jnp.clip(jnp.sum((first[:, None] >= pends[None, :]).astype(jnp.int32), axis=1), 0, N_EXPERTS - 1)
    own = block_e[:, None] == jnp.arange(N_EXPERTS, dtype=jnp.int32)[None, :]
    left = jnp.sum(jnp.where(own, (counts + pstarts)[None, :], 0), axis=1) - first
    n_valid = jnp.clip(left, 0, MOE_BLOCK)
    later = jnp.where(block_e[None, :] > block_e[:, None], block_e[None, :], N_EXPERTS)
    next_e = jnp.min(later, axis=1)
    next_e = jnp.where(next_e == N_EXPERTS, -1, next_e)
    present = jnp.any(own, axis=0)
    runs_before = jnp.sum((present[None, :] & (jnp.arange(N_EXPERTS)[None, :] < block_e[:, None])).astype(jnp.int32),
                          axis=1)
    parity = runs_before % 2
    return pos, (block_e, n_valid.astype(jnp.int32), next_e.astype(jnp.int32), parity.astype(jnp.int32)), n_slots


def _expand_rows(pos):
    spread = (jnp.arange(SUB * LANES, dtype=jnp.int32)[None, :] // SUB
              == jnp.arange(LANES, dtype=jnp.int32)[:, None]).astype(F32)
    wide = jnp.dot(pos.reshape(-1, LANES).astype(F32), spread, precision=lax.Precision.HIGHEST)
    wide = wide.astype(jnp.int32) * SUB + (jnp.arange(SUB * LANES, dtype=jnp.int32) % SUB)[None, :]
    return wide.reshape(-1, LANES)


def _group_forward(x, p_l, rope_tab, mix_prm, moe_prm, ple_prm):
    b, s, d = x.shape
    n_tok = b * s
    h1, hn_rows, route, gates, counts = _mixer_call(x, rope_tab, mix_prm)
    pos, block_info, n_slots = _routing(route, counts[:, 0], n_tok)
    slot_rows = _expand_rows(pos)
    xb = _sc_dispatch(hn_rows, slot_rows, n_slots * SUB)
    y = _moe_call(*block_info, xb, *moe_prm)
    gathered = _sc_gather(y, slot_rows)
    out = _combine_call(gathered, h1.reshape(n_tok, d), gates,
                        p_l.reshape(n_tok, PLE_DIM), *ple_prm)
    return out.reshape(b, s, d)


def kernel(x_prompt, x_sample, p_prompt, p_sample, norm_mix, w_in, q_gain, k_gain, attn_sink, gmlp_v_gain, gmlp_w_s, gmlp_b_s, w_branch, w_out, norm_ffn, w_router, b_router, w_gate_up, b_gate_up, w_down, b_down, norm_ple, w_ple_gate, w_ple_proj):
    depth = norm_mix.shape[0]
    hp, hs = x_prompt, x_sample
    for l in range(depth):
        row = lambda a: a.reshape(1, -1)
        blockdiag = jnp.kron(jnp.eye(N_Q_HEADS, dtype=F32),
                             jnp.full((HEAD_DIM, HEAD_DIM), 1.0 / HEAD_DIM, F32)).astype(BF16)
        sink_heads = attn_sink[l].reshape(N_KV_HEADS, GQA_GROUP)[:, jnp.array(HEAD_ORDER)]
        sink_col = jnp.repeat(sink_heads, ATTN_BLOCK, axis=1)[..., None]
        sink_rows = jnp.where(jnp.arange(LANES) == 0, sink_col, -jnp.inf)
        first_row = (jnp.arange(LANES) == 0)[:, None]
        upper = (jnp.arange(LANES) >= HEAD_DIM)[None, :]
        sink_values = jnp.stack([first_row & upper, first_row & ~upper]).astype(BF16)
        wcat = jnp.transpose(gmlp_w_s[l], (1, 0, 2)).reshape(GMLP_CHUNK, GMLP_GROUPS * GMLP_CHUNK).astype(BF16)
        bias_full = jnp.repeat(gmlp_b_s[l].T, GMLP_GROUP_DIM, axis=1)
        wr = jnp.pad(w_router[l], ((0, 0), (0, ROUTER_PAD - N_EXPERTS)))
        wr_hi = wr.astype(BF16)
        wr_lo = (wr - wr_hi.astype(F32)).astype(BF16)
        br = jnp.pad(b_router[l], (0, ROUTER_PAD - N_EXPERTS)).reshape(1, ROUTER_PAD)
        tri = (jnp.arange(MIX_TILE)[:, None] < jnp.arange(MIX_TILE)[None, :]).astype(BF16)
        mix_prm = (
            row(norm_mix[l]), w_in[l].astype(BF16),
            row(jnp.tile(q_gain[l], N_Q_HEADS)), row(jnp.tile(k_gain[l], N_KV_HEADS)),
            sink_rows, sink_values, row(gmlp_v_gain[l]), blockdiag, wcat, bias_full,
            w_branch[l].astype(BF16), w_out[l].astype(BF16), row(norm_ffn[l]),
            jnp.concatenate([wr_hi, wr_lo], axis=1), br, tri,
        )
        moe_prm = (
            w_gate_up[l], b_gate_up[l].reshape(N_EXPERTS, 1, 2 * D_FF),
            w_down[l], b_down[l].reshape(N_EXPERTS, 1, D_MODEL),
        )
        ple_prm = (row(norm_ple[l]), w_ple_gate[l].astype(BF16), w_ple_proj[l].astype(BF16))
        rope_tab = _rope_table(max(hp.shape[1], hs.shape[1]))
        hp = _group_forward(hp, p_prompt[l], rope_tab, mix_prm, moe_prm, ple_prm)
        hs = _group_forward(hs, p_sample[l], rope_tab, mix_prm, moe_prm, ple_prm)
    return (hp.astype(x_prompt.dtype), hs.astype(x_sample.dtype))
```

```python
import functools

import jax
import jax.numpy as jnp
import numpy as np
from jax import lax
from jax.experimental import pallas as pl
from jax.experimental.pallas import tpu as pltpu
from jax.experimental.pallas import tpu_sc as plsc

D_MODEL = 1024
HEAD_DIM = 64
N_Q_HEADS = 8
N_KV_HEADS = 2
GQA_GROUP = N_Q_HEADS // N_KV_HEADS
HEAD_ORDER = (0, 2, 1, 3)
ATTN_WIDTH = N_Q_HEADS * HEAD_DIM
KV_WIDTH = N_KV_HEADS * HEAD_DIM
WINDOW = 128
ATTN_BLOCK = 128
ROPE_THETA = 500000.0
ROPE_DIM = HEAD_DIM // 4
GMLP_WIDTH = D_MODEL // 2
GMLP_GROUPS = 8
GMLP_GROUP_DIM = GMLP_WIDTH // GMLP_GROUPS
GMLP_CHUNK = 128
N_BRANCH = 2
IN_WIDTH = ATTN_WIDTH + 2 * KV_WIDTH + 2 * GMLP_WIDTH + N_BRANCH * D_MODEL
N_EXPERTS = 32
TOP_K = 4
D_FF = D_MODEL
SWIGLU_ALPHA = 1.702
SWIGLU_LIMIT = 7.0
PLE_DIM = 256
EPS = 1e-6

Q_OFF = 0
K_OFF = ATTN_WIDTH
V_OFF = K_OFF + KV_WIDTH
U_OFF = V_OFF + KV_WIDTH
VG_OFF = U_OFF + GMLP_WIDTH
GL_OFF = VG_OFF + GMLP_WIDTH

LANES = 128
PACK_WORDS = D_MODEL // 2
SUB = PACK_WORDS // LANES
MIX_TILE = 512
PROJ_COLS = 256
MOE_BLOCK = 512
MOE_COLS = 256
CMB_TILE = 1024
ROUTER_PAD = 128
SC_CHUNK = 128
DISPATCH_TOKENS = 64
GATHER_STREAMS = 4
VMEM_LIMIT = 56 * 1024 * 1024

BF16 = jnp.bfloat16
F32 = jnp.float32
U32 = jnp.uint32


def _rms(x, gain):
    return x * lax.rsqrt(jnp.mean(x * x, axis=-1, keepdims=True) + EPS) * gain


def _gelu(x):
    return 0.5 * x * (1.0 + lax.erf(x * np.float32(np.sqrt(0.5))))


def _sigmoid(x):
    return 0.5 * jnp.tanh(0.5 * x) + 0.5


def _head_rms(x, blockdiag, gain):
    ms = jnp.dot((x * x).astype(BF16), blockdiag, preferred_element_type=F32)
    return x * lax.rsqrt(ms + EPS) * gain


def _rope(x, cos, sin_lo, sin_hi):
    w = x.shape[-1]
    return x * cos + pltpu.roll(x, w - ROPE_DIM // 2, 1) * sin_lo + pltpu.roll(x, ROPE_DIM // 2, 1) * sin_hi


def _value_blocks(v):
    swapped = pltpu.roll(v, HEAD_DIM, 1)
    low = lax.broadcasted_iota(jnp.int32, v.shape, 1) < HEAD_DIM
    blocks = [jnp.where(low, v, 1.0), jnp.where(low, 1.0, swapped), jnp.where(low, swapped, 1.0), jnp.where(low, 1.0, v)]
    return jnp.concatenate(blocks, axis=1).astype(BF16)


def _tile_lanes(t, reps):
    return t if reps == 1 else jnp.concatenate([t] * reps, axis=-1)


def _store_packed(ref, x):
    rows = x.shape[0]
    hi = lax.bitcast_convert_type(x[:, :PACK_WORDS].astype(BF16).astype(F32), U32)
    lo = lax.bitcast_convert_type(x[:, PACK_WORDS:].astype(BF16).astype(F32), U32)
    words = hi | (lo >> 16)
    for j in range(SUB):
        ref[pl.ds(j, rows, stride=SUB), :] = words[:, j * LANES:(j + 1) * LANES]


def _load_packed(ref, rows):
    words = jnp.concatenate([ref[pl.ds(j, rows, stride=SUB), :] for j in range(SUB)], axis=1)
    left = lax.bitcast_convert_type(words & np.uint32(0xFFFF0000), F32)
    right = lax.bitcast_convert_type(words << 16, F32)
    return jnp.concatenate([left, right], axis=1)


def _mixer_kernel(x_ref, xn_ref, rp_ref, rpn_ref,
                  nmix_ref, win_ref, qg_ref, kg_ref, sink_ref, sinkv_ref, vgain_ref, bdq_ref,
                  wcat_ref, bias_ref, wbr_ref, wout_ref, nffn_ref, wr_ref, br_ref, tri_ref,
                  h1_ref, hn_ref, route_ref, gate_ref, cnt_out_ref,
                  z_ref, q_ref, k_ref, v_ref, attn_ref, gm_ref, cnt_ref, *, n_blocks_seq):
    ts = MIX_TILE
    i = pl.program_id(1)
    x = x_ref[0]

    @pl.when(i > 0)
    def _():
        k_ref[0:ATTN_BLOCK, :] = k_ref[ts:ts + ATTN_BLOCK, :]
        v_ref[0:ATTN_BLOCK, :] = v_ref[ts:ts + ATTN_BLOCK, :]

    @pl.when(i == 0)
    def _():
        k_ref[0:ATTN_BLOCK, :] = jnp.zeros((ATTN_BLOCK, k_ref.shape[1]), BF16)
        v_ref[0:ATTN_BLOCK, :] = jnp.zeros((ATTN_BLOCK, v_ref.shape[1]), BF16)

    xn = _rms(x, nmix_ref[...]).astype(BF16)

    def project(lo, hi):
        z_ref[:, lo:hi] = jnp.dot(xn, win_ref[:, lo:hi], preferred_element_type=F32)

    project(Q_OFF, U_OFF)
    n_items = (ts // ATTN_BLOCK) * N_KV_HEADS
    later = [(U_OFF + t * PROJ_COLS, U_OFF + (t + 1) * PROJ_COLS) for t in range((IN_WIDTH - U_OFF) // PROJ_COLS)]
    per_item = [len(later) // n_items + (1 if n < len(later) % n_items else 0) for n in range(n_items)]

    cos = rp_ref[:, 0:LANES]
    sin_lo = rp_ref[:, LANES:2 * LANES]
    sin_hi = rp_ref[:, 2 * LANES:3 * LANES]
    bdq = bdq_ref[...]
    bdk = bdq_ref[0:KV_WIDTH, 0:KV_WIDTH]

    q = _head_rms(z_ref[:, Q_OFF:Q_OFF + ATTN_WIDTH], bdq, qg_ref[...])
    reps = ATTN_WIDTH // LANES
    q = _rope(q, _tile_lanes(cos, reps), _tile_lanes(sin_lo, reps), _tile_lanes(sin_hi, reps))
    q_ref[...] = (q * (HEAD_DIM ** -0.5)).astype(BF16)

    k = _head_rms(z_ref[:, K_OFF:K_OFF + KV_WIDTH], bdk, kg_ref[...])
    k_ref[ATTN_BLOCK:ATTN_BLOCK + ts, :] = _rope(k, cos, sin_lo, sin_hi).astype(BF16)
    v_ref[ATTN_BLOCK:ATTN_BLOCK + ts, :] = _value_blocks(z_ref[:, V_OFF:V_OFF + KV_WIDTH])

    xhn = _rms(xn_ref[0], nmix_ref[...]).astype(BF16)
    zh = jnp.dot(xhn, win_ref[:, K_OFF:K_OFF + 2 * KV_WIDTH], preferred_element_type=F32)
    kh = _head_rms(zh[:, 0:KV_WIDTH], bdk, kg_ref[...])
    kh = _rope(kh, rpn_ref[:, 0:LANES], rpn_ref[:, LANES:2 * LANES], rpn_ref[:, 2 * LANES:3 * LANES])
    k_ref[ATTN_BLOCK + ts:2 * ATTN_BLOCK + ts, :] = kh.astype(BF16)
    v_ref[ATTN_BLOCK + ts:2 * ATTN_BLOCK + ts, :] = _value_blocks(zh[:, KV_WIDTH:2 * KV_WIDTH])

    rows = GQA_GROUP * ATTN_BLOCK
    keys = 3 * ATTN_BLOCK
    r = lax.broadcasted_iota(jnp.int32, (rows, keys), 0) % ATTN_BLOCK
    c = lax.broadcasted_iota(jnp.int32, (rows, keys), 1)
    band = (c >= r) & (c <= r + 2 * WINDOW)
    low = lax.broadcasted_iota(jnp.int32, (ATTN_BLOCK, 2 * HEAD_DIM), 1) < HEAD_DIM
    pair = 2 * ATTN_BLOCK
    items = [(qb, j) for qb in range(ts // ATTN_BLOCK) for j in range(N_KV_HEADS)]

    def scores(n):
        qb, j = items[n]
        r0 = qb * ATTN_BLOCK
        q4 = jnp.concatenate(
            [q_ref[r0:r0 + ATTN_BLOCK, (GQA_GROUP * j + g) * HEAD_DIM:(GQA_GROUP * j + g + 1) * HEAD_DIM]
             for g in HEAD_ORDER], axis=0)
        kw = k_ref[r0:r0 + keys, j * HEAD_DIM:(j + 1) * HEAD_DIM]
        return lax.dot_general(q4, kw, (((1,), (1,)), ((), ())), preferred_element_type=F32)

    s_next = scores(0)
    for n, (qb, j) in enumerate(items):
        s = s_next
        if n + 1 < len(items):
            s_next = scores(n + 1)
        for _ in range(per_item[n]):
            project(*later.pop(0))
        gb = i * (ts // ATTN_BLOCK) + qb
        lo = jnp.where(gb == 0, ATTN_BLOCK, 0)
        hi = jnp.where(gb == n_blocks_seq - 1, 2 * ATTN_BLOCK, keys)
        valid = band & (c >= lo) & (c < hi)
        r0 = qb * ATTN_BLOCK
        s = jnp.concatenate([jnp.where(valid, s, -jnp.inf), sink_ref[j]], axis=1)
        p = jnp.exp(s - jnp.max(s, axis=-1, keepdims=True)).astype(BF16)
        outs = []
        for par in range(2):
            vw = jnp.concatenate([v_ref[r0:r0 + keys, (2 * j + par) * LANES:(2 * j + par + 1) * LANES],
                                  sinkv_ref[par]], axis=0)
            o = jnp.dot(p[par * pair:(par + 1) * pair], vw, preferred_element_type=F32)
            outs.append(o / pltpu.roll(o, HEAD_DIM, 1))
        for a in range(GQA_GROUP // 2):
            both = jnp.where(low, outs[0][a * ATTN_BLOCK:(a + 1) * ATTN_BLOCK],
                             outs[1][a * ATTN_BLOCK:(a + 1) * ATTN_BLOCK])
            h0 = GQA_GROUP * j + 2 * a
            attn_ref[r0:r0 + ATTN_BLOCK, h0 * HEAD_DIM:(h0 + 2) * HEAD_DIM] = both.astype(BF16)

    vgn_all = _head_rms(_gelu(z_ref[:, VG_OFF:VG_OFF + GMLP_WIDTH]), bdq, vgain_ref[...])
    half = GMLP_WIDTH // 2
    gpm = half // GMLP_GROUP_DIM
    lane_grp = lax.broadcasted_iota(jnp.int32, (GMLP_CHUNK, half), 1) // GMLP_GROUP_DIM
    mixed = []
    for ch in range(ts // GMLP_CHUNK):
        c0 = ch * GMLP_CHUNK
        parts = []
        for nt in range(2):
            part = vgn_all[c0:c0 + GMLP_CHUNK, nt * half:(nt + 1) * half]
            vexp = jnp.concatenate(
                [jnp.where(lane_grp == gl, part, 0.0).astype(BF16) for gl in range(gpm)], axis=0)
            wpart = wcat_ref[:, nt * gpm * GMLP_CHUNK:(nt + 1) * gpm * GMLP_CHUNK]
            parts.append(jnp.dot(wpart, vexp, preferred_element_type=F32))
        mixed.append(jnp.concatenate(parts, axis=-1))
    for ch in range(ts // GMLP_CHUNK):
        c0 = ch * GMLP_CHUNK
        u = _gelu(z_ref[c0:c0 + GMLP_CHUNK, U_OFF:U_OFF + GMLP_WIDTH])
        gm_ref[c0:c0 + GMLP_CHUNK, :] = (u * (mixed[ch] + bias_ref[...])).astype(BF16)

    ya = jnp.dot(attn_ref[...], wbr_ref[0], preferred_element_type=F32)
    yg = jnp.dot(gm_ref[...], wbr_ref[1], preferred_element_type=F32)
    ga = _sigmoid(z_ref[:, GL_OFF:GL_OFF + D_MODEL])
    gg = _sigmoid(z_ref[:, GL_OFF + D_MODEL:GL_OFF + 2 * D_MODEL])
    merged = (ga * ya + gg * yg).astype(BF16)
    h1 = x + jnp.dot(merged, wout_ref[...], preferred_element_type=F32)
    h1_ref[0] = h1

    hn = _rms(h1, nffn_ref[...])
    _store_packed(hn_ref, hn)

    hn_hi = hn.astype(BF16)
    hn_lo = (hn - hn_hi.astype(F32)).astype(BF16)
    by_hi = jnp.dot(hn_hi, wr_ref[...], preferred_element_type=F32)
    by_lo = jnp.dot(hn_lo, wr_ref[:, 0:ROUTER_PAD], preferred_element_type=F32)
    logits = by_hi[:, 0:ROUTER_PAD] + by_hi[:, ROUTER_PAD:2 * ROUTER_PAD] + by_lo + br_ref[...]
    lt = jnp.transpose(logits)[0:N_EXPERTS, :]
    eid = lax.broadcasted_iota(jnp.int32, lt.shape, 0)
    vals, ids = [], []
    for _ in range(TOP_K):
        mx = jnp.max(lt, axis=0, keepdims=True)
        am = jnp.min(jnp.where(lt == mx, eid, N_EXPERTS), axis=0, keepdims=True)
        vals.append(mx)
        ids.append(am)
        lt = jnp.where(eid == am, -jnp.inf, lt)
    ex = [jnp.exp(v - vals[0]) for v in vals]
    tot = ex[0] + ex[1] + ex[2] + ex[3]
    gate_ref[...] = jnp.concatenate([e / tot for e in ex] + [jnp.zeros_like(tot)] * (8 - TOP_K), axis=0)

    @pl.when((pl.program_id(0) == 0) & (i == 0))
    def _():
        cnt_ref[...] = jnp.zeros_like(cnt_ref)

    sel = [eid == a for a in ids]
    member = (sel[0] | sel[1] | sel[2] | sel[3]).astype(F32)
    before = jnp.dot(member.astype(BF16), tri_ref[...], preferred_element_type=F32) + cnt_ref[:, 0:1]
    ranks = [jnp.sum(jnp.where(s_, before, 0.0), axis=0, keepdims=True).astype(jnp.int32) for s_ in sel]
    route_ref[...] = jnp.concatenate(ids + ranks, axis=0)
    cnt_ref[...] = cnt_ref[...] + jnp.sum(member, axis=1, keepdims=True)
    cnt_out_ref[...] = cnt_ref[...]


def _mixer_call(x, rope_tab, prm):
    b, s, d = x.shape
    ts = MIX_TILE
    nt = s // ts
    nb = s // ATTN_BLOCK
    per = ts // ATTN_BLOCK
    const2 = lambda bi, i: (0, 0)
    const3 = lambda bi, i: (0, 0, 0)

    def wspec(arr):
        return pl.BlockSpec(arr.shape, const2 if arr.ndim == 2 else const3)

    in_specs = [
        pl.BlockSpec((1, ts, d), lambda bi, i: (bi, i, 0)),
        pl.BlockSpec((1, ATTN_BLOCK, d), lambda bi, i: (bi, jnp.minimum((i + 1) * per, nb - 1), 0)),
        pl.BlockSpec((ts, 3 * LANES), lambda bi, i: (i, 0)),
        pl.BlockSpec((ATTN_BLOCK, 3 * LANES), lambda bi, i: (jnp.minimum((i + 1) * per, nb - 1), 0)),
    ] + [wspec(a) for a in prm]
    out_shape = [
        jax.ShapeDtypeStruct((b, s, d), F32),
        jax.ShapeDtypeStruct((b * s * SUB, LANES), U32),
        jax.ShapeDtypeStruct((2 * TOP_K, b * s), jnp.int32),
        jax.ShapeDtypeStruct((2 * TOP_K, b * s), F32),
        jax.ShapeDtypeStruct((N_EXPERTS, LANES), F32),
    ]
    out_specs = [
        pl.BlockSpec((1, ts, d), lambda bi, i: (bi, i, 0)),
        pl.BlockSpec((ts * SUB, LANES), lambda bi, i: (bi * nt + i, 0)),
        pl.BlockSpec((2 * TOP_K, ts), lambda bi, i: (0, bi * nt + i)),
        pl.BlockSpec((2 * TOP_K, ts), lambda bi, i: (0, bi * nt + i)),
        pl.BlockSpec((N_EXPERTS, LANES), const2),
    ]
    scratch = [
        pltpu.VMEM((ts, IN_WIDTH), F32),
        pltpu.VMEM((ts, ATTN_WIDTH), BF16),
        pltpu.VMEM((ts + 2 * ATTN_BLOCK, KV_WIDTH), BF16),
        pltpu.VMEM((ts + 2 * ATTN_BLOCK, 2 * N_KV_HEADS * LANES), BF16),
        pltpu.VMEM((ts, ATTN_WIDTH), BF16),
        pltpu.VMEM((ts, GMLP_WIDTH), BF16),
        pltpu.VMEM((N_EXPERTS, LANES), F32),
    ]
    return pl.pallas_call(
        functools.partial(_mixer_kernel, n_blocks_seq=nb),
        grid=(b, nt),
        in_specs=in_specs,
        out_specs=out_specs,
        out_shape=out_shape,
        scratch_shapes=scratch,
        compiler_params=pltpu.CompilerParams(
            dimension_semantics=("arbitrary", "arbitrary"), vmem_limit_bytes=VMEM_LIMIT),
        name="mixer",
    )(x, x, rope_tab, rope_tab, *prm)


def _sc_workers():
    info = plsc.get_sparse_core_info()
    return info.num_cores, info.num_cores * info.num_subcores


def _sc_dispatch(rows2d, slot_rows, n_out_rows):
    n_cores, n_workers = _sc_workers()
    chunk_rows = DISPATCH_TOKENS * SUB
    halves = chunk_rows // SC_CHUNK
    n_chunks = rows2d.shape[0] // chunk_rows
    per_w = n_chunks // n_workers
    idx_rows_per_choice = rows2d.shape[0] // SC_CHUNK
    mesh = plsc.VectorSubcoreMesh(core_axis_name="c", subcore_axis_name="s")

    @functools.partial(
        pl.kernel, mesh=mesh,
        out_type=jax.ShapeDtypeStruct((n_out_rows, LANES), rows2d.dtype),
        scratch_types=[pltpu.VMEM((TOP_K * halves, SC_CHUNK), jnp.int32),
                       pltpu.VMEM((chunk_rows, LANES), rows2d.dtype),
                       pltpu.SemaphoreType.DMA, pltpu.SemaphoreType.DMA],
    )
    def k(src_hbm, idx_hbm, out_hbm, idx_v, rows_v, sem_in, sem_out):
        wid = lax.axis_index("s") * n_cores + lax.axis_index("c")

        @pl.loop(0, per_w)
        def _(j):
            c = wid * per_w + j
            loads = [pltpu.async_copy(src_hbm.at[pl.ds(c * chunk_rows, chunk_rows)], rows_v, sem_in)]
            loads += [
                pltpu.async_copy(idx_hbm.at[pl.ds(kk * idx_rows_per_choice + c * halves, halves)],
                                 idx_v.at[pl.ds(kk * halves, halves)], sem_in)
                for kk in range(TOP_K)]
            for cp in loads:
                cp.wait()
            copies = [
                pltpu.async_copy(rows_v.at[pl.ds((q % halves) * SC_CHUNK, SC_CHUNK)],
                                 out_hbm.at[idx_v.at[q]], sem_out)
                for q in range(TOP_K * halves)]
            for cp in copies:
                cp.wait()

    return k(rows2d, slot_rows)


def _sc_gather(table2d, idx2d):
    n_cores, n_workers = _sc_workers()
    n = idx2d.shape[0] * SC_CHUNK
    per_w = n // n_workers
    step_rows = GATHER_STREAMS * SC_CHUNK
    idx_tile = 8
    mesh = plsc.VectorSubcoreMesh(core_axis_name="c", subcore_axis_name="s")

    @functools.partial(
        pl.kernel, mesh=mesh,
        out_type=jax.ShapeDtypeStruct((n, LANES), table2d.dtype),
        scratch_types=[pltpu.VMEM((idx_tile, SC_CHUNK), jnp.int32),
                       pltpu.VMEM((step_rows, LANES), table2d.dtype),
                       pltpu.SemaphoreType.DMA],
    )
    def k(table_hbm, idx_hbm, out_hbm, idx_v, rows_v, sem):
        wid = lax.axis_index("s") * n_cores + lax.axis_index("c")

        @pl.loop(0, per_w // (idx_tile * SC_CHUNK))
        def _(j):
            row0 = pl.multiple_of(wid * (per_w // SC_CHUNK) + j * idx_tile, idx_tile)
            pltpu.sync_copy(idx_hbm.at[pl.ds(row0, idx_tile)], idx_v)
            for part in range(idx_tile // GATHER_STREAMS):
                copies = [pltpu.async_copy(table_hbm.at[idx_v.at[part * GATHER_STREAMS + q]],
                                           rows_v.at[pl.ds(q * SC_CHUNK, SC_CHUNK)], sem)
                          for q in range(GATHER_STREAMS)]
                for cp in copies:
                    cp.wait()
                pltpu.sync_copy(rows_v, out_hbm.at[pl.ds((row0 + part * GATHER_STREAMS) * SC_CHUNK, step_rows)])

    return k(table2d, idx2d)


def _expert_weight_copies(wgu_hbm, wd_hbm, wgu_buf, wd_buf, sem, expert, slot):
    return (pltpu.make_async_copy(wgu_hbm.at[expert], wgu_buf.at[slot], sem.at[0, slot]),
            pltpu.make_async_copy(wd_hbm.at[expert], wd_buf.at[slot], sem.at[1, slot]))


def _moe_kernel(be_ref, nv_ref, nxt_ref, par_ref, x_ref, wgu_hbm, bgu_ref, wd_hbm, bd_ref, y_ref,
                wgu_buf, wd_buf, sem, wgu_s, wd_s, slab):
    i = pl.program_id(0)
    nv = nv_ref[i]
    slot = par_ref[i]

    cw = MOE_COLS
    n_chunks = D_FF // cw
    copies = functools.partial(_expert_weight_copies, wgu_hbm, wd_hbm, wgu_buf, wd_buf, sem)

    @pl.when(i == 0)
    def _():
        for cp in copies(be_ref[0], slot):
            cp.start()

    @pl.when((i == 0) | (be_ref[i] != be_ref[jnp.maximum(i - 1, 0)]))
    def _():
        for cp in copies(be_ref[i], slot):
            cp.wait()

        @pl.when(nxt_ref[i] >= 0)
        def _():
            for cp in copies(nxt_ref[i], 1 - slot):
                cp.start()

        for c in range(n_chunks):
            wgu_s[:, 2 * c * cw:(2 * c + 1) * cw] = wgu_buf[slot, :, c * cw:(c + 1) * cw].astype(BF16)
            wgu_s[:, (2 * c + 1) * cw:(2 * c + 2) * cw] = (
                wgu_buf[slot, :, D_FF + c * cw:D_FF + (c + 1) * cw].astype(BF16))
        half = D_FF // 2
        for c in range(D_MODEL // LANES):
            slab[c, pl.ds(0, half, stride=2), :] = wd_buf[slot, 0:half, c * LANES:(c + 1) * LANES]
            slab[c, pl.ds(1, half, stride=2), :] = wd_buf[slot, half:D_FF, c * LANES:(c + 1) * LANES]
        for c in range(D_MODEL // LANES):
            wd_s[:, c * LANES:(c + 1) * LANES] = slab[c].astype(BF16)

    expert = be_ref[i]

    def run_expert(rows):
        x = _load_packed(x_ref, rows)
        live = lax.broadcasted_iota(jnp.int32, x.shape, 0) < nv
        xe = jnp.where(live, x, 0.0).astype(BF16)
        even = (lax.broadcasted_iota(jnp.int32, (rows, cw), 1) % 2) == 0
        y = bd_ref[expert]

        def gate_up(c):
            return jnp.dot(xe, wgu_s[:, 2 * c * cw:(2 * c + 2) * cw], preferred_element_type=F32)

        h_next = gate_up(0)
        for c in range(n_chunks):
            h = h_next
            if c + 1 < n_chunks:
                h_next = gate_up(c + 1)
            h_a = h[:, 0:cw] + bgu_ref[expert, :, c * cw:(c + 1) * cw]
            h_b = h[:, cw:2 * cw] + bgu_ref[expert, :, D_FF + c * cw:D_FF + (c + 1) * cw]
            gate = jnp.where(even, h_a, pltpu.roll(h_b, 1, 1))
            up = jnp.where(even, pltpu.roll(h_a, cw - 1, 1), h_b)
            gate = jnp.minimum(gate, SWIGLU_LIMIT)
            up = jnp.clip(up, -SWIGLU_LIMIT, SWIGLU_LIMIT)
            act = (up + 1.0) * (gate * jax.nn.sigmoid(SWIGLU_ALPHA * gate))
            y = y + jnp.dot(act.astype(BF16), wd_s[c * cw:(c + 1) * cw, :], preferred_element_type=F32)
        _store_packed(y_ref, y)

    @pl.when(nv > MOE_BLOCK // 2)
    def _():
        run_expert(MOE_BLOCK)

    @pl.when((nv > 0) & (nv <= MOE_BLOCK // 2))
    def _():
        run_expert(MOE_BLOCK // 2)


def _moe_call(block_e, n_valid, next_e, parity, xb, wgu, bgu, wd, bd):
    n_blocks = block_e.shape[0]
    d = D_MODEL
    grid_spec = pltpu.PrefetchScalarGridSpec(
        num_scalar_prefetch=4,
        grid=(n_blocks,),
        in_specs=[
            pl.BlockSpec((MOE_BLOCK * SUB, LANES), lambda i, be, nv, nx, pr: (i, 0)),
            pl.BlockSpec(memory_space=pl.ANY),
            pl.BlockSpec((N_EXPERTS, 1, 2 * D_FF), lambda i, be, nv, nx, pr: (0, 0, 0)),
            pl.BlockSpec(memory_space=pl.ANY),
            pl.BlockSpec((N_EXPERTS, 1, d), lambda i, be, nv, nx, pr: (0, 0, 0)),
        ],
        out_specs=pl.BlockSpec((MOE_BLOCK * SUB, LANES), lambda i, be, nv, nx, pr: (i, 0)),
        scratch_shapes=[pltpu.VMEM((2, d, 2 * D_FF), F32), pltpu.VMEM((2, D_FF, d), F32),
                        pltpu.SemaphoreType.DMA((2, 2)),
                        pltpu.VMEM((d, 2 * D_FF), BF16), pltpu.VMEM((D_FF, d), BF16),
                        pltpu.VMEM((d // LANES, D_FF, LANES), F32)],
    )
    return pl.pallas_call(
        _moe_kernel,
        grid_spec=grid_spec,
        out_shape=jax.ShapeDtypeStruct((n_blocks * MOE_BLOCK * SUB, LANES), U32),
        compiler_params=pltpu.CompilerParams(
            dimension_semantics=("arbitrary",), vmem_limit_bytes=VMEM_LIMIT),
        name="moe",
    )(block_e, n_valid, next_e, parity, xb, wgu, bgu, wd, bd)


def _combine_kernel(g0_ref, g1_ref, g2_ref, g3_ref, h1_ref, gate_ref, p_ref, nple_ref, wpg_ref, wpp_ref, o_ref):
    g = jnp.transpose(jnp.concatenate([gate_ref[...], jnp.zeros((LANES - 2 * TOP_K, CMB_TILE), F32)], axis=0))
    moe = jnp.zeros((CMB_TILE, D_MODEL), F32)
    for kk, g_ref in enumerate((g0_ref, g1_ref, g2_ref, g3_ref)):
        moe = moe + _load_packed(g_ref, CMB_TILE) * g[:, kk:kk + 1]
    h2 = h1_ref[...] + moe
    hp = _rms(h2, nple_ref[...]).astype(BF16)
    gate = jax.nn.sigmoid(jnp.dot(hp, wpg_ref[...], preferred_element_type=F32))
    proj = jnp.dot(p_ref[...].astype(BF16), wpp_ref[...], preferred_element_type=F32)
    o_ref[...] = h2 + gate * proj


def _combine_call(gathered, h1_flat, gates, p_flat, nple, wpg, wpp):
    n_tok, d = h1_flat.shape
    tc = CMB_TILE
    n_tiles = n_tok // tc
    row = lambda i: (i, 0)
    const = lambda i: (0, 0)
    g_specs = [pl.BlockSpec((tc * SUB, LANES), functools.partial(lambda i, kk: (kk * n_tiles + i, 0), kk=kk))
               for kk in range(TOP_K)]
    return pl.pallas_call(
        _combine_kernel,
        grid=(n_tiles,),
        in_specs=g_specs + [
            pl.BlockSpec((tc, d), row),
            pl.BlockSpec((2 * TOP_K, tc), lambda i: (0, i)),
            pl.BlockSpec((tc, PLE_DIM), row),
            pl.BlockSpec((1, d), const),
            pl.BlockSpec((d, d), const),
            pl.BlockSpec((PLE_DIM, d), const),
        ],
        out_specs=pl.BlockSpec((tc, d), row),
        out_shape=jax.ShapeDtypeStruct((n_tok, d), F32),
        compiler_params=pltpu.CompilerParams(
            dimension_semantics=("arbitrary",), vmem_limit_bytes=VMEM_LIMIT),
        name="combine",
    )(gathered, gathered, gathered, gathered, h1_flat, gates, p_flat, nple, wpg, wpp)


def _rope_table(s):
    half = ROPE_DIM // 2
    inv_freq = jnp.power(ROPE_THETA, -jnp.arange(half, dtype=F32) * (2.0 / ROPE_DIM))
    ang = jnp.arange(s, dtype=F32)[:, None] * inv_freq[None, :]
    base = jnp.concatenate([jnp.cos(ang), jnp.sin(ang)], axis=-1)
    place = np.zeros((2 * half, 3 * LANES), np.float32)
    ones = np.zeros((1, 3 * LANES), np.float32)
    for lane in range(3 * LANES):
        kind, d = lane // LANES, lane % HEAD_DIM
        if kind == 0 and d < ROPE_DIM:
            place[d % half, lane] = 1.0
        elif kind == 0:
            ones[0, lane] = 1.0
        elif kind == 1 and d < half:
            place[half + d, lane] = -1.0
        elif kind == 2 and half <= d < ROPE_DIM:
            place[d, lane] = 1.0
    return jnp.dot(base, place, precision=lax.Precision.HIGHEST) + ones


def _routing(route, counts, n_tok):
    ids = route[0:TOP_K]
    ranks = route[TOP_K:2 * TOP_K]
    counts = counts.astype(jnp.int32)
    padded = (counts + MOE_BLOCK - 1) // MOE_BLOCK * MOE_BLOCK
    pends = jnp.cumsum(padded)
    pstarts = pends - padded
    pos = ranks
    for e in range(N_EXPERTS):
        pos = pos + jnp.where(ids == e, pstarts[e], 0)
    n_slots = -(-(n_tok * TOP_K) // MOE_BLOCK) * MOE_BLOCK + N_EXPERTS * MOE_BLOCK
    n_blocks = n_slots // MOE_BLOCK
    first = jnp.arange(n_blocks, dtype=jnp.int32) * MOE_BLOCK
    block_e = jnp.clip(jnp.sum((first[:, None] >= pends[None, :]).astype(jnp.int32), axis=1), 0, N_EXPERTS - 1)
    own = block_e[:, None] == jnp.arange(N_EXPERTS, dtype=jnp.int32)[None, :]
    left = jnp.sum(jnp.where(own, (counts + pstarts)[None, :], 0), axis=1) - first
    n_valid = jnp.clip(left, 0, MOE_BLOCK)
    later = jnp.where(block_e[None, :] > block_e[:, None], block_e[None, :], N_EXPERTS)
    next_e = jnp.min(later, axis=1)
    next_e = jnp.where(next_e == N_EXPERTS, -1, next_e)
    present = jnp.any(own, axis=0)
    runs_before = jnp.sum((present[None, :] & (jnp.arange(N_EXPERTS)[None, :] < block_e[:, None])).astype(jnp.int32),
                          axis=1)
    parity = runs_before % 2
    return pos, (block_e, n_valid.astype(jnp.int32), next_e.astype(jnp.int32), parity.astype(jnp.int32)), n_slots


def _expand_rows(pos):
    spread = (jnp.arange(SUB * LANES, dtype=jnp.int32)[None, :] // SUB
              == jnp.arange(LANES, dtype=jnp.int32)[:, None]).astype(F32)
    wide = jnp.dot(pos.reshape(-1, LANES).astype(F32), spread, precision=lax.Precision.HIGHEST)
    wide = wide.astype(jnp.int32) * SUB + (jnp.arange(SUB * LANES, dtype=jnp.int32) % SUB)[None, :]
    return wide.reshape(-1, LANES)


def _group_forward(x, p_l, rope_tab, mix_prm, moe_prm, ple_prm):
    b, s, d = x.shape
    n_tok = b * s
    h1, hn_rows, route, gates, counts = _mixer_call(x, rope_tab, mix_prm)
    pos, block_info, n_slots = _routing(route, counts[:, 0], n_tok)
    slot_rows = _expand_rows(pos)
    xb = _sc_dispatch(hn_rows, slot_rows, n_slots * SUB)
    y = _moe_call(*block_info, xb, *moe_prm)
    gathered = _sc_gather(y, slot_rows)
    out = _combine_call(gathered, h1.reshape(n_tok, d), gates,
                        p_l.reshape(n_tok, PLE_DIM), *ple_prm)
    return out.reshape(b, s, d)


def kernel(x_prompt, x_sample, p_prompt, p_sample, norm_mix, w_in, q_gain, k_gain, attn_sink, gmlp_v_gain, gmlp_w_s, gmlp_b_s, w_branch, w_out, norm_ffn, w_router, b_router, w_gate_up, b_gate_up, w_down, b_down, norm_ple, w_ple_gate, w_ple_proj):
    depth = norm_mix.shape[0]
    hp, hs = x_prompt, x_sample
    for l in range(depth):
        row = lambda a: a.reshape(1, -1)
        blockdiag = jnp.kron(jnp.eye(N_Q_HEADS, dtype=F32),
                             jnp.full((HEAD_DIM, HEAD_DIM), 1.0 / HEAD_DIM, F32)).astype(BF16)
        sink_heads = attn_sink[l].reshape(N_KV_HEADS, GQA_GROUP)[:, jnp.array(HEAD_ORDER)]
        sink_col = jnp.repeat(sink_heads, ATTN_BLOCK, axis=1)[..., None]
        sink_rows = jnp.where(jnp.arange(LANES) == 0, sink_col, -jnp.inf)
        first_row = (jnp.arange(LANES) == 0)[:, None]
        upper = (jnp.arange(LANES) >= HEAD_DIM)[None, :]
        sink_values = jnp.stack([first_row & upper, first_row & ~upper]).astype(BF16)
        wcat = jnp.transpose(gmlp_w_s[l], (1, 0, 2)).reshape(GMLP_CHUNK, GMLP_GROUPS * GMLP_CHUNK).astype(BF16)
        bias_full = jnp.repeat(gmlp_b_s[l].T, GMLP_GROUP_DIM, axis=1)
        wr = jnp.pad(w_router[l], ((0, 0), (0, ROUTER_PAD - N_EXPERTS)))
        wr_hi = wr.astype(BF16)
        wr_lo = (wr - wr_hi.astype(F32)).astype(BF16)
        br = jnp.pad(b_router[l], (0, ROUTER_PAD - N_EXPERTS)).reshape(1, ROUTER_PAD)
        tri = (jnp.arange(MIX_TILE)[:, None] < jnp.arange(MIX_TILE)[None, :]).astype(BF16)
        mix_prm = (
            row(norm_mix[l]), w_in[l].astype(BF16),
            row(jnp.tile(q_gain[l], N_Q_HEADS)), row(jnp.tile(k_gain[l], N_KV_HEADS)),
            sink_rows, sink_values, row(gmlp_v_gain[l]), blockdiag, wcat, bias_full,
            w_branch[l].astype(BF16), w_out[l].astype(BF16), row(norm_ffn[l]),
            jnp.concatenate([wr_hi, wr_lo], axis=1), br, tri,
        )
        moe_prm = (
            w_gate_up[l], b_gate_up[l].reshape(N_EXPERTS, 1, 2 * D_FF),
            w_down[l], b_down[l].reshape(N_EXPERTS, 1, D_MODEL),
        )
        ple_prm = (row(norm_ple[l]), w_ple_gate[l].astype(BF16), w_ple_proj[l].astype(BF16))
        rope_tab = _rope_table(max(hp.shape[1], hs.shape[1]))
        hp = _group_forward(hp, p_prompt[l], rope_tab, mix_prm, moe_prm, ple_prm)
        hs = _group_forward(hs, p_sample[l], rope_tab, mix_prm, moe_prm, ple_prm)
    return (hp.astype(x_prompt.dtype), hs.astype(x_sample.dtype))
```

```python
import functools

import jax
import jax.numpy as jnp
import numpy as np
from jax import lax
from jax.experimental import pallas as pl
from jax.experimental.pallas import tpu as pltpu
from jax.experimental.pallas import tpu_sc as plsc

D_MODEL = 1024
HEAD_DIM = 64
N_Q_HEADS = 8
N_KV_HEADS = 2
GQA_GROUP = N_Q_HEADS // N_KV_HEADS
HEAD_ORDER = (0, 2, 1, 3)
ATTN_WIDTH = N_Q_HEADS * HEAD_DIM
KV_WIDTH = N_KV_HEADS * HEAD_DIM
WINDOW = 128
ATTN_BLOCK = 128
ROPE_THETA = 500000.0
ROPE_DIM = HEAD_DIM // 4
GMLP_WIDTH = D_MODEL // 2
GMLP_GROUPS = 8
GMLP_GROUP_DIM = GMLP_WIDTH // GMLP_GROUPS
GMLP_CHUNK = 128
N_BRANCH = 2
IN_WIDTH = ATTN_WIDTH + 2 * KV_WIDTH + 2 * GMLP_WIDTH + N_BRANCH * D_MODEL
N_EXPERTS = 32
TOP_K = 4
D_FF = D_MODEL
SWIGLU_ALPHA = 1.702
SWIGLU_LIMIT = 7.0
PLE_DIM = 256
EPS = 1e-6

Q_OFF = 0
K_OFF = ATTN_WIDTH
V_OFF = K_OFF + KV_WIDTH
U_OFF = V_OFF + KV_WIDTH
VG_OFF = U_OFF + GMLP_WIDTH
GL_OFF = VG_OFF + GMLP_WIDTH

LANES = 128
PACK_WORDS = D_MODEL // 2
SUB = PACK_WORDS // LANES
MIX_TILE = 512
PROJ_COLS = 256
MOE_BLOCK = 1024
MOE_ROWS_STEP = 256
MOE_COLS = 256
CMB_TILE = 1024
ROUTER_PAD = 128
SC_CHUNK = 128
DISPATCH_TOKENS = 64
GATHER_STREAMS = 4
VMEM_LIMIT = 56 * 1024 * 1024

BF16 = jnp.bfloat16
F32 = jnp.float32
U32 = jnp.uint32


def _rms(x, gain):
    return x * lax.rsqrt(jnp.mean(x * x, axis=-1, keepdims=True) + EPS) * gain


def _gelu(x):
    return 0.5 * x * (1.0 + lax.erf(x * np.float32(np.sqrt(0.5))))


def _sigmoid(x):
    return 0.5 * jnp.tanh(0.5 * x) + 0.5


def _head_rms(x, blockdiag, gain):
    ms = jnp.dot((x * x).astype(BF16), blockdiag, preferred_element_type=F32)
    return x * lax.rsqrt(ms + EPS) * gain


def _rope(x, cos, sin_lo, sin_hi):
    w = x.shape[-1]
    return x * cos + pltpu.roll(x, w - ROPE_DIM // 2, 1) * sin_lo + pltpu.roll(x, ROPE_DIM // 2, 1) * sin_hi


def _value_blocks(v):
    swapped = pltpu.roll(v, HEAD_DIM, 1)
    low = lax.broadcasted_iota(jnp.int32, v.shape, 1) < HEAD_DIM
    blocks = [jnp.where(low, v, 1.0), jnp.where(low, 1.0, swapped), jnp.where(low, swapped, 1.0), jnp.where(low, 1.0, v)]
    return jnp.concatenate(blocks, axis=1).astype(BF16)


def _tile_lanes(t, reps):
    return t if reps == 1 else jnp.concatenate([t] * reps, axis=-1)


def _store_packed(ref, x):
    rows = x.shape[0]
    hi = lax.bitcast_convert_type(x[:, :PACK_WORDS].astype(BF16).astype(F32), U32)
    lo = lax.bitcast_convert_type(x[:, PACK_WORDS:].astype(BF16).astype(F32), U32)
    words = hi | (lo >> 16)
    for j in range(SUB):
        ref[pl.ds(j, rows, stride=SUB), :] = words[:, j * LANES:(j + 1) * LANES]


def _load_packed(ref, rows):
    words = jnp.concatenate([ref[pl.ds(j, rows, stride=SUB), :] for j in range(SUB)], axis=1)
    left = lax.bitcast_convert_type(words & np.uint32(0xFFFF0000), F32)
    right = lax.bitcast_convert_type(words << 16, F32)
    return jnp.concatenate([left, right], axis=1)


def _mixer_kernel(x_ref, xn_ref, rp_ref, rpn_ref,
                  nmix_ref, win_ref, qg_ref, kg_ref, sink_ref, sinkv_ref, vgain_ref, bdq_ref,
                  wcat_ref, bias_ref, wbr_ref, wout_ref, nffn_ref, wr_ref, br_ref, tri_ref,
                  h1_ref, hn_ref, route_ref, gate_ref, cnt_out_ref,
                  z_ref, q_ref, k_ref, v_ref, attn_ref, gm_ref, cnt_ref, *, n_blocks_seq):
    ts = MIX_TILE
    i = pl.program_id(1)
    x = x_ref[0]

    @pl.when(i > 0)
    def _():
        k_ref[0:ATTN_BLOCK, :] = k_ref[ts:ts + ATTN_BLOCK, :]
        v_ref[0:ATTN_BLOCK, :] = v_ref[ts:ts + ATTN_BLOCK, :]

    @pl.when(i == 0)
    def _():
        k_ref[0:ATTN_BLOCK, :] = jnp.zeros((ATTN_BLOCK, k_ref.shape[1]), BF16)
        v_ref[0:ATTN_BLOCK, :] = jnp.zeros((ATTN_BLOCK, v_ref.shape[1]), BF16)

    xn = _rms(x, nmix_ref[...]).astype(BF16)

    def project(lo, hi):
        z_ref[:, lo:hi] = jnp.dot(xn, win_ref[:, lo:hi], preferred_element_type=F32)

    project(Q_OFF, U_OFF)
    n_items = (ts // ATTN_BLOCK) * N_KV_HEADS
    later = [(U_OFF + t * PROJ_COLS, U_OFF + (t + 1) * PROJ_COLS) for t in range((IN_WIDTH - U_OFF) // PROJ_COLS)]
    per_item = [len(later) // n_items + (1 if n < len(later) % n_items else 0) for n in range(n_items)]

    cos = rp_ref[:, 0:LANES]
    sin_lo = rp_ref[:, LANES:2 * LANES]
    sin_hi = rp_ref[:, 2 * LANES:3 * LANES]
    bdq = bdq_ref[...]
    bdk = bdq_ref[0:KV_WIDTH, 0:KV_WIDTH]

    q = _head_rms(z_ref[:, Q_OFF:Q_OFF + ATTN_WIDTH], bdq, qg_ref[...])
    reps = ATTN_WIDTH // LANES
    q = _rope(q, _tile_lanes(cos, reps), _tile_lanes(sin_lo, reps), _tile_lanes(sin_hi, reps))
    q_ref[...] = (q * (HEAD_DIM ** -0.5)).astype(BF16)

    k = _head_rms(z_ref[:, K_OFF:K_OFF + KV_WIDTH], bdk, kg_ref[...])
    k_ref[ATTN_BLOCK:ATTN_BLOCK + ts, :] = _rope(k, cos, sin_lo, sin_hi).astype(BF16)
    v_ref[ATTN_BLOCK:ATTN_BLOCK + ts, :] = _value_blocks(z_ref[:, V_OFF:V_OFF + KV_WIDTH])

    xhn = _rms(xn_ref[0], nmix_ref[...]).astype(BF16)
    zh = jnp.dot(xhn, win_ref[:, K_OFF:K_OFF + 2 * KV_WIDTH], preferred_element_type=F32)
    kh = _head_rms(zh[:, 0:KV_WIDTH], bdk, kg_ref[...])
    kh = _rope(kh, rpn_ref[:, 0:LANES], rpn_ref[:, LANES:2 * LANES], rpn_ref[:, 2 * LANES:3 * LANES])
    k_ref[ATTN_BLOCK + ts:2 * ATTN_BLOCK + ts, :] = kh.astype(BF16)
    v_ref[ATTN_BLOCK + ts:2 * ATTN_BLOCK + ts, :] = _value_blocks(zh[:, KV_WIDTH:2 * KV_WIDTH])

    rows = GQA_GROUP * ATTN_BLOCK
    keys = 3 * ATTN_BLOCK
    r = lax.broadcasted_iota(jnp.int32, (rows, keys), 0) % ATTN_BLOCK
    c = lax.broadcasted_iota(jnp.int32, (rows, keys), 1)
    band = (c >= r) & (c <= r + 2 * WINDOW)
    low = lax.broadcasted_iota(jnp.int32, (ATTN_BLOCK, 2 * HEAD_DIM), 1) < HEAD_DIM
    pair = 2 * ATTN_BLOCK
    items = [(qb, j) for qb in range(ts // ATTN_BLOCK) for j in range(N_KV_HEADS)]

    def scores(n):
        qb, j = items[n]
        r0 = qb * ATTN_BLOCK
        q4 = jnp.concatenate(
            [q_ref[r0:r0 + ATTN_BLOCK, (GQA_GROUP * j + g) * HEAD_DIM:(GQA_GROUP * j + g + 1) * HEAD_DIM]
             for g in HEAD_ORDER], axis=0)
        kw = k_ref[r0:r0 + keys, j * HEAD_DIM:(j + 1) * HEAD_DIM]
        return lax.dot_general(q4, kw, (((1,), (1,)), ((), ())), preferred_element_type=F32)

    s_next = scores(0)
    for n, (qb, j) in enumerate(items):
        s = s_next
        if n + 1 < len(items):
            s_next = scores(n + 1)
        for _ in range(per_item[n]):
            project(*later.pop(0))
        gb = i * (ts // ATTN_BLOCK) + qb
        lo = jnp.where(gb == 0, ATTN_BLOCK, 0)
        hi = jnp.where(gb == n_blocks_seq - 1, 2 * ATTN_BLOCK, keys)
        valid = band & (c >= lo) & (c < hi)
        r0 = qb * ATTN_BLOCK
        s = jnp.concatenate([jnp.where(valid, s, -jnp.inf), sink_ref[j]], axis=1)
        p = jnp.exp(s - jnp.max(s, axis=-1, keepdims=True)).astype(BF16)
        outs = []
        for par in range(2):
            vw = jnp.concatenate([v_ref[r0:r0 + keys, (2 * j + par) * LANES:(2 * j + par + 1) * LANES],
                                  sinkv_ref[par]], axis=0)
            o = jnp.dot(p[par * pair:(par + 1) * pair], vw, preferred_element_type=F32)
            outs.append(o / pltpu.roll(o, HEAD_DIM, 1))
        for a in range(GQA_GROUP // 2):
            both = jnp.where(low, outs[0][a * ATTN_BLOCK:(a + 1) * ATTN_BLOCK],
                             outs[1][a * ATTN_BLOCK:(a + 1) * ATTN_BLOCK])
            h0 = GQA_GROUP * j + 2 * a
            attn_ref[r0:r0 + ATTN_BLOCK, h0 * HEAD_DIM:(h0 + 2) * HEAD_DIM] = both.astype(BF16)

    vgn_all = _head_rms(_gelu(z_ref[:, VG_OFF:VG_OFF + GMLP_WIDTH]), bdq, vgain_ref[...])
    half = GMLP_WIDTH // 2
    gpm = half // GMLP_GROUP_DIM
    lane_grp = lax.broadcasted_iota(jnp.int32, (GMLP_CHUNK, half), 1) // GMLP_GROUP_DIM
    mixed = []
    for ch in range(ts // GMLP_CHUNK):
        c0 = ch * GMLP_CHUNK
        parts = []
        for nt in range(2):
            part = vgn_all[c0:c0 + GMLP_CHUNK, nt * half:(nt + 1) * half]
            vexp = jnp.concatenate(
                [jnp.where(lane_grp == gl, part, 0.0).astype(BF16) for gl in range(gpm)], axis=0)
            wpart = wcat_ref[:, nt * gpm * GMLP_CHUNK:(nt + 1) * gpm * GMLP_CHUNK]
            parts.append(jnp.dot(wpart, vexp, preferred_element_type=F32))
        mixed.append(jnp.concatenate(parts, axis=-1))
    for ch in range(ts // GMLP_CHUNK):
        c0 = ch * GMLP_CHUNK
        u = _gelu(z_ref[c0:c0 + GMLP_CHUNK, U_OFF:U_OFF + GMLP_WIDTH])
        gm_ref[c0:c0 + GMLP_CHUNK, :] = (u * (mixed[ch] + bias_ref[...])).astype(BF16)

    ya = jnp.dot(attn_ref[...], wbr_ref[0], preferred_element_type=F32)
    yg = jnp.dot(gm_ref[...], wbr_ref[1], preferred_element_type=F32)
    ga = _sigmoid(z_ref[:, GL_OFF:GL_OFF + D_MODEL])
    gg = _sigmoid(z_ref[:, GL_OFF + D_MODEL:GL_OFF + 2 * D_MODEL])
    merged = (ga * ya + gg * yg).astype(BF16)
    h1 = x + jnp.dot(merged, wout_ref[...], preferred_element_type=F32)
    h1_ref[0] = h1

    hn = _rms(h1, nffn_ref[...])
    _store_packed(hn_ref, hn)

    hn_hi = hn.astype(BF16)
    hn_lo = (hn - hn_hi.astype(F32)).astype(BF16)
    by_hi = jnp.dot(hn_hi, wr_ref[...], preferred_element_type=F32)
    by_lo = jnp.dot(hn_lo, wr_ref[:, 0:ROUTER_PAD], preferred_element_type=F32)
    logits = by_hi[:, 0:ROUTER_PAD] + by_hi[:, ROUTER_PAD:2 * ROUTER_PAD] + by_lo + br_ref[...]
    lt = jnp.transpose(logits)[0:N_EXPERTS, :]
    eid = lax.broadcasted_iota(jnp.int32, lt.shape, 0)
    vals, ids = [], []
    for _ in range(TOP_K):
        mx = jnp.max(lt, axis=0, keepdims=True)
        am = jnp.min(jnp.where(lt == mx, eid, N_EXPERTS), axis=0, keepdims=True)
        vals.append(mx)
        ids.append(am)
        lt = jnp.where(eid == am, -jnp.inf, lt)
    ex = [jnp.exp(v - vals[0]) for v in vals]
    tot = ex[0] + ex[1] + ex[2] + ex[3]
    gate_ref[...] = jnp.concatenate([e / tot for e in ex] + [jnp.zeros_like(tot)] * (8 - TOP_K), axis=0)

    @pl.when((pl.program_id(0) == 0) & (i == 0))
    def _():
        cnt_ref[...] = jnp.zeros_like(cnt_ref)

    sel = [eid == a for a in ids]
    member = (sel[0] | sel[1] | sel[2] | sel[3]).astype(F32)
    before = jnp.dot(member.astype(BF16), tri_ref[...], preferred_element_type=F32) + cnt_ref[:, 0:1]
    ranks = [jnp.sum(jnp.where(s_, before, 0.0), axis=0, keepdims=True).astype(jnp.int32) for s_ in sel]
    route_ref[...] = jnp.concatenate(ids + ranks, axis=0)
    cnt_ref[...] = cnt_ref[...] + jnp.sum(member, axis=1, keepdims=True)
    cnt_out_ref[...] = cnt_ref[...]


def _mixer_call(x, rope_tab, prm):
    b, s, d = x.shape
    ts = MIX_TILE
    nt = s // ts
    nb = s // ATTN_BLOCK
    per = ts // ATTN_BLOCK
    const2 = lambda bi, i: (0, 0)
    const3 = lambda bi, i: (0, 0, 0)

    def wspec(arr):
        return pl.BlockSpec(arr.shape, const2 if arr.ndim == 2 else const3)

    in_specs = [
        pl.BlockSpec((1, ts, d), lambda bi, i: (bi, i, 0)),
        pl.BlockSpec((1, ATTN_BLOCK, d), lambda bi, i: (bi, jnp.minimum((i + 1) * per, nb - 1), 0)),
        pl.BlockSpec((ts, 3 * LANES), lambda bi, i: (i, 0)),
        pl.BlockSpec((ATTN_BLOCK, 3 * LANES), lambda bi, i: (jnp.minimum((i + 1) * per, nb - 1), 0)),
    ] + [wspec(a) for a in prm]
    out_shape = [
        jax.ShapeDtypeStruct((b, s, d), F32),
        jax.ShapeDtypeStruct((b * s * SUB, LANES), U32),
        jax.ShapeDtypeStruct((2 * TOP_K, b * s), jnp.int32),
        jax.ShapeDtypeStruct((2 * TOP_K, b * s), F32),
        jax.ShapeDtypeStruct((N_EXPERTS, LANES), F32),
    ]
    out_specs = [
        pl.BlockSpec((1, ts, d), lambda bi, i: (bi, i, 0)),
        pl.BlockSpec((ts * SUB, LANES), lambda bi, i: (bi * nt + i, 0)),
        pl.BlockSpec((2 * TOP_K, ts), lambda bi, i: (0, bi * nt + i)),
        pl.BlockSpec((2 * TOP_K, ts), lambda bi, i: (0, bi * nt + i)),
        pl.BlockSpec((N_EXPERTS, LANES), const2),
    ]
    scratch = [
        pltpu.VMEM((ts, IN_WIDTH), F32),
        pltpu.VMEM((ts, ATTN_WIDTH), BF16),
        pltpu.VMEM((ts + 2 * ATTN_BLOCK, KV_WIDTH), BF16),
        pltpu.VMEM((ts + 2 * ATTN_BLOCK, 2 * N_KV_HEADS * LANES), BF16),
        pltpu.VMEM((ts, ATTN_WIDTH), BF16),
        pltpu.VMEM((ts, GMLP_WIDTH), BF16),
        pltpu.VMEM((N_EXPERTS, LANES), F32),
    ]
    return pl.pallas_call(
        functools.partial(_mixer_kernel, n_blocks_seq=nb),
        grid=(b, nt),
        in_specs=in_specs,
        out_specs=out_specs,
        out_shape=out_shape,
        scratch_shapes=scratch,
        compiler_params=pltpu.CompilerParams(
            dimension_semantics=("arbitrary", "arbitrary"), vmem_limit_bytes=VMEM_LIMIT),
        name="mixer",
    )(x, x, rope_tab, rope_tab, *prm)


def _sc_workers():
    info = plsc.get_sparse_core_info()
    return info.num_cores, info.num_cores * info.num_subcores


def _sc_dispatch(rows2d, slot_rows, n_out_rows):
    n_cores, n_workers = _sc_workers()
    chunk_rows = DISPATCH_TOKENS * SUB
    halves = chunk_rows // SC_CHUNK
    n_chunks = rows2d.shape[0] // chunk_rows
    per_w = n_chunks // n_workers
    idx_rows_per_choice = rows2d.shape[0] // SC_CHUNK
    mesh = plsc.VectorSubcoreMesh(core_axis_name="c", subcore_axis_name="s")

    @functools.partial(
        pl.kernel, mesh=mesh,
        out_type=jax.ShapeDtypeStruct((n_out_rows, LANES), rows2d.dtype),
        scratch_types=[pltpu.VMEM((TOP_K * halves, SC_CHUNK), jnp.int32),
                       pltpu.VMEM((chunk_rows, LANES), rows2d.dtype),
                       pltpu.SemaphoreType.DMA, pltpu.SemaphoreType.DMA],
    )
    def k(src_hbm, idx_hbm, out_hbm, idx_v, rows_v, sem_in, sem_out):
        wid = lax.axis_index("s") * n_cores + lax.axis_index("c")

        @pl.loop(0, per_w)
        def _(j):
            c = wid * per_w + j
            loads = [pltpu.async_copy(src_hbm.at[pl.ds(c * chunk_rows, chunk_rows)], rows_v, sem_in)]
            loads += [
                pltpu.async_copy(idx_hbm.at[pl.ds(kk * idx_rows_per_choice + c * halves, halves)],
                                 idx_v.at[pl.ds(kk * halves, halves)], sem_in)
                for kk in range(TOP_K)]
            for cp in loads:
                cp.wait()
            copies = [
                pltpu.async_copy(rows_v.at[pl.ds((q % halves) * SC_CHUNK, SC_CHUNK)],
                                 out_hbm.at[idx_v.at[q]], sem_out)
                for q in range(TOP_K * halves)]
            for cp in copies:
                cp.wait()

    return k(rows2d, slot_rows)


def _sc_gather(table2d, idx2d):
    n_cores, n_workers = _sc_workers()
    n = idx2d.shape[0] * SC_CHUNK
    per_w = n // n_workers
    step_rows = GATHER_STREAMS * SC_CHUNK
    idx_tile = 8
    mesh = plsc.VectorSubcoreMesh(core_axis_name="c", subcore_axis_name="s")

    @functools.partial(
        pl.kernel, mesh=mesh,
        out_type=jax.ShapeDtypeStruct((n, LANES), table2d.dtype),
        scratch_types=[pltpu.VMEM((idx_tile, SC_CHUNK), jnp.int32),
                       pltpu.VMEM((step_rows, LANES), table2d.dtype),
                       pltpu.SemaphoreType.DMA],
    )
    def k(table_hbm, idx_hbm, out_hbm, idx_v, rows_v, sem):
        wid = lax.axis_index("s") * n_cores + lax.axis_index("c")

        @pl.loop(0, per_w // (idx_tile * SC_CHUNK))
        def _(j):
            row0 = pl.multiple_of(wid * (per_w // SC_CHUNK) + j * idx_tile, idx_tile)
            pltpu.sync_copy(idx_hbm.at[pl.ds(row0, idx_tile)], idx_v)
            for part in range(idx_tile // GATHER_STREAMS):
                copies = [pltpu.async_copy(table_hbm.at[idx_v.at[part * GATHER_STREAMS + q]],
                                           rows_v.at[pl.ds(q * SC_CHUNK, SC_CHUNK)], sem)
                          for q in range(GATHER_STREAMS)]
                for cp in copies:
                    cp.wait()
                pltpu.sync_copy(rows_v, out_hbm.at[pl.ds((row0 + part * GATHER_STREAMS) * SC_CHUNK, step_rows)])

    return k(table2d, idx2d)


def _expert_weight_copies(wgu_hbm, wd_hbm, wgu_buf, wd_buf, sem, expert, slot):
    return (pltpu.make_async_copy(wgu_hbm.at[expert], wgu_buf.at[slot], sem.at[0, slot]),
            pltpu.make_async_copy(wd_hbm.at[expert], wd_buf.at[slot], sem.at[1, slot]))


def _moe_kernel(be_ref, nv_ref, nxt_ref, par_ref, x_ref, wgu_hbm, bgu_ref, wd_hbm, bd_ref, y_ref,
                wgu_buf, wd_buf, sem, wgu_s, wd_s, slab):
    i = pl.program_id(0)
    nv = nv_ref[i]
    slot = par_ref[i]

    cw = MOE_COLS
    n_chunks = D_FF // cw
    copies = functools.partial(_expert_weight_copies, wgu_hbm, wd_hbm, wgu_buf, wd_buf, sem)

    @pl.when(i == 0)
    def _():
        for cp in copies(be_ref[0], slot):
            cp.start()

    @pl.when((i == 0) | (be_ref[i] != be_ref[jnp.maximum(i - 1, 0)]))
    def _():
        for cp in copies(be_ref[i], slot):
            cp.wait()

        @pl.when(nxt_ref[i] >= 0)
        def _():
            for cp in copies(nxt_ref[i], 1 - slot):
                cp.start()

        for c in range(n_chunks):
            wgu_s[:, 2 * c * cw:(2 * c + 1) * cw] = wgu_buf[slot, :, c * cw:(c + 1) * cw].astype(BF16)
            wgu_s[:, (2 * c + 1) * cw:(2 * c + 2) * cw] = (
                wgu_buf[slot, :, D_FF + c * cw:D_FF + (c + 1) * cw].astype(BF16))
        half = D_FF // 2
        for c in range(D_MODEL // LANES):
            slab[c, pl.ds(0, half, stride=2), :] = wd_buf[slot, 0:half, c * LANES:(c + 1) * LANES]
            slab[c, pl.ds(1, half, stride=2), :] = wd_buf[slot, half:D_FF, c * LANES:(c + 1) * LANES]
        for c in range(D_MODEL // LANES):
            wd_s[:, c * LANES:(c + 1) * LANES] = slab[c].astype(BF16)

    expert = be_ref[i]

    def run_expert(rows):
        x = _load_packed(x_ref, rows)
        live = lax.broadcasted_iota(jnp.int32, x.shape, 0) < nv
        xe = jnp.where(live, x, 0.0).astype(BF16)
        even = (lax.broadcasted_iota(jnp.int32, (rows, cw), 1) % 2) == 0
        y = bd_ref[expert]

        def gate_up(c):
            return jnp.dot(xe, wgu_s[:, 2 * c * cw:(2 * c + 2) * cw], preferred_element_type=F32)

        h_next = gate_up(0)
        for c in range(n_chunks):
            h = h_next
            if c + 1 < n_chunks:
                h_next = gate_up(c + 1)
            h_a = h[:, 0:cw] + bgu_ref[expert, :, c * cw:(c + 1) * cw]
            h_b = h[:, cw:2 * cw] + bgu_ref[expert, :, D_FF + c * cw:D_FF + (c + 1) * cw]
            gate = jnp.where(even, h_a, pltpu.roll(h_b, 1, 1))
            up = jnp.where(even, pltpu.roll(h_a, cw - 1, 1), h_b)
            gate = jnp.minimum(gate, SWIGLU_LIMIT)
            up = jnp.clip(up, -SWIGLU_LIMIT, SWIGLU_LIMIT)
            act = (up + 1.0) * (gate * jax.nn.sigmoid(SWIGLU_ALPHA * gate))
            y = y + jnp.dot(act.astype(BF16), wd_s[c * cw:(c + 1) * cw, :], preferred_element_type=F32)
        _store_packed(y_ref, y)

    for rows in range(MOE_ROWS_STEP, MOE_BLOCK + 1, MOE_ROWS_STEP):
        @pl.when((nv > rows - MOE_ROWS_STEP) & (nv <= rows))
        def _(rows=rows):
            run_expert(rows)


def _moe_call(block_e, n_valid, next_e, parity, xb, wgu, bgu, wd, bd):
    n_blocks = block_e.shape[0]
    d = D_MODEL
    grid_spec = pltpu.PrefetchScalarGridSpec(
        num_scalar_prefetch=4,
        grid=(n_blocks,),
        in_specs=[
            pl.BlockSpec((MOE_BLOCK * SUB, LANES), lambda i, be, nv, nx, pr: (i, 0)),
            pl.BlockSpec(memory_space=pl.ANY),
            pl.BlockSpec((N_EXPERTS, 1, 2 * D_FF), lambda i, be, nv, nx, pr: (0, 0, 0)),
            pl.BlockSpec(memory_space=pl.ANY),
            pl.BlockSpec((N_EXPERTS, 1, d), lambda i, be, nv, nx, pr: (0, 0, 0)),
        ],
        out_specs=pl.BlockSpec((MOE_BLOCK * SUB, LANES), lambda i, be, nv, nx, pr: (i, 0)),
        scratch_shapes=[pltpu.VMEM((2, d, 2 * D_FF), F32), pltpu.VMEM((2, D_FF, d), F32),
                        pltpu.SemaphoreType.DMA((2, 2)),
                        pltpu.VMEM((d, 2 * D_FF), BF16), pltpu.VMEM((D_FF, d), BF16),
                        pltpu.VMEM((d // LANES, D_FF, LANES), F32)],
    )
    return pl.pallas_call(
        _moe_kernel,
        grid_spec=grid_spec,
        out_shape=jax.ShapeDtypeStruct((n_blocks * MOE_BLOCK * SUB, LANES), U32),
        compiler_params=pltpu.CompilerParams(
            dimension_semantics=("arbitrary",), vmem_limit_bytes=VMEM_LIMIT),
        name="moe",
    )(block_e, n_valid, next_e, parity, xb, wgu, bgu, wd, bd)


def _combine_kernel(g0_ref, g1_ref, g2_ref, g3_ref, h1_ref, gate_ref, p_ref, nple_ref, wpg_ref, wpp_ref, o_ref):
    g = jnp.transpose(jnp.concatenate([gate_ref[...], jnp.zeros((LANES - 2 * TOP_K, CMB_TILE), F32)], axis=0))
    moe = jnp.zeros((CMB_TILE, D_MODEL), F32)
    for kk, g_ref in enumerate((g0_ref, g1_ref, g2_ref, g3_ref)):
        moe = moe + _load_packed(g_ref, CMB_TILE) * g[:, kk:kk + 1]
    h2 = h1_ref[...] + moe
    hp = _rms(h2, nple_ref[...]).astype(BF16)
    gate = jax.nn.sigmoid(jnp.dot(hp, wpg_ref[...], preferred_element_type=F32))
    proj = jnp.dot(p_ref[...].astype(BF16), wpp_ref[...], preferred_element_type=F32)
    o_ref[...] = h2 + gate * proj


def _combine_call(gathered, h1_flat, gates, p_flat, nple, wpg, wpp):
    n_tok, d = h1_flat.shape
    tc = CMB_TILE
    n_tiles = n_tok // tc
    row = lambda i: (i, 0)
    const = lambda i: (0, 0)
    g_specs = [pl.BlockSpec((tc * SUB, LANES), functools.partial(lambda i, kk: (kk * n_tiles + i, 0), kk=kk))
               for kk in range(TOP_K)]
    return pl.pallas_call(
        _combine_kernel,
        grid=(n_tiles,),
        in_specs=g_specs + [
            pl.BlockSpec((tc, d), row),
            pl.BlockSpec((2 * TOP_K, tc), lambda i: (0, i)),
            pl.BlockSpec((tc, PLE_DIM), row),
            pl.BlockSpec((1, d), const),
            pl.BlockSpec((d, d), const),
            pl.BlockSpec((PLE_DIM, d), const),
        ],
        out_specs=pl.BlockSpec((tc, d), row),
        out_shape=jax.ShapeDtypeStruct((n_tok, d), F32),
        compiler_params=pltpu.CompilerParams(
            dimension_semantics=("arbitrary",), vmem_limit_bytes=VMEM_LIMIT),
        name="combine",
    )(gathered, gathered, gathered, gathered, h1_flat, gates, p_flat, nple, wpg, wpp)


def _rope_table(s):
    half = ROPE_DIM // 2
    inv_freq = jnp.power(ROPE_THETA, -jnp.arange(half, dtype=F32) * (2.0 / ROPE_DIM))
    ang = jnp.arange(s, dtype=F32)[:, None] * inv_freq[None, :]
    base = jnp.concatenate([jnp.cos(ang), jnp.sin(ang)], axis=-1)
    place = np.zeros((2 * half, 3 * LANES), np.float32)
    ones = np.zeros((1, 3 * LANES), np.float32)
    for lane in range(3 * LANES):
        kind, d = lane // LANES, lane % HEAD_DIM
        if kind == 0 and d < ROPE_DIM:
            place[d % half, lane] = 1.0
        elif kind == 0:
            ones[0, lane] = 1.0
        elif kind == 1 and d < half:
            place[half + d, lane] = -1.0
        elif kind == 2 and half <= d < ROPE_DIM:
            place[d, lane] = 1.0
    return jnp.dot(base, place, precision=lax.Precision.HIGHEST) + ones


def _routing(route, counts, n_tok):
    ids = route[0:TOP_K]
    ranks = route[TOP_K:2 * TOP_K]
    counts = counts.astype(jnp.int32)
    padded = (counts + MOE_BLOCK - 1) // MOE_BLOCK * MOE_BLOCK
    pends = jnp.cumsum(padded)
    pstarts = pends - padded
    pos = ranks
    for e in range(N_EXPERTS):
        pos = pos + jnp.where(ids == e, pstarts[e], 0)
    n_slots = -(-(n_tok * TOP_K) // MOE_BLOCK) * MOE_BLOCK + N_EXPERTS * MOE_BLOCK
    n_blocks = n_slots // MOE_BLOCK
    first = jnp.arange(n_blocks, dtype=jnp.int32) * MOE_BLOCK
    block_e = jnp.clip(jnp.sum((first[:, None] >= pends[None, :]).astype(jnp.int32), axis=1), 0, N_EXPERTS - 1)
    own = block_e[:, None] == jnp.arange(N_EXPERTS, dtype=jnp.int32)[None, :]
    left = jnp.sum(jnp.where(own, (counts + pstarts)[None, :], 0), axis=1) - first
    n_valid = jnp.clip(left, 0, MOE_BLOCK)
    later = jnp.where(block_e[None, :] > block_e[:, None], block_e[None, :], N_EXPERTS)
    next_e = jnp.min(later, axis=1)
    next_e = jnp.where(next_e == N_EXPERTS, -1, next_e)
    present = jnp.any(own, axis=0)
    runs_before = jnp.sum((present[None, :] & (jnp.arange(N_EXPERTS)[None, :] < block_e[:, None])).astype(jnp.int32),
                          axis=1)
    parity = runs_before % 2
    return pos, (block_e, n_valid.astype(jnp.int32), next_e.astype(jnp.int32), parity.astype(jnp.int32)), n_slots


def _expand_rows(pos):
    spread = (jnp.arange(SUB * LANES, dtype=jnp.int32)[None, :] // SUB
              == jnp.arange(LANES, dtype=jnp.int32)[:, None]).astype(F32)
    wide = jnp.dot(pos.reshape(-1, LANES).astype(F32), spread, precision=lax.Precision.HIGHEST)
    wide = wide.astype(jnp.int32) * SUB + (jnp.arange(SUB * LANES, dtype=jnp.int32) % SUB)[None, :]
    return wide.reshape(-1, LANES)


def _group_forward(x, p_l, rope_tab, mix_prm, moe_prm, ple_prm):
    b, s, d = x.shape
    n_tok = b * s
    h1, hn_rows, route, gates, counts = _mixer_call(x, rope_tab, mix_prm)
    pos, block_info, n_slots = _routing(route, counts[:, 0], n_tok)
    slot_rows = _expand_rows(pos)
    xb = _sc_dispatch(hn_rows, slot_rows, n_slots * SUB)
    y = _moe_call(*block_info, xb, *moe_prm)
    gathered = _sc_gather(y, slot_rows)
    out = _combine_call(gathered, h1.reshape(n_tok, d), gates,
                        p_l.reshape(n_tok, PLE_DIM), *ple_prm)
    return out.reshape(b, s, d)


def kernel(x_prompt, x_sample, p_prompt, p_sample, norm_mix, w_in, q_gain, k_gain, attn_sink, gmlp_v_gain, gmlp_w_s, gmlp_b_s, w_branch, w_out, norm_ffn, w_router, b_router, w_gate_up, b_gate_up, w_down, b_down, norm_ple, w_ple_gate, w_ple_proj):
    depth = norm_mix.shape[0]
    hp, hs = x_prompt, x_sample
    for l in range(depth):
        row = lambda a: a.reshape(1, -1)
        blockdiag = jnp.kron(jnp.eye(N_Q_HEADS, dtype=F32),
                             jnp.full((HEAD_DIM, HEAD_DIM), 1.0 / HEAD_DIM, F32)).astype(BF16)
        sink_heads = attn_sink[l].reshape(N_KV_HEADS, GQA_GROUP)[:, jnp.array(HEAD_ORDER)]
        sink_col = jnp.repeat(sink_heads, ATTN_BLOCK, axis=1)[..., None]
        sink_rows = jnp.where(jnp.arange(LANES) == 0, sink_col, -jnp.inf)
        first_row = (jnp.arange(LANES) == 0)[:, None]
        upper = (jnp.arange(LANES) >= HEAD_DIM)[None, :]
        sink_values = jnp.stack([first_row & upper, first_row & ~upper]).astype(BF16)
        wcat = jnp.transpose(gmlp_w_s[l], (1, 0, 2)).reshape(GMLP_CHUNK, GMLP_GROUPS * GMLP_CHUNK).astype(BF16)
        bias_full = jnp.repeat(gmlp_b_s[l].T, GMLP_GROUP_DIM, axis=1)
        wr = jnp.pad(w_router[l], ((0, 0), (0, ROUTER_PAD - N_EXPERTS)))
        wr_hi = wr.astype(BF16)
        wr_lo = (wr - wr_hi.astype(F32)).astype(BF16)
        br = jnp.pad(b_router[l], (0, ROUTER_PAD - N_EXPERTS)).reshape(1, ROUTER_PAD)
        tri = (jnp.arange(MIX_TILE)[:, None] < jnp.arange(MIX_TILE)[None, :]).astype(BF16)
        mix_prm = (
            row(norm_mix[l]), w_in[l].astype(BF16),
            row(jnp.tile(q_gain[l], N_Q_HEADS)), row(jnp.tile(k_gain[l], N_KV_HEADS)),
            sink_rows, sink_values, row(gmlp_v_gain[l]), blockdiag, wcat, bias_full,
            w_branch[l].astype(BF16), w_out[l].astype(BF16), row(norm_ffn[l]),
            jnp.concatenate([wr_hi, wr_lo], axis=1), br, tri,
        )
        moe_prm = (
            w_gate_up[l], b_gate_up[l].reshape(N_EXPERTS, 1, 2 * D_FF),
            w_down[l], b_down[l].reshape(N_EXPERTS, 1, D_MODEL),
        )
        ple_prm = (row(norm_ple[l]), w_ple_gate[l].astype(BF16), w_ple_proj[l].astype(BF16))
        rope_tab = _rope_table(max(hp.shape[1], hs.shape[1]))
        hp = _group_forward(hp, p_prompt[l], rope_tab, mix_prm, moe_prm, ple_prm)
        hs = _group_forward(hs, p_sample[l], rope_tab, mix_prm, moe_prm, ple_prm)
    return (hp.astype(x_prompt.dtype), hs.astype(x_sample.dtype))
```

```python
import functools

import jax
import jax.numpy as jnp
import numpy as np
from jax import lax
from jax.experimental import pallas as pl
from jax.experimental.pallas import tpu as pltpu
from jax.experimental.pallas import tpu_sc as plsc

D_MODEL = 1024
HEAD_DIM = 64
N_Q_HEADS = 8
N_KV_HEADS = 2
GQA_GROUP = N_Q_HEADS // N_KV_HEADS
HEAD_ORDER = (0, 2, 1, 3)
ATTN_WIDTH = N_Q_HEADS * HEAD_DIM
KV_WIDTH = N_KV_HEADS * HEAD_DIM
WINDOW = 128
ATTN_BLOCK = 128
ROPE_THETA = 500000.0
ROPE_DIM = HEAD_DIM // 4
GMLP_WIDTH = D_MODEL // 2
GMLP_GROUPS = 8
GMLP_GROUP_DIM = GMLP_WIDTH // GMLP_GROUPS
GMLP_CHUNK = 128
N_BRANCH = 2
IN_WIDTH = ATTN_WIDTH + 2 * KV_WIDTH + 2 * GMLP_WIDTH + N_BRANCH * D_MODEL
N_EXPERTS = 32
TOP_K = 4
D_FF = D_MODEL
SWIGLU_ALPHA = 1.702
SWIGLU_LIMIT = 7.0
PLE_DIM = 256
EPS = 1e-6

Q_OFF = 0
K_OFF = ATTN_WIDTH
V_OFF = K_OFF + KV_WIDTH
U_OFF = V_OFF + KV_WIDTH
VG_OFF = U_OFF + GMLP_WIDTH
GL_OFF = VG_OFF + GMLP_WIDTH

LANES = 128
PACK_WORDS = D_MODEL // 2
SUB = PACK_WORDS // LANES
MIX_TILE = 512
PROJ_COLS = 256
MOE_BLOCK = 512
MOE_COLS = 256
CMB_TILE = 1024
ROUTER_PAD = 128
SC_CHUNK = 128
DISPATCH_TOKENS = 64
GATHER_STREAMS = 4
VMEM_LIMIT = 56 * 1024 * 1024

BF16 = jnp.bfloat16
F32 = jnp.float32
U32 = jnp.uint32


def _rms(x, gain):
    return x * lax.rsqrt(jnp.mean(x * x, axis=-1, keepdims=True) + EPS) * gain


def _gelu(x):
    return 0.5 * x * (1.0 + lax.erf(x * np.float32(np.sqrt(0.5))))


def _sigmoid(x):
    return 0.5 * jnp.tanh(0.5 * x) + 0.5


def _head_rms(x, blockdiag, gain):
    ms = jnp.dot((x * x).astype(BF16), blockdiag, preferred_element_type=F32)
    return x * lax.rsqrt(ms + EPS) * gain


def _rope(x, cos, sin_lo, sin_hi):
    w = x.shape[-1]
    return x * cos + pltpu.roll(x, w - ROPE_DIM // 2, 1) * sin_lo + pltpu.roll(x, ROPE_DIM // 2, 1) * sin_hi


def _value_blocks(v):
    swapped = pltpu.roll(v, HEAD_DIM, 1)
    low = lax.broadcasted_iota(jnp.int32, v.shape, 1) < HEAD_DIM
    blocks = [jnp.where(low, v, 1.0), jnp.where(low, 1.0, swapped), jnp.where(low, swapped, 1.0), jnp.where(low, 1.0, v)]
    return jnp.concatenate(blocks, axis=1).astype(BF16)


def _tile_lanes(t, reps):
    return t if reps == 1 else jnp.concatenate([t] * reps, axis=-1)


def _store_packed(ref, x):
    rows = x.shape[0]
    hi = lax.bitcast_convert_type(x[:, :PACK_WORDS].astype(BF16).astype(F32), U32)
    lo = lax.bitcast_convert_type(x[:, PACK_WORDS:].astype(BF16).astype(F32), U32)
    words = hi | (lo >> 16)
    for j in range(SUB):
        ref[pl.ds(j, rows, stride=SUB), :] = words[:, j * LANES:(j + 1) * LANES]


def _load_packed(ref, rows):
    words = jnp.concatenate([ref[pl.ds(j, rows, stride=SUB), :] for j in range(SUB)], axis=1)
    left = lax.bitcast_convert_type(words & np.uint32(0xFFFF0000), F32)
    right = lax.bitcast_convert_type(words << 16, F32)
    return jnp.concatenate([left, right], axis=1)


def _mixer_kernel(x_ref, xn_ref, rp_ref, rpn_ref,
                  nmix_ref, win_ref, qg_ref, kg_ref, sink_ref, sinkv_ref, vgain_ref, bdq_ref,
                  wcat_ref, bias_ref, wbr_ref, wout_ref, nffn_ref, wr_ref, br_ref, tri_ref,
                  h1_ref, hn_ref, route_ref, gate_ref, cnt_out_ref,
                  z_ref, q_ref, k_ref, v_ref, attn_ref, gm_ref, cnt_ref, *, n_blocks_seq):
    ts = MIX_TILE
    i = pl.program_id(1)
    x = x_ref[0]

    @pl.when(i > 0)
    def _():
        k_ref[0:ATTN_BLOCK, :] = k_ref[ts:ts + ATTN_BLOCK, :]
        v_ref[0:ATTN_BLOCK, :] = v_ref[ts:ts + ATTN_BLOCK, :]

    @pl.when(i == 0)
    def _():
        k_ref[0:ATTN_BLOCK, :] = jnp.zeros((ATTN_BLOCK, k_ref.shape[1]), BF16)
        v_ref[0:ATTN_BLOCK, :] = jnp.zeros((ATTN_BLOCK, v_ref.shape[1]), BF16)

    xn = _rms(x, nmix_ref[...]).astype(BF16)

    def project(lo, hi):
        z_ref[:, lo:hi] = jnp.dot(xn, win_ref[:, lo:hi], preferred_element_type=F32)

    project(Q_OFF, U_OFF)
    n_items = (ts // ATTN_BLOCK) * N_KV_HEADS
    later = [(U_OFF + t * PROJ_COLS, U_OFF + (t + 1) * PROJ_COLS) for t in range((IN_WIDTH - U_OFF) // PROJ_COLS)]
    per_item = [len(later) // n_items + (1 if n < len(later) % n_items else 0) for n in range(n_items)]

    cos = rp_ref[:, 0:LANES]
    sin_lo = rp_ref[:, LANES:2 * LANES]
    sin_hi = rp_ref[:, 2 * LANES:3 * LANES]
    bdq = bdq_ref[...]
    bdk = bdq_ref[0:KV_WIDTH, 0:KV_WIDTH]

    q = _head_rms(z_ref[:, Q_OFF:Q_OFF + ATTN_WIDTH], bdq, qg_ref[...])
    reps = ATTN_WIDTH // LANES
    q = _rope(q, _tile_lanes(cos, reps), _tile_lanes(sin_lo, reps), _tile_lanes(sin_hi, reps))
    q_ref[...] = (q * (HEAD_DIM ** -0.5)).astype(BF16)

    k = _head_rms(z_ref[:, K_OFF:K_OFF + KV_WIDTH], bdk, kg_ref[...])
    k_ref[ATTN_BLOCK:ATTN_BLOCK + ts, :] = _rope(k, cos, sin_lo, sin_hi).astype(BF16)
    v_ref[ATTN_BLOCK:ATTN_BLOCK + ts, :] = _value_blocks(z_ref[:, V_OFF:V_OFF + KV_WIDTH])

    xhn = _rms(xn_ref[0], nmix_ref[...]).astype(BF16)
    zh = jnp.dot(xhn, win_ref[:, K_OFF:K_OFF + 2 * KV_WIDTH], preferred_element_type=F32)
    kh = _head_rms(zh[:, 0:KV_WIDTH], bdk, kg_ref[...])
    kh = _rope(kh, rpn_ref[:, 0:LANES], rpn_ref[:, LANES:2 * LANES], rpn_ref[:, 2 * LANES:3 * LANES])
    k_ref[ATTN_BLOCK + ts:2 * ATTN_BLOCK + ts, :] = kh.astype(BF16)
    v_ref[ATTN_BLOCK + ts:2 * ATTN_BLOCK + ts, :] = _value_blocks(zh[:, KV_WIDTH:2 * KV_WIDTH])

    rows = GQA_GROUP * ATTN_BLOCK
    keys = 3 * ATTN_BLOCK
    r = lax.broadcasted_iota(jnp.int32, (rows, keys), 0) % ATTN_BLOCK
    c = lax.broadcasted_iota(jnp.int32, (rows, keys), 1)
    band = (c >= r) & (c <= r + 2 * WINDOW)
    low = lax.broadcasted_iota(jnp.int32, (ATTN_BLOCK, 2 * HEAD_DIM), 1) < HEAD_DIM
    pair = 2 * ATTN_BLOCK
    items = [(qb, j) for qb in range(ts // ATTN_BLOCK) for j in range(N_KV_HEADS)]

    def scores(n):
        qb, j = items[n]
        r0 = qb * ATTN_BLOCK
        q4 = jnp.concatenate(
            [q_ref[r0:r0 + ATTN_BLOCK, (GQA_GROUP * j + g) * HEAD_DIM:(GQA_GROUP * j + g + 1) * HEAD_DIM]
             for g in HEAD_ORDER], axis=0)
        kw = k_ref[r0:r0 + keys, j * HEAD_DIM:(j + 1) * HEAD_DIM]
        return lax.dot_general(q4, kw, (((1,), (1,)), ((), ())), preferred_element_type=F32)

    s_next = scores(0)
    for n, (qb, j) in enumerate(items):
        s = s_next
        if n + 1 < len(items):
            s_next = scores(n + 1)
        for _ in range(per_item[n]):
            project(*later.pop(0))
        gb = i * (ts // ATTN_BLOCK) + qb
        lo = jnp.where(gb == 0, ATTN_BLOCK, 0)
        hi = jnp.where(gb == n_blocks_seq - 1, 2 * ATTN_BLOCK, keys)
        valid = band & (c >= lo) & (c < hi)
        r0 = qb * ATTN_BLOCK
        s = jnp.concatenate([jnp.where(valid, s, -jnp.inf), sink_ref[j]], axis=1)
        p = jnp.exp(s - jnp.max(s, axis=-1, keepdims=True)).astype(BF16)
        outs = []
        for par in range(2):
            vw = jnp.concatenate([v_ref[r0:r0 + keys, (2 * j + par) * LANES:(2 * j + par + 1) * LANES],
                                  sinkv_ref[par]], axis=0)
            o = jnp.dot(p[par * pair:(par + 1) * pair], vw, preferred_element_type=F32)
            outs.append(o / pltpu.roll(o, HEAD_DIM, 1))
        for a in range(GQA_GROUP // 2):
            both = jnp.where(low, outs[0][a * ATTN_BLOCK:(a + 1) * ATTN_BLOCK],
                             outs[1][a * ATTN_BLOCK:(a + 1) * ATTN_BLOCK])
            h0 = GQA_GROUP * j + 2 * a
            attn_ref[r0:r0 + ATTN_BLOCK, h0 * HEAD_DIM:(h0 + 2) * HEAD_DIM] = both.astype(BF16)

    vgn_all = _head_rms(_gelu(z_ref[:, VG_OFF:VG_OFF + GMLP_WIDTH]), bdq, vgain_ref[...])
    half = GMLP_WIDTH // 2
    gpm = half // GMLP_GROUP_DIM
    lane_grp = lax.broadcasted_iota(jnp.int32, (GMLP_CHUNK, half), 1) // GMLP_GROUP_DIM
    mixed = []
    for ch in range(ts // GMLP_CHUNK):
        c0 = ch * GMLP_CHUNK
        parts = []
        for nt in range(2):
            part = vgn_all[c0:c0 + GMLP_CHUNK, nt * half:(nt + 1) * half]
            vexp = jnp.concatenate(
                [jnp.where(lane_grp == gl, part, 0.0).astype(BF16) for gl in range(gpm)], axis=0)
            wpart = wcat_ref[:, nt * gpm * GMLP_CHUNK:(nt + 1) * gpm * GMLP_CHUNK]
            parts.append(jnp.dot(wpart, vexp, preferred_element_type=F32))
        mixed.append(jnp.concatenate(parts, axis=-1))
    for ch in range(ts // GMLP_CHUNK):
        c0 = ch * GMLP_CHUNK
        u = _gelu(z_ref[c0:c0 + GMLP_CHUNK, U_OFF:U_OFF + GMLP_WIDTH])
        gm_ref[c0:c0 + GMLP_CHUNK, :] = (u * (mixed[ch] + bias_ref[...])).astype(BF16)

    ya = jnp.dot(attn_ref[...], wbr_ref[0], preferred_element_type=F32)
    yg = jnp.dot(gm_ref[...], wbr_ref[1], preferred_element_type=F32)
    ga = _sigmoid(z_ref[:, GL_OFF:GL_OFF + D_MODEL])
    gg = _sigmoid(z_ref[:, GL_OFF + D_MODEL:GL_OFF + 2 * D_MODEL])
    merged = (ga * ya + gg * yg).astype(BF16)
    h1 = x + jnp.dot(merged, wout_ref[...], preferred_element_type=F32)
    h1_ref[0] = h1

    hn = _rms(h1, nffn_ref[...])
    _store_packed(hn_ref, hn)

    hn_hi = hn.astype(BF16)
    hn_lo = (hn - hn_hi.astype(F32)).astype(BF16)
    by_hi = jnp.dot(hn_hi, wr_ref[...], preferred_element_type=F32)
    by_lo = jnp.dot(hn_lo, wr_ref[:, 0:ROUTER_PAD], preferred_element_type=F32)
    logits = by_hi[:, 0:ROUTER_PAD] + by_hi[:, ROUTER_PAD:2 * ROUTER_PAD] + by_lo + br_ref[...]
    lt = jnp.transpose(logits)[0:N_EXPERTS, :]
    eid = lax.broadcasted_iota(jnp.int32, lt.shape, 0)
    vals, ids = [], []
    for _ in range(TOP_K):
        mx = jnp.max(lt, axis=0, keepdims=True)
        am = jnp.min(jnp.where(lt == mx, eid, N_EXPERTS), axis=0, keepdims=True)
        vals.append(mx)
        ids.append(am)
        lt = jnp.where(eid == am, -jnp.inf, lt)
    ex = [jnp.exp(v - vals[0]) for v in vals]
    tot = ex[0] + ex[1] + ex[2] + ex[3]
    gate_ref[...] = jnp.concatenate([e / tot for e in ex] + [jnp.zeros_like(tot)] * (8 - TOP_K), axis=0)

    @pl.when((pl.program_id(0) == 0) & (i == 0))
    def _():
        cnt_ref[...] = jnp.zeros_like(cnt_ref)

    sel = [eid == a for a in ids]
    member = (sel[0] | sel[1] | sel[2] | sel[3]).astype(F32)
    before = jnp.dot(member.astype(BF16), tri_ref[...], preferred_element_type=F32) + cnt_ref[:, 0:1]
    ranks = [jnp.sum(jnp.where(s_, before, 0.0), axis=0, keepdims=True).astype(jnp.int32) for s_ in sel]
    route_ref[...] = jnp.concatenate(ids + ranks, axis=0)
    cnt_ref[...] = cnt_ref[...] + jnp.sum(member, axis=1, keepdims=True)
    cnt_out_ref[...] = cnt_ref[...]


def _mixer_call(x, rope_tab, prm):
    b, s, d = x.shape
    ts = MIX_TILE
    nt = s // ts
    nb = s // ATTN_BLOCK
    per = ts // ATTN_BLOCK
    const2 = lambda bi, i: (0, 0)
    const3 = lambda bi, i: (0, 0, 0)

    def wspec(arr):
        return pl.BlockSpec(arr.shape, const2 if arr.ndim == 2 else const3)

    in_specs = [
        pl.BlockSpec((1, ts, d), lambda bi, i: (bi, i, 0)),
        pl.BlockSpec((1, ATTN_BLOCK, d), lambda bi, i: (bi, jnp.minimum((i + 1) * per, nb - 1), 0)),
        pl.BlockSpec((ts, 3 * LANES), lambda bi, i: (i, 0)),
        pl.BlockSpec((ATTN_BLOCK, 3 * LANES), lambda bi, i: (jnp.minimum((i + 1) * per, nb - 1), 0)),
    ] + [wspec(a) for a in prm]
    out_shape = [
        jax.ShapeDtypeStruct((b, s, d), F32),
        jax.ShapeDtypeStruct((b * s * SUB, LANES), U32),
        jax.ShapeDtypeStruct((2 * TOP_K, b * s), jnp.int32),
        jax.ShapeDtypeStruct((2 * TOP_K, b * s), F32),
        jax.ShapeDtypeStruct((N_EXPERTS, LANES), F32),
    ]
    out_specs = [
        pl.BlockSpec((1, ts, d), lambda bi, i: (bi, i, 0)),
        pl.BlockSpec((ts * SUB, LANES), lambda bi, i: (bi * nt + i, 0)),
        pl.BlockSpec((2 * TOP_K, ts), lambda bi, i: (0, bi * nt + i)),
        pl.BlockSpec((2 * TOP_K, ts), lambda bi, i: (0, bi * nt + i)),
        pl.BlockSpec((N_EXPERTS, LANES), const2),
    ]
    scratch = [
        pltpu.VMEM((ts, IN_WIDTH), F32),
        pltpu.VMEM((ts, ATTN_WIDTH), BF16),
        pltpu.VMEM((ts + 2 * ATTN_BLOCK, KV_WIDTH), BF16),
        pltpu.VMEM((ts + 2 * ATTN_BLOCK, 2 * N_KV_HEADS * LANES), BF16),
        pltpu.VMEM((ts, ATTN_WIDTH), BF16),
        pltpu.VMEM((ts, GMLP_WIDTH), BF16),
        pltpu.VMEM((N_EXPERTS, LANES), F32),
    ]
    return pl.pallas_call(
        functools.partial(_mixer_kernel, n_blocks_seq=nb),
        grid=(b, nt),
        in_specs=in_specs,
        out_specs=out_specs,
        out_shape=out_shape,
        scratch_shapes=scratch,
        compiler_params=pltpu.CompilerParams(
            dimension_semantics=("arbitrary", "arbitrary"), vmem_limit_bytes=VMEM_LIMIT),
        name="mixer",
    )(x, x, rope_tab, rope_tab, *prm)


def _sc_workers():
    info = plsc.get_sparse_core_info()
    return info.num_cores, info.num_cores * info.num_subcores


def _sc_dispatch(rows2d, slot_rows, n_out_rows):
    n_cores, n_workers = _sc_workers()
    chunk_rows = DISPATCH_TOKENS * SUB
    halves = chunk_rows // SC_CHUNK
    n_chunks = rows2d.shape[0] // chunk_rows
    per_w = n_chunks // n_workers
    idx_rows_per_choice = rows2d.shape[0] // SC_CHUNK
    mesh = plsc.VectorSubcoreMesh(core_axis_name="c", subcore_axis_name="s")

    @functools.partial(
        pl.kernel, mesh=mesh,
        out_type=jax.ShapeDtypeStruct((n_out_rows, LANES), rows2d.dtype),
        scratch_types=[pltpu.VMEM((TOP_K * halves, SC_CHUNK), jnp.int32),
                       pltpu.VMEM((chunk_rows, LANES), rows2d.dtype),
                       pltpu.SemaphoreType.DMA, pltpu.SemaphoreType.DMA],
    )
    def k(src_hbm, idx_hbm, out_hbm, idx_v, rows_v, sem_in, sem_out):
        wid = lax.axis_index("s") * n_cores + lax.axis_index("c")

        @pl.loop(0, per_w)
        def _(j):
            c = wid * per_w + j
            loads = [pltpu.async_copy(src_hbm.at[pl.ds(c * chunk_rows, chunk_rows)], rows_v, sem_in)]
            loads += [
                pltpu.async_copy(idx_hbm.at[pl.ds(kk * idx_rows_per_choice + c * halves, halves)],
                                 idx_v.at[pl.ds(kk * halves, halves)], sem_in)
                for kk in range(TOP_K)]
            for cp in loads:
                cp.wait()
            copies = [
                pltpu.async_copy(rows_v.at[pl.ds((q % halves) * SC_CHUNK, SC_CHUNK)],
                                 out_hbm.at[idx_v.at[q]], sem_out)
                for q in range(TOP_K * halves)]
            for cp in copies:
                cp.wait()

    return k(rows2d, slot_rows)


def _sc_gather(table2d, idx2d):
    n_cores, n_workers = _sc_workers()
    n = idx2d.shape[0] * SC_CHUNK
    per_w = n // n_workers
    step_rows = GATHER_STREAMS * SC_CHUNK
    idx_tile = 8
    mesh = plsc.VectorSubcoreMesh(core_axis_name="c", subcore_axis_name="s")

    @functools.partial(
        pl.kernel, mesh=mesh,
        out_type=jax.ShapeDtypeStruct((n, LANES), table2d.dtype),
        scratch_types=[pltpu.VMEM((idx_tile, SC_CHUNK), jnp.int32),
                       pltpu.VMEM((step_rows, LANES), table2d.dtype),
                       pltpu.SemaphoreType.DMA],
    )
    def k(table_hbm, idx_hbm, out_hbm, idx_v, rows_v, sem):
        wid = lax.axis_index("s") * n_cores + lax.axis_index("c")

        @pl.loop(0, per_w // (idx_tile * SC_CHUNK))
        def _(j):
            row0 = pl.multiple_of(wid * (per_w // SC_CHUNK) + j * idx_tile, idx_tile)
            pltpu.sync_copy(idx_hbm.at[pl.ds(row0, idx_tile)], idx_v)
            for part in range(idx_tile // GATHER_STREAMS):
                copies = [pltpu.async_copy(table_hbm.at[idx_v.at[part * GATHER_STREAMS + q]],
                                           rows_v.at[pl.ds(q * SC_CHUNK, SC_CHUNK)], sem)
                          for q in range(GATHER_STREAMS)]
                for cp in copies:
                    cp.wait()
                pltpu.sync_copy(rows_v, out_hbm.at[pl.ds((row0 + part * GATHER_STREAMS) * SC_CHUNK, step_rows)])

    return k(table2d, idx2d)


def _expert_weight_copies(wgu_hbm, wd_hbm, wgu_buf, wd_buf, sem, expert, slot):
    return (pltpu.make_async_copy(wgu_hbm.at[expert], wgu_buf.at[slot], sem.at[0, slot]),
            pltpu.make_async_copy(wd_hbm.at[expert], wd_buf.at[slot], sem.at[1, slot]))


def _moe_kernel(be_ref, nv_ref, nxt_ref, par_ref, x_ref, wgu_hbm, bgu_ref, wd_hbm, bd_ref, y_ref,
                wgu_buf, wd_buf, sem, wgu_s, wd_s, slab):
    i = pl.program_id(0)
    nv = nv_ref[i]
    slot = par_ref[i]

    cw = MOE_COLS
    n_chunks = D_FF // cw
    copies = functools.partial(_expert_weight_copies, wgu_hbm, wd_hbm, wgu_buf, wd_buf, sem)

    @pl.when(i == 0)
    def _():
        for cp in copies(be_ref[0], slot):
            cp.start()

    @pl.when((i == 0) | (be_ref[i] != be_ref[jnp.maximum(i - 1, 0)]))
    def _():
        for cp in copies(be_ref[i], slot):
            cp.wait()

        @pl.when(nxt_ref[i] >= 0)
        def _():
            for cp in copies(nxt_ref[i], 1 - slot):
                cp.start(priority=1)

        for c in range(n_chunks):
            wgu_s[:, 2 * c * cw:(2 * c + 1) * cw] = wgu_buf[slot, :, c * cw:(c + 1) * cw].astype(BF16)
            wgu_s[:, (2 * c + 1) * cw:(2 * c + 2) * cw] = (
                wgu_buf[slot, :, D_FF + c * cw:D_FF + (c + 1) * cw].astype(BF16))
        half = D_FF // 2
        for c in range(D_MODEL // LANES):
            slab[c, pl.ds(0, half, stride=2), :] = wd_buf[slot, 0:half, c * LANES:(c + 1) * LANES]
            slab[c, pl.ds(1, half, stride=2), :] = wd_buf[slot, half:D_FF, c * LANES:(c + 1) * LANES]
        for c in range(D_MODEL // LANES):
            wd_s[:, c * LANES:(c + 1) * LANES] = slab[c].astype(BF16)

    expert = be_ref[i]

    def run_expert(rows):
        x = _load_packed(x_ref, rows)
        live = lax.broadcasted_iota(jnp.int32, x.shape, 0) < nv
        xe = jnp.where(live, x, 0.0).astype(BF16)
        even = (lax.broadcasted_iota(jnp.int32, (rows, cw), 1) % 2) == 0
        y = bd_ref[expert]

        def gate_up(c):
            return jnp.dot(xe, wgu_s[:, 2 * c * cw:(2 * c + 2) * cw], preferred_element_type=F32)

        h_next = gate_up(0)
        for c in range(n_chunks):
            h = h_next
            if c + 1 < n_chunks:
                h_next = gate_up(c + 1)
            h_a = h[:, 0:cw] + bgu_ref[expert, :, c * cw:(c + 1) * cw]
            h_b = h[:, cw:2 * cw] + bgu_ref[expert, :, D_FF + c * cw:D_FF + (c + 1) * cw]
            gate = jnp.where(even, h_a, pltpu.roll(h_b, 1, 1))
            up = jnp.where(even, pltpu.roll(h_a, cw - 1, 1), h_b)
            gate = jnp.minimum(gate, SWIGLU_LIMIT)
            up = jnp.clip(up, -SWIGLU_LIMIT, SWIGLU_LIMIT)
            act = (up + 1.0) * (gate * jax.nn.sigmoid(SWIGLU_ALPHA * gate))
            y = y + jnp.dot(act.astype(BF16), wd_s[c * cw:(c + 1) * cw, :], preferred_element_type=F32)
        _store_packed(y_ref, y)

    @pl.when(nv > MOE_BLOCK // 2)
    def _():
        run_expert(MOE_BLOCK)

    @pl.when((nv > 0) & (nv <= MOE_BLOCK // 2))
    def _():
        run_expert(MOE_BLOCK // 2)


def _moe_call(block_e, n_valid, next_e, parity, xb, wgu, bgu, wd, bd):
    n_blocks = block_e.shape[0]
    d = D_MODEL
    grid_spec = pltpu.PrefetchScalarGridSpec(
        num_scalar_prefetch=4,
        grid=(n_blocks,),
        in_specs=[
            pl.BlockSpec((MOE_BLOCK * SUB, LANES), lambda i, be, nv, nx, pr: (i, 0)),
            pl.BlockSpec(memory_space=pl.ANY),
            pl.BlockSpec((N_EXPERTS, 1, 2 * D_FF), lambda i, be, nv, nx, pr: (0, 0, 0)),
            pl.BlockSpec(memory_space=pl.ANY),
            pl.BlockSpec((N_EXPERTS, 1, d), lambda i, be, nv, nx, pr: (0, 0, 0)),
        ],
        out_specs=pl.BlockSpec((MOE_BLOCK * SUB, LANES), lambda i, be, nv, nx, pr: (i, 0)),
        scratch_shapes=[pltpu.VMEM((2, d, 2 * D_FF), F32), pltpu.VMEM((2, D_FF, d), F32),
                        pltpu.SemaphoreType.DMA((2, 2)),
                        pltpu.VMEM((d, 2 * D_FF), BF16), pltpu.VMEM((D_FF, d), BF16),
                        pltpu.VMEM((d // LANES, D_FF, LANES), F32)],
    )
    return pl.pallas_call(
        _moe_kernel,
        grid_spec=grid_spec,
        out_shape=jax.ShapeDtypeStruct((n_blocks * MOE_BLOCK * SUB, LANES), U32),
        compiler_params=pltpu.CompilerParams(
            dimension_semantics=("arbitrary",), vmem_limit_bytes=VMEM_LIMIT),
        name="moe",
    )(block_e, n_valid, next_e, parity, xb, wgu, bgu, wd, bd)


def _combine_kernel(g0_ref, g1_ref, g2_ref, g3_ref, h1_ref, gate_ref, p_ref, nple_ref, wpg_ref, wpp_ref, o_ref):
    g = jnp.transpose(jnp.concatenate([gate_ref[...], jnp.zeros((LANES - 2 * TOP_K, CMB_TILE), F32)], axis=0))
    moe = jnp.zeros((CMB_TILE, D_MODEL), F32)
    for kk, g_ref in enumerate((g0_ref, g1_ref, g2_ref, g3_ref)):
        moe = moe + _load_packed(g_ref, CMB_TILE) * g[:, kk:kk + 1]
    h2 = h1_ref[...] + moe
    hp = _rms(h2, nple_ref[...]).astype(BF16)
    gate = jax.nn.sigmoid(jnp.dot(hp, wpg_ref[...], preferred_element_type=F32))
    proj = jnp.dot(p_ref[...].astype(BF16), wpp_ref[...], preferred_element_type=F32)
    o_ref[...] = h2 + gate * proj


def _combine_call(gathered, h1_flat, gates, p_flat, nple, wpg, wpp):
    n_tok, d = h1_flat.shape
    tc = CMB_TILE
    n_tiles = n_tok // tc
    row = lambda i: (i, 0)
    const = lambda i: (0, 0)
    g_specs = [pl.BlockSpec((tc * SUB, LANES), functools.partial(lambda i, kk: (kk * n_tiles + i, 0), kk=kk))
               for kk in range(TOP_K)]
    return pl.pallas_call(
        _combine_kernel,
        grid=(n_tiles,),
        in_specs=g_specs + [
            pl.BlockSpec((tc, d), row),
            pl.BlockSpec((2 * TOP_K, tc), lambda i: (0, i)),
            pl.BlockSpec((tc, PLE_DIM), row),
            pl.BlockSpec((1, d), const),
            pl.BlockSpec((d, d), const),
            pl.BlockSpec((PLE_DIM, d), const),
        ],
        out_specs=pl.BlockSpec((tc, d), row),
        out_shape=jax.ShapeDtypeStruct((n_tok, d), F32),
        compiler_params=pltpu.CompilerParams(
            dimension_semantics=("arbitrary",), vmem_limit_bytes=VMEM_LIMIT),
        name="combine",
    )(gathered, gathered, gathered, gathered, h1_flat, gates, p_flat, nple, wpg, wpp)


def _rope_table(s):
    half = ROPE_DIM // 2
    inv_freq = jnp.power(ROPE_THETA, -jnp.arange(half, dtype=F32) * (2.0 / ROPE_DIM))
    ang = jnp.arange(s, dtype=F32)[:, None] * inv_freq[None, :]
    base = jnp.concatenate([jnp.cos(ang), jnp.sin(ang)], axis=-1)
    place = np.zeros((2 * half, 3 * LANES), np.float32)
    ones = np.zeros((1, 3 * LANES), np.float32)
    for lane in range(3 * LANES):
        kind, d = lane // LANES, lane % HEAD_DIM
        if kind == 0 and d < ROPE_DIM:
            place[d % half, lane] = 1.0
        elif kind == 0:
            ones[0, lane] = 1.0
        elif kind == 1 and d < half:
            place[half + d, lane] = -1.0
        elif kind == 2 and half <= d < ROPE_DIM:
            place[d, lane] = 1.0
    return jnp.dot(base, place, precision=lax.Precision.HIGHEST) + ones


def _routing(route, counts, n_tok):
    ids = route[0:TOP_K]
    ranks = route[TOP_K:2 * TOP_K]
    counts = counts.astype(jnp.int32)
    padded = (counts + MOE_BLOCK - 1) // MOE_BLOCK * MOE_BLOCK
    pends = jnp.cumsum(padded)
    pstarts = pends - padded
    pos = ranks
    for e in range(N_EXPERTS):
        pos = pos + jnp.where(ids == e, pstarts[e], 0)
    n_slots = -(-(n_tok * TOP_K) // MOE_BLOCK) * MOE_BLOCK + N_EXPERTS * MOE_BLOCK
    n_blocks = n_slots // MOE_BLOCK
    first = jnp.arange(n_blocks, dtype=jnp.int32) * MOE_BLOCK
    block_e = jnp.clip(jnp.sum((first[:, None] >= pends[None, :]).astype(jnp.int32), axis=1), 0, N_EXPERTS - 1)
    own = block_e[:, None] == jnp.arange(N_EXPERTS, dtype=jnp.int32)[None, :]
    left = jnp.sum(jnp.where(own, (counts + pstarts)[None, :], 0), axis=1) - first
    n_valid = jnp.clip(left, 0, MOE_BLOCK)
    later = jnp.where(block_e[None, :] > block_e[:, None], block_e[None, :], N_EXPERTS)
    next_e = jnp.min(later, axis=1)
    next_e = jnp.where(next_e == N_EXPERTS, -1, next_e)
    present = jnp.any(own, axis=0)
    runs_before = jnp.sum((present[None, :] & (jnp.arange(N_EXPERTS)[None, :] < block_e[:, None])).astype(jnp.int32),
                          axis=1)
    parity = runs_before % 2
    return pos, (block_e, n_valid.astype(jnp.int32), next_e.astype(jnp.int32), parity.astype(jnp.int32)), n_slots


def _expand_rows(pos):
    spread = (jnp.arange(SUB * LANES, dtype=jnp.int32)[None, :] // SUB
              == jnp.arange(LANES, dtype=jnp.int32)[:, None]).astype(F32)
    wide = jnp.dot(pos.reshape(-1, LANES).astype(F32), spread, precision=lax.Precision.HIGHEST)
    wide = wide.astype(jnp.int32) * SUB + (jnp.arange(SUB * LANES, dtype=jnp.int32) % SUB)[None, :]
    return wide.reshape(-1, LANES)


def _group_forward(x, p_l, rope_tab, mix_prm, moe_prm, ple_prm):
    b, s, d = x.shape
    n_tok = b * s
    h1, hn_rows, route, gates, counts = _mixer_call(x, rope_tab, mix_prm)
    pos, block_info, n_slots = _routing(route, counts[:, 0], n_tok)
    slot_rows = _expand_rows(pos)
    xb = _sc_dispatch(hn_rows, slot_rows, n_slots * SUB)
    y = _moe_call(*block_info, xb, *moe_prm)
    gathered = _sc_gather(y, slot_rows)
    out = _combine_call(gathered, h1.reshape(n_tok, d), gates,
                        p_l.reshape(n_tok, PLE_DIM), *ple_prm)
    return out.reshape(b, s, d)


def kernel(x_prompt, x_sample, p_prompt, p_sample, norm_mix, w_in, q_gain, k_gain, attn_sink, gmlp_v_gain, gmlp_w_s, gmlp_b_s, w_branch, w_out, norm_ffn, w_router, b_router, w_gate_up, b_gate_up, w_down, b_down, norm_ple, w_ple_gate, w_ple_proj):
    depth = norm_mix.shape[0]
    hp, hs = x_prompt, x_sample
    for l in range(depth):
        row = lambda a: a.reshape(1, -1)
        blockdiag = jnp.kron(jnp.eye(N_Q_HEADS, dtype=F32),
                             jnp.full((HEAD_DIM, HEAD_DIM), 1.0 / HEAD_DIM, F32)).astype(BF16)
        sink_heads = attn_sink[l].reshape(N_KV_HEADS, GQA_GROUP)[:, jnp.array(HEAD_ORDER)]
        sink_col = jnp.repeat(sink_heads, ATTN_BLOCK, axis=1)[..., None]
        sink_rows = jnp.where(jnp.arange(LANES) == 0, sink_col, -jnp.inf)
        first_row = (jnp.arange(LANES) == 0)[:, None]
        upper = (jnp.arange(LANES) >= HEAD_DIM)[None, :]
        sink_values = jnp.stack([first_row & upper, first_row & ~upper]).astype(BF16)
        wcat = jnp.transpose(gmlp_w_s[l], (1, 0, 2)).reshape(GMLP_CHUNK, GMLP_GROUPS * GMLP_CHUNK).astype(BF16)
        bias_full = jnp.repeat(gmlp_b_s[l].T, GMLP_GROUP_DIM, axis=1)
        wr = jnp.pad(w_router[l], ((0, 0), (0, ROUTER_PAD - N_EXPERTS)))
        wr_hi = wr.astype(BF16)
        wr_lo = (wr - wr_hi.astype(F32)).astype(BF16)
        br = jnp.pad(b_router[l], (0, ROUTER_PAD - N_EXPERTS)).reshape(1, ROUTER_PAD)
        tri = (jnp.arange(MIX_TILE)[:, None] < jnp.arange(MIX_TILE)[None, :]).astype(BF16)
        mix_prm = (
            row(norm_mix[l]), w_in[l].astype(BF16),
            row(jnp.tile(q_gain[l], N_Q_HEADS)), row(jnp.tile(k_gain[l], N_KV_HEADS)),
            sink_rows, sink_values, row(gmlp_v_gain[l]), blockdiag, wcat, bias_full,
            w_branch[l].astype(BF16), w_out[l].astype(BF16), row(norm_ffn[l]),
            jnp.concatenate([wr_hi, wr_lo], axis=1), br, tri,
        )
        moe_prm = (
            w_gate_up[l], b_gate_up[l].reshape(N_EXPERTS, 1, 2 * D_FF),
            w_down[l], b_down[l].reshape(N_EXPERTS, 1, D_MODEL),
        )
        ple_prm = (row(norm_ple[l]), w_ple_gate[l].astype(BF16), w_ple_proj[l].astype(BF16))
        rope_tab = _rope_table(max(hp.shape[1], hs.shape[1]))
        hp = _group_forward(hp, p_prompt[l], rope_tab, mix_prm, moe_prm, ple_prm)
        hs = _group_forward(hs, p_sample[l], rope_tab, mix_prm, moe_prm, ple_prm)
    return (hp.astype(x_prompt.dtype), hs.astype(x_sample.dtype))
```

```python
import functools

import jax
import jax.numpy as jnp
import numpy as np
from jax import lax
from jax.experimental import pallas as pl
from jax.experimental.pallas import tpu as pltpu
from jax.experimental.pallas import tpu_sc as plsc

D_MODEL = 1024
HEAD_DIM = 64
N_Q_HEADS = 8
N_KV_HEADS = 2
GQA_GROUP = N_Q_HEADS // N_KV_HEADS
HEAD_ORDER = (0, 2, 1, 3)
ATTN_WIDTH = N_Q_HEADS * HEAD_DIM
KV_WIDTH = N_KV_HEADS * HEAD_DIM
WINDOW = 128
ATTN_BLOCK = 128
ROPE_THETA = 500000.0
ROPE_DIM = HEAD_DIM // 4
GMLP_WIDTH = D_MODEL // 2
GMLP_GROUPS = 8
GMLP_GROUP_DIM = GMLP_WIDTH // GMLP_GROUPS
GMLP_CHUNK = 128
N_BRANCH = 2
IN_WIDTH = ATTN_WIDTH + 2 * KV_WIDTH + 2 * GMLP_WIDTH + N_BRANCH * D_MODEL
N_EXPERTS = 32
TOP_K = 4
D_FF = D_MODEL
SWIGLU_ALPHA = 1.702
SWIGLU_LIMIT = 7.0
PLE_DIM = 256
EPS = 1e-6

Q_OFF = 0
K_OFF = ATTN_WIDTH
V_OFF = K_OFF + KV_WIDTH
U_OFF = V_OFF + KV_WIDTH
VG_OFF = U_OFF + GMLP_WIDTH
GL_OFF = VG_OFF + GMLP_WIDTH

LANES = 128
PACK_WORDS = D_MODEL // 2
SUB = PACK_WORDS // LANES
MIX_TILE = 512
PROJ_COLS = 256
MOE_BLOCK = 512
MOE_COLS = 256
CMB_TILE = 1024
ROUTER_PAD = 128
SC_CHUNK = 128
DISPATCH_TOKENS = 64
GATHER_STREAMS = 4
VMEM_LIMIT = 56 * 1024 * 1024

BF16 = jnp.bfloat16
F32 = jnp.float32
U32 = jnp.uint32


def _rms(x, gain):
    return x * lax.rsqrt(jnp.mean(x * x, axis=-1, keepdims=True) + EPS) * gain


def _gelu(x):
    return 0.5 * x * (1.0 + lax.erf(x * np.float32(np.sqrt(0.5))))


def _sigmoid(x):
    return 0.5 * jnp.tanh(0.5 * x) + 0.5


def _head_rms(x, blockdiag, gain):
    ms = jnp.dot((x * x).astype(BF16), blockdiag, preferred_element_type=F32)
    return x * lax.rsqrt(ms + EPS) * gain


def _rope(x, cos, sin_lo, sin_hi):
    w = x.shape[-1]
    return x * cos + pltpu.roll(x, w - ROPE_DIM // 2, 1) * sin_lo + pltpu.roll(x, ROPE_DIM // 2, 1) * sin_hi


def _value_blocks(v):
    swapped = pltpu.roll(v, HEAD_DIM, 1)
    low = lax.broadcasted_iota(jnp.int32, v.shape, 1) < HEAD_DIM
    blocks = [jnp.where(low, v, 1.0), jnp.where(low, 1.0, swapped), jnp.where(low, swapped, 1.0), jnp.where(low, 1.0, v)]
    return jnp.concatenate(blocks, axis=1).astype(BF16)


def _tile_lanes(t, reps):
    return t if reps == 1 else jnp.concatenate([t] * reps, axis=-1)


def _store_packed(ref, x):
    rows = x.shape[0]
    hi = lax.bitcast_convert_type(x[:, :PACK_WORDS].astype(BF16).astype(F32), U32)
    lo = lax.bitcast_convert_type(x[:, PACK_WORDS:].astype(BF16).astype(F32), U32)
    words = hi | (lo >> 16)
    for j in range(SUB):
        ref[pl.ds(j, rows, stride=SUB), :] = words[:, j * LANES:(j + 1) * LANES]


def _load_packed(ref, rows):
    words = jnp.concatenate([ref[pl.ds(j, rows, stride=SUB), :] for j in range(SUB)], axis=1)
    left = lax.bitcast_convert_type(words & np.uint32(0xFFFF0000), F32)
    right = lax.bitcast_convert_type(words << 16, F32)
    return jnp.concatenate([left, right], axis=1)


def _mixer_kernel(x_ref, xn_ref, rp_ref, rpn_ref,
                  nmix_ref, win_ref, qg_ref, kg_ref, sink_ref, sinkv_ref, vgain_ref, bdq_ref,
                  wcat_ref, bias_ref, wbr_ref, wout_ref, nffn_ref, wr_ref, br_ref, tri_ref,
                  h1_ref, hn_ref, route_ref, gate_ref, cnt_out_ref,
                  z_ref, q_ref, k_ref, v_ref, attn_ref, gm_ref, cnt_ref, *, n_blocks_seq):
    ts = MIX_TILE
    i = pl.program_id(1)
    x = x_ref[0]

    @pl.when(i > 0)
    def _():
        k_ref[0:ATTN_BLOCK, :] = k_ref[ts:ts + ATTN_BLOCK, :]
        v_ref[0:ATTN_BLOCK, :] = v_ref[ts:ts + ATTN_BLOCK, :]

    @pl.when(i == 0)
    def _():
        k_ref[0:ATTN_BLOCK, :] = jnp.zeros((ATTN_BLOCK, k_ref.shape[1]), BF16)
        v_ref[0:ATTN_BLOCK, :] = jnp.zeros((ATTN_BLOCK, v_ref.shape[1]), BF16)

    xn = _rms(x, nmix_ref[...]).astype(BF16)

    def project(lo, hi):
        z_ref[:, lo:hi] = jnp.dot(xn, win_ref[:, lo:hi], preferred_element_type=F32)

    project(Q_OFF, U_OFF)
    n_items = (ts // ATTN_BLOCK) * N_KV_HEADS
    later = [(U_OFF + t * PROJ_COLS, U_OFF + (t + 1) * PROJ_COLS) for t in range((IN_WIDTH - U_OFF) // PROJ_COLS)]
    per_item = [len(later) // n_items + (1 if n < len(later) % n_items else 0) for n in range(n_items)]

    cos = rp_ref[:, 0:LANES]
    sin_lo = rp_ref[:, LANES:2 * LANES]
    sin_hi = rp_ref[:, 2 * LANES:3 * LANES]
    bdq = bdq_ref[...]
    bdk = bdq_ref[0:KV_WIDTH, 0:KV_WIDTH]

    q = _head_rms(z_ref[:, Q_OFF:Q_OFF + ATTN_WIDTH], bdq, qg_ref[...])
    reps = ATTN_WIDTH // LANES
    q = _rope(q, _tile_lanes(cos, reps), _tile_lanes(sin_lo, reps), _tile_lanes(sin_hi, reps))
    q_ref[...] = (q * (HEAD_DIM ** -0.5)).astype(BF16)

    k = _head_rms(z_ref[:, K_OFF:K_OFF + KV_WIDTH], bdk, kg_ref[...])
    k_ref[ATTN_BLOCK:ATTN_BLOCK + ts, :] = _rope(k, cos, sin_lo, sin_hi).astype(BF16)
    v_ref[ATTN_BLOCK:ATTN_BLOCK + ts, :] = _value_blocks(z_ref[:, V_OFF:V_OFF + KV_WIDTH])

    xhn = _rms(xn_ref[0], nmix_ref[...]).astype(BF16)
    zh = jnp.dot(xhn, win_ref[:, K_OFF:K_OFF + 2 * KV_WIDTH], preferred_element_type=F32)
    kh = _head_rms(zh[:, 0:KV_WIDTH], bdk, kg_ref[...])
    kh = _rope(kh, rpn_ref[:, 0:LANES], rpn_ref[:, LANES:2 * LANES], rpn_ref[:, 2 * LANES:3 * LANES])
    k_ref[ATTN_BLOCK + ts:2 * ATTN_BLOCK + ts, :] = kh.astype(BF16)
    v_ref[ATTN_BLOCK + ts:2 * ATTN_BLOCK + ts, :] = _value_blocks(zh[:, KV_WIDTH:2 * KV_WIDTH])

    rows = GQA_GROUP * ATTN_BLOCK
    keys = 3 * ATTN_BLOCK
    r = lax.broadcasted_iota(jnp.int32, (rows, keys), 0) % ATTN_BLOCK
    c = lax.broadcasted_iota(jnp.int32, (rows, keys), 1)
    band = (c >= r) & (c <= r + 2 * WINDOW)
    low = lax.broadcasted_iota(jnp.int32, (ATTN_BLOCK, 2 * HEAD_DIM), 1) < HEAD_DIM
    pair = 2 * ATTN_BLOCK
    items = [(qb, j) for qb in range(ts // ATTN_BLOCK) for j in range(N_KV_HEADS)]

    def scores(n):
        qb, j = items[n]
        r0 = qb * ATTN_BLOCK
        q4 = jnp.concatenate(
            [q_ref[r0:r0 + ATTN_BLOCK, (GQA_GROUP * j + g) * HEAD_DIM:(GQA_GROUP * j + g + 1) * HEAD_DIM]
             for g in HEAD_ORDER], axis=0)
        kw = k_ref[r0:r0 + keys, j * HEAD_DIM:(j + 1) * HEAD_DIM]
        return lax.dot_general(q4, kw, (((1,), (1,)), ((), ())), preferred_element_type=F32)

    s_next = scores(0)
    for n, (qb, j) in enumerate(items):
        s = s_next
        if n + 1 < len(items):
            s_next = scores(n + 1)
        for _ in range(per_item[n]):
            project(*later.pop(0))
        gb = i * (ts // ATTN_BLOCK) + qb
        lo = jnp.where(gb == 0, ATTN_BLOCK, 0)
        hi = jnp.where(gb == n_blocks_seq - 1, 2 * ATTN_BLOCK, keys)
        valid = band & (c >= lo) & (c < hi)
        r0 = qb * ATTN_BLOCK
        s = jnp.concatenate([jnp.where(valid, s, -jnp.inf), sink_ref[j]], axis=1)
        p = jnp.exp(s - jnp.max(s, axis=-1, keepdims=True)).astype(BF16)
        outs = []
        for par in range(2):
            vw = jnp.concatenate([v_ref[r0:r0 + keys, (2 * j + par) * LANES:(2 * j + par + 1) * LANES],
                                  sinkv_ref[par]], axis=0)
            o = jnp.dot(p[par * pair:(par + 1) * pair], vw, preferred_element_type=F32)
            outs.append(o / pltpu.roll(o, HEAD_DIM, 1))
        for a in range(GQA_GROUP // 2):
            both = jnp.where(low, outs[0][a * ATTN_BLOCK:(a + 1) * ATTN_BLOCK],
                             outs[1][a * ATTN_BLOCK:(a + 1) * ATTN_BLOCK])
            h0 = GQA_GROUP * j + 2 * a
            attn_ref[r0:r0 + ATTN_BLOCK, h0 * HEAD_DIM:(h0 + 2) * HEAD_DIM] = both.astype(BF16)

    vgn_all = _head_rms(_gelu(z_ref[:, VG_OFF:VG_OFF + GMLP_WIDTH]), bdq, vgain_ref[...])
    half = GMLP_WIDTH // 2
    gpm = half // GMLP_GROUP_DIM
    lane_grp = lax.broadcasted_iota(jnp.int32, (GMLP_CHUNK, half), 1) // GMLP_GROUP_DIM
    mixed = []
    for ch in range(ts // GMLP_CHUNK):
        c0 = ch * GMLP_CHUNK
        parts = []
        for nt in range(2):
            part = vgn_all[c0:c0 + GMLP_CHUNK, nt * half:(nt + 1) * half]
            vexp = jnp.concatenate(
                [jnp.where(lane_grp == gl, part, 0.0).astype(BF16) for gl in range(gpm)], axis=0)
            wpart = wcat_ref[:, nt * gpm * GMLP_CHUNK:(nt + 1) * gpm * GMLP_CHUNK]
            parts.append(jnp.dot(wpart, vexp, preferred_element_type=F32))
        mixed.append(jnp.concatenate(parts, axis=-1))
    for ch in range(ts // GMLP_CHUNK):
        c0 = ch * GMLP_CHUNK
        u = _gelu(z_ref[c0:c0 + GMLP_CHUNK, U_OFF:U_OFF + GMLP_WIDTH])
        gm_ref[c0:c0 + GMLP_CHUNK, :] = (u * (mixed[ch] + bias_ref[...])).astype(BF16)

    ya = jnp.dot(attn_ref[...], wbr_ref[0], preferred_element_type=F32)
    yg = jnp.dot(gm_ref[...], wbr_ref[1], preferred_element_type=F32)
    ga = _sigmoid(z_ref[:, GL_OFF:GL_OFF + D_MODEL])
    gg = _sigmoid(z_ref[:, GL_OFF + D_MODEL:GL_OFF + 2 * D_MODEL])
    merged = (ga * ya + gg * yg).astype(BF16)
    h1 = x + jnp.dot(merged, wout_ref[...], preferred_element_type=F32)
    h1_ref[0] = h1

    hn = _rms(h1, nffn_ref[...])
    _store_packed(hn_ref, hn)

    hn_hi = hn.astype(BF16)
    hn_lo = (hn - hn_hi.astype(F32)).astype(BF16)
    by_hi = jnp.dot(hn_hi, wr_ref[...], preferred_element_type=F32)
    by_lo = jnp.dot(hn_lo, wr_ref[:, 0:ROUTER_PAD], preferred_element_type=F32)
    logits = by_hi[:, 0:ROUTER_PAD] + by_hi[:, ROUTER_PAD:2 * ROUTER_PAD] + by_lo + br_ref[...]
    lt = jnp.transpose(logits)[0:N_EXPERTS, :]
    eid = lax.broadcasted_iota(jnp.int32, lt.shape, 0)
    vals, ids = [], []
    for _ in range(TOP_K):
        mx = jnp.max(lt, axis=0, keepdims=True)
        am = jnp.min(jnp.where(lt == mx, eid, N_EXPERTS), axis=0, keepdims=True)
        vals.append(mx)
        ids.append(am)
        lt = jnp.where(eid == am, -jnp.inf, lt)
    ex = [jnp.exp(v - vals[0]) for v in vals]
    tot = ex[0] + ex[1] + ex[2] + ex[3]
    gate_ref[...] = jnp.concatenate([e / tot for e in ex] + [jnp.zeros_like(tot)] * (8 - TOP_K), axis=0)

    @pl.when((pl.program_id(0) == 0) & (i == 0))
    def _():
        cnt_ref[...] = jnp.zeros_like(cnt_ref)

    sel = [eid == a for a in ids]
    member = (sel[0] | sel[1] | sel[2] | sel[3]).astype(F32)
    before = jnp.dot(member.astype(BF16), tri_ref[...], preferred_element_type=F32) + cnt_ref[:, 0:1]
    ranks = [jnp.sum(jnp.where(s_, before, 0.0), axis=0, keepdims=True).astype(jnp.int32) for s_ in sel]
    route_ref[...] = jnp.concatenate(ids + ranks, axis=0)
    cnt_ref[...] = cnt_ref[...] + jnp.sum(member, axis=1, keepdims=True)
    cnt_out_ref[...] = cnt_ref[...]


def _mixer_call(x, rope_tab, prm):
    b, s, d = x.shape
    ts = MIX_TILE
    nt = s // ts
    nb = s // ATTN_BLOCK
    per = ts // ATTN_BLOCK
    const2 = lambda bi, i: (0, 0)
    const3 = lambda bi, i: (0, 0, 0)

    def wspec(arr):
        return pl.BlockSpec(arr.shape, const2 if arr.ndim == 2 else const3, pipeline_mode=pl.Buffered(1))

    in_specs = [
        pl.BlockSpec((1, ts, d), lambda bi, i: (bi, i, 0)),
        pl.BlockSpec((1, ATTN_BLOCK, d), lambda bi, i: (bi, jnp.minimum((i + 1) * per, nb - 1), 0)),
        pl.BlockSpec((ts, 3 * LANES), lambda bi, i: (i, 0)),
        pl.BlockSpec((ATTN_BLOCK, 3 * LANES), lambda bi, i: (jnp.minimum((i + 1) * per, nb - 1), 0)),
    ] + [wspec(a) for a in prm]
    out_shape = [
        jax.ShapeDtypeStruct((b, s, d), F32),
        jax.ShapeDtypeStruct((b * s * SUB, LANES), U32),
        jax.ShapeDtypeStruct((2 * TOP_K, b * s), jnp.int32),
        jax.ShapeDtypeStruct((2 * TOP_K, b * s), F32),
        jax.ShapeDtypeStruct((N_EXPERTS, LANES), F32),
    ]
    out_specs = [
        pl.BlockSpec((1, ts, d), lambda bi, i: (bi, i, 0)),
        pl.BlockSpec((ts * SUB, LANES), lambda bi, i: (bi * nt + i, 0)),
        pl.BlockSpec((2 * TOP_K, ts), lambda bi, i: (0, bi * nt + i)),
        pl.BlockSpec((2 * TOP_K, ts), lambda bi, i: (0, bi * nt + i)),
        pl.BlockSpec((N_EXPERTS, LANES), const2),
    ]
    scratch = [
        pltpu.VMEM((ts, IN_WIDTH), F32),
        pltpu.VMEM((ts, ATTN_WIDTH), BF16),
        pltpu.VMEM((ts + 2 * ATTN_BLOCK, KV_WIDTH), BF16),
        pltpu.VMEM((ts + 2 * ATTN_BLOCK, 2 * N_KV_HEADS * LANES), BF16),
        pltpu.VMEM((ts, ATTN_WIDTH), BF16),
        pltpu.VMEM((ts, GMLP_WIDTH), BF16),
        pltpu.VMEM((N_EXPERTS, LANES), F32),
    ]
    return pl.pallas_call(
        functools.partial(_mixer_kernel, n_blocks_seq=nb),
        grid=(b, nt),
        in_specs=in_specs,
        out_specs=out_specs,
        out_shape=out_shape,
        scratch_shapes=scratch,
        compiler_params=pltpu.CompilerParams(
            dimension_semantics=("arbitrary", "arbitrary"), vmem_limit_bytes=VMEM_LIMIT),
        name="mixer",
    )(x, x, rope_tab, rope_tab, *prm)


def _sc_workers():
    info = plsc.get_sparse_core_info()
    return info.num_cores, info.num_cores * info.num_subcores


def _sc_dispatch(rows2d, slot_rows, n_out_rows):
    n_cores, n_workers = _sc_workers()
    chunk_rows = DISPATCH_TOKENS * SUB
    halves = chunk_rows // SC_CHUNK
    n_chunks = rows2d.shape[0] // chunk_rows
    per_w = n_chunks // n_workers
    idx_rows_per_choice = rows2d.shape[0] // SC_CHUNK
    mesh = plsc.VectorSubcoreMesh(core_axis_name="c", subcore_axis_name="s")

    @functools.partial(
        pl.kernel, mesh=mesh,
        out_type=jax.ShapeDtypeStruct((n_out_rows, LANES), rows2d.dtype),
        scratch_types=[pltpu.VMEM((TOP_K * halves, SC_CHUNK), jnp.int32),
                       pltpu.VMEM((chunk_rows, LANES), rows2d.dtype),
                       pltpu.SemaphoreType.DMA, pltpu.SemaphoreType.DMA],
    )
    def k(src_hbm, idx_hbm, out_hbm, idx_v, rows_v, sem_in, sem_out):
        wid = lax.axis_index("s") * n_cores + lax.axis_index("c")

        @pl.loop(0, per_w)
        def _(j):
            c = wid * per_w + j
            loads = [pltpu.async_copy(src_hbm.at[pl.ds(c * chunk_rows, chunk_rows)], rows_v, sem_in)]
            loads += [
                pltpu.async_copy(idx_hbm.at[pl.ds(kk * idx_rows_per_choice + c * halves, halves)],
                                 idx_v.at[pl.ds(kk * halves, halves)], sem_in)
                for kk in range(TOP_K)]
            for cp in loads:
                cp.wait()
            copies = [
                pltpu.async_copy(rows_v.at[pl.ds((q % halves) * SC_CHUNK, SC_CHUNK)],
                                 out_hbm.at[idx_v.at[q]], sem_out)
                for q in range(TOP_K * halves)]
            for cp in copies:
                cp.wait()

    return k(rows2d, slot_rows)


def _sc_gather(table2d, idx2d):
    n_cores, n_workers = _sc_workers()
    n = idx2d.shape[0] * SC_CHUNK
    per_w = n // n_workers
    step_rows = GATHER_STREAMS * SC_CHUNK
    idx_tile = 8
    mesh = plsc.VectorSubcoreMesh(core_axis_name="c", subcore_axis_name="s")

    @functools.partial(
        pl.kernel, mesh=mesh,
        out_type=jax.ShapeDtypeStruct((n, LANES), table2d.dtype),
        scratch_types=[pltpu.VMEM((idx_tile, SC_CHUNK), jnp.int32),
                       pltpu.VMEM((step_rows, LANES), table2d.dtype),
                       pltpu.SemaphoreType.DMA],
    )
    def k(table_hbm, idx_hbm, out_hbm, idx_v, rows_v, sem):
        wid = lax.axis_index("s") * n_cores + lax.axis_index("c")

        @pl.loop(0, per_w // (idx_tile * SC_CHUNK))
        def _(j):
            row0 = pl.multiple_of(wid * (per_w // SC_CHUNK) + j * idx_tile, idx_tile)
            pltpu.sync_copy(idx_hbm.at[pl.ds(row0, idx_tile)], idx_v)
            for part in range(idx_tile // GATHER_STREAMS):
                copies = [pltpu.async_copy(table_hbm.at[idx_v.at[part * GATHER_STREAMS + q]],
                                           rows_v.at[pl.ds(q * SC_CHUNK, SC_CHUNK)], sem)
                          for q in range(GATHER_STREAMS)]
                for cp in copies:
                    cp.wait()
                pltpu.sync_copy(rows_v, out_hbm.at[pl.ds((row0 + part * GATHER_STREAMS) * SC_CHUNK, step_rows)])

    return k(table2d, idx2d)


def _expert_weight_copies(wgu_hbm, wd_hbm, wgu_buf, wd_buf, sem, expert, slot):
    return (pltpu.make_async_copy(wgu_hbm.at[expert], wgu_buf.at[slot], sem.at[0, slot]),
            pltpu.make_async_copy(wd_hbm.at[expert], wd_buf.at[slot], sem.at[1, slot]))


def _moe_kernel(be_ref, nv_ref, nxt_ref, par_ref, x_ref, wgu_hbm, bgu_ref, wd_hbm, bd_ref, y_ref,
                wgu_buf, wd_buf, sem, wgu_s, wd_s, slab):
    i = pl.program_id(0)
    nv = nv_ref[i]
    slot = par_ref[i]

    cw = MOE_COLS
    n_chunks = D_FF // cw
    copies = functools.partial(_expert_weight_copies, wgu_hbm, wd_hbm, wgu_buf, wd_buf, sem)

    @pl.when(i == 0)
    def _():
        for cp in copies(be_ref[0], slot):
            cp.start()

    @pl.when((i == 0) | (be_ref[i] != be_ref[jnp.maximum(i - 1, 0)]))
    def _():
        for cp in copies(be_ref[i], slot):
            cp.wait()

        @pl.when(nxt_ref[i] >= 0)
        def _():
            for cp in copies(nxt_ref[i], 1 - slot):
                cp.start(priority=1)

        for c in range(n_chunks):
            wgu_s[:, 2 * c * cw:(2 * c + 1) * cw] = wgu_buf[slot, :, c * cw:(c + 1) * cw].astype(BF16)
            wgu_s[:, (2 * c + 1) * cw:(2 * c + 2) * cw] = (
                wgu_buf[slot, :, D_FF + c * cw:D_FF + (c + 1) * cw].astype(BF16))
        half = D_FF // 2
        for c in range(D_MODEL // LANES):
            slab[c, pl.ds(0, half, stride=2), :] = wd_buf[slot, 0:half, c * LANES:(c + 1) * LANES]
            slab[c, pl.ds(1, half, stride=2), :] = wd_buf[slot, half:D_FF, c * LANES:(c + 1) * LANES]
        for c in range(D_MODEL // LANES):
            wd_s[:, c * LANES:(c + 1) * LANES] = slab[c].astype(BF16)

    expert = be_ref[i]

    def run_expert(rows):
        x = _load_packed(x_ref, rows)
        live = lax.broadcasted_iota(jnp.int32, x.shape, 0) < nv
        xe = jnp.where(live, x, 0.0).astype(BF16)
        even = (lax.broadcasted_iota(jnp.int32, (rows, cw), 1) % 2) == 0
        y = bd_ref[expert]

        def gate_up(c):
            return jnp.dot(xe, wgu_s[:, 2 * c * cw:(2 * c + 2) * cw], preferred_element_type=F32)

        h_next = gate_up(0)
        for c in range(n_chunks):
            h = h_next
            if c + 1 < n_chunks:
                h_next = gate_up(c + 1)
            h_a = h[:, 0:cw] + bgu_ref[expert, :, c * cw:(c + 1) * cw]
            h_b = h[:, cw:2 * cw] + bgu_ref[expert, :, D_FF + c * cw:D_FF + (c + 1) * cw]
            gate = jnp.where(even, h_a, pltpu.roll(h_b, 1, 1))
            up = jnp.where(even, pltpu.roll(h_a, cw - 1, 1), h_b)
            gate = jnp.minimum(gate, SWIGLU_LIMIT)
            up = jnp.clip(up, -SWIGLU_LIMIT, SWIGLU_LIMIT)
            act = (up + 1.0) * (gate * jax.nn.sigmoid(SWIGLU_ALPHA * gate))
            y = y + jnp.dot(act.astype(BF16), wd_s[c * cw:(c + 1) * cw, :], preferred_element_type=F32)
        _store_packed(y_ref, y)

    @pl.when(nv > MOE_BLOCK // 2)
    def _():
        run_expert(MOE_BLOCK)

    @pl.when((nv > 0) & (nv <= MOE_BLOCK // 2))
    def _():
        run_expert(MOE_BLOCK // 2)


def _moe_call(block_e, n_valid, next_e, parity, xb, wgu, bgu, wd, bd):
    n_blocks = block_e.shape[0]
    d = D_MODEL
    grid_spec = pltpu.PrefetchScalarGridSpec(
        num_scalar_prefetch=4,
        grid=(n_blocks,),
        in_specs=[
            pl.BlockSpec((MOE_BLOCK * SUB, LANES), lambda i, be, nv, nx, pr: (i, 0)),
            pl.BlockSpec(memory_space=pl.ANY),
            pl.BlockSpec((N_EXPERTS, 1, 2 * D_FF), lambda i, be, nv, nx, pr: (0, 0, 0)),
            pl.BlockSpec(memory_space=pl.ANY),
            pl.BlockSpec((N_EXPERTS, 1, d), lambda i, be, nv, nx, pr: (0, 0, 0)),
        ],
        out_specs=pl.BlockSpec((MOE_BLOCK * SUB, LANES), lambda i, be, nv, nx, pr: (i, 0)),
        scratch_shapes=[pltpu.VMEM((2, d, 2 * D_FF), F32), pltpu.VMEM((2, D_FF, d), F32),
                        pltpu.SemaphoreType.DMA((2, 2)),
                        pltpu.VMEM((d, 2 * D_FF), BF16), pltpu.VMEM((D_FF, d), BF16),
                        pltpu.VMEM((d // LANES, D_FF, LANES), F32)],
    )
    return pl.pallas_call(
        _moe_kernel,
        grid_spec=grid_spec,
        out_shape=jax.ShapeDtypeStruct((n_blocks * MOE_BLOCK * SUB, LANES), U32),
        compiler_params=pltpu.CompilerParams(
            dimension_semantics=("arbitrary",), vmem_limit_bytes=VMEM_LIMIT),
        name="moe",
    )(block_e, n_valid, next_e, parity, xb, wgu, bgu, wd, bd)


def _combine_kernel(g0_ref, g1_ref, g2_ref, g3_ref, h1_ref, gate_ref, p_ref, nple_ref, wpg_ref, wpp_ref, o_ref):
    g = jnp.transpose(jnp.concatenate([gate_ref[...], jnp.zeros((LANES - 2 * TOP_K, CMB_TILE), F32)], axis=0))
    moe = jnp.zeros((CMB_TILE, D_MODEL), F32)
    for kk, g_ref in enumerate((g0_ref, g1_ref, g2_ref, g3_ref)):
        moe = moe + _load_packed(g_ref, CMB_TILE) * g[:, kk:kk + 1]
    h2 = h1_ref[...] + moe
    hp = _rms(h2, nple_ref[...]).astype(BF16)
    gate = jax.nn.sigmoid(jnp.dot(hp, wpg_ref[...], preferred_element_type=F32))
    proj = jnp.dot(p_ref[...].astype(BF16), wpp_ref[...], preferred_element_type=F32)
    o_ref[...] = h2 + gate * proj


def _combine_call(gathered, h1_flat, gates, p_flat, nple, wpg, wpp):
    n_tok, d = h1_flat.shape
    tc = CMB_TILE
    n_tiles = n_tok // tc
    row = lambda i: (i, 0)
    const = lambda i: (0, 0)
    g_specs = [pl.BlockSpec((tc * SUB, LANES), functools.partial(lambda i, kk: (kk * n_tiles + i, 0), kk=kk))
               for kk in range(TOP_K)]
    return pl.pallas_call(
        _combine_kernel,
        grid=(n_tiles,),
        in_specs=g_specs + [
            pl.BlockSpec((tc, d), row),
            pl.BlockSpec((2 * TOP_K, tc), lambda i: (0, i)),
            pl.BlockSpec((tc, PLE_DIM), row),
            pl.BlockSpec((1, d), const),
            pl.BlockSpec((d, d), const),
            pl.BlockSpec((PLE_DIM, d), const),
        ],
        out_specs=pl.BlockSpec((tc, d), row),
        out_shape=jax.ShapeDtypeStruct((n_tok, d), F32),
        compiler_params=pltpu.CompilerParams(
            dimension_semantics=("arbitrary",), vmem_limit_bytes=VMEM_LIMIT),
        name="combine",
    )(gathered, gathered, gathered, gathered, h1_flat, gates, p_flat, nple, wpg, wpp)


def _rope_table(s):
    half = ROPE_DIM // 2
    inv_freq = jnp.power(ROPE_THETA, -jnp.arange(half, dtype=F32) * (2.0 / ROPE_DIM))
    ang = jnp.arange(s, dtype=F32)[:, None] * inv_freq[None, :]
    base = jnp.concatenate([jnp.cos(ang), jnp.sin(ang)], axis=-1)
    place = np.zeros((2 * half, 3 * LANES), np.float32)
    ones = np.zeros((1, 3 * LANES), np.float32)
    for lane in range(3 * LANES):
        kind, d = lane // LANES, lane % HEAD_DIM
        if kind == 0 and d < ROPE_DIM:
            place[d % half, lane] = 1.0
        elif kind == 0:
            ones[0, lane] = 1.0
        elif kind == 1 and d < half:
            place[half + d, lane] = -1.0
        elif kind == 2 and half <= d < ROPE_DIM:
            place[d, lane] = 1.0
    return jnp.dot(base, place, precision=lax.Precision.HIGHEST) + ones


def _routing(route, counts, n_tok):
    ids = route[0:TOP_K]
    ranks = route[TOP_K:2 * TOP_K]
    counts = counts.astype(jnp.int32)
    padded = (counts + MOE_BLOCK - 1) // MOE_BLOCK * MOE_BLOCK
    pends = jnp.cumsum(padded)
    pstarts = pends - padded
    pos = ranks
    for e in range(N_EXPERTS):
        pos = pos + jnp.where(ids == e, pstarts[e], 0)
    n_slots = -(-(n_tok * TOP_K) // MOE_BLOCK) * MOE_BLOCK + N_EXPERTS * MOE_BLOCK
    n_blocks = n_slots // MOE_BLOCK
    first = jnp.arange(n_blocks, dtype=jnp.int32) * MOE_BLOCK
    block_e = jnp.clip(jnp.sum((first[:, None] >= pends[None, :]).astype(jnp.int32), axis=1), 0, N_EXPERTS - 1)
    own = block_e[:, None] == jnp.arange(N_EXPERTS, dtype=jnp.int32)[None, :]
    left = jnp.sum(jnp.where(own, (counts + pstarts)[None, :], 0), axis=1) - first
    n_valid = jnp.clip(left, 0, MOE_BLOCK)
    later = jnp.where(block_e[None, :] > block_e[:, None], block_e[None, :], N_EXPERTS)
    next_e = jnp.min(later, axis=1)
    next_e = jnp.where(next_e == N_EXPERTS, -1, next_e)
    present = jnp.any(own, axis=0)
    runs_before = jnp.sum((present[None, :] & (jnp.arange(N_EXPERTS)[None, :] < block_e[:, None])).astype(jnp.int32),
                          axis=1)
    parity = runs_before % 2
    return pos, (block_e, n_valid.astype(jnp.int32), next_e.astype(jnp.int32), parity.astype(jnp.int32)), n_slots


def _expand_rows(pos):
    spread = (jnp.arange(SUB * LANES, dtype=jnp.int32)[None, :] // SUB
              == jnp.arange(LANES, dtype=jnp.int32)[:, None]).astype(F32)
    wide = jnp.dot(pos.reshape(-1, LANES).astype(F32), spread, precision=lax.Precision.HIGHEST)
    wide = wide.astype(jnp.int32) * SUB + (jnp.arange(SUB * LANES, dtype=jnp.int32) % SUB)[None, :]
    return wide.reshape(-1, LANES)


def _group_forward(x, p_l, rope_tab, mix_prm, moe_prm, ple_prm):
    b, s, d = x.shape
    n_tok = b * s
    h1, hn_rows, route, gates, counts = _mixer_call(x, rope_tab, mix_prm)
    pos, block_info, n_slots = _routing(route, counts[:, 0], n_tok)
    slot_rows = _expand_rows(pos)
    xb = _sc_dispatch(hn_rows, slot_rows, n_slots * SUB)
    y = _moe_call(*block_info, xb, *moe_prm)
    gathered = _sc_gather(y, slot_rows)
    out = _combine_call(gathered, h1.reshape(n_tok, d), gates,
                        p_l.reshape(n_tok, PLE_DIM), *ple_prm)
    return out.reshape(b, s, d)


def kernel(x_prompt, x_sample, p_prompt, p_sample, norm_mix, w_in, q_gain, k_gain, attn_sink, gmlp_v_gain, gmlp_w_s, gmlp_b_s, w_branch, w_out, norm_ffn, w_router, b_router, w_gate_up, b_gate_up, w_down, b_down, norm_ple, w_ple_gate, w_ple_proj):
    depth = norm_mix.shape[0]
    hp, hs = x_prompt, x_sample
    for l in range(depth):
        row = lambda a: a.reshape(1, -1)
        blockdiag = jnp.kron(jnp.eye(N_Q_HEADS, dtype=F32),
                             jnp.full((HEAD_DIM, HEAD_DIM), 1.0 / HEAD_DIM, F32)).astype(BF16)
        sink_heads = attn_sink[l].reshape(N_KV_HEADS, GQA_GROUP)[:, jnp.array(HEAD_ORDER)]
        sink_col = jnp.repeat(sink_heads, ATTN_BLOCK, axis=1)[..., None]
        sink_rows = jnp.where(jnp.arange(LANES) == 0, sink_col, -jnp.inf)
        first_row = (jnp.arange(LANES) == 0)[:, None]
        upper = (jnp.arange(LANES) >= HEAD_DIM)[None, :]
        sink_values = jnp.stack([first_row & upper, first_row & ~upper]).astype(BF16)
        wcat = jnp.transpose(gmlp_w_s[l], (1, 0, 2)).reshape(GMLP_CHUNK, GMLP_GROUPS * GMLP_CHUNK).astype(BF16)
        bias_full = jnp.repeat(gmlp_b_s[l].T, GMLP_GROUP_DIM, axis=1)
        wr = jnp.pad(w_router[l], ((0, 0), (0, ROUTER_PAD - N_EXPERTS)))
        wr_hi = wr.astype(BF16)
        wr_lo = (wr - wr_hi.astype(F32)).astype(BF16)
        br = jnp.pad(b_router[l], (0, ROUTER_PAD - N_EXPERTS)).reshape(1, ROUTER_PAD)
        tri = (jnp.arange(MIX_TILE)[:, None] < jnp.arange(MIX_TILE)[None, :]).astype(BF16)
        mix_prm = (
            row(norm_mix[l]), w_in[l].astype(BF16),
            row(jnp.tile(q_gain[l], N_Q_HEADS)), row(jnp.tile(k_gain[l], N_KV_HEADS)),
            sink_rows, sink_values, row(gmlp_v_gain[l]), blockdiag, wcat, bias_full,
            w_branch[l].astype(BF16), w_out[l].astype(BF16), row(norm_ffn[l]),
            jnp.concatenate([wr_hi, wr_lo], axis=1), br, tri,
        )
        moe_prm = (
            w_gate_up[l], b_gate_up[l].reshape(N_EXPERTS, 1, 2 * D_FF),
            w_down[l], b_down[l].reshape(N_EXPERTS, 1, D_MODEL),
        )
        ple_prm = (row(norm_ple[l]), w_ple_gate[l].astype(BF16), w_ple_proj[l].astype(BF16))
        rope_tab = _rope_table(max(hp.shape[1], hs.shape[1]))
        hp = _group_forward(hp, p_prompt[l], rope_tab, mix_prm, moe_prm, ple_prm)
        hs = _group_forward(hs, p_sample[l], rope_tab, mix_prm, moe_prm, ple_prm)
    return (hp.astype(x_prompt.dtype), hs.astype(x_sample.dtype))
```
